```python
import math
import jax, jax.numpy as jnp
from jax import lax
import numpy as np

D_MODEL = 1024
BATCH = 16
SEQ = 2048
DEPTH = 1
DEC_BATCH = 128
DEC_SEQ = 4
PAST_LEN = 8192
PAGE_SIZE = 128

ATT_HEAD_DIM = 64
ATT_WIDTH = D_MODEL // 2
ATT_HEADS = ATT_WIDTH // ATT_HEAD_DIM
DILATIONS = ((128, 1), (512, 4), (2048, 16))
MAX_WINDOW = 2048
ATT_BLOCK = 128
ROPE_DIM = ATT_HEAD_DIM // 4
ROPE_THETA = 500000.0
SSM_HEAD_DIM = 64
SSM_WIDTH = D_MODEL - ATT_WIDTH
SSM_HEADS = SSM_WIDTH // SSM_HEAD_DIM
SSM_GROUPS = 2
SSM_STATE = 128
CONV_WIDTH = 4
CONV_CH = SSM_WIDTH + 2 * SSM_GROUPS * SSM_STATE
SSM_CHUNK = 128
MIX_WIDTH = ATT_WIDTH + SSM_WIDTH
IN_WIDTH = 3 * ATT_WIDTH + SSM_WIDTH + SSM_HEADS + CONV_CH
N_MEM = 256
MEM_HEADS = 4
MEM_HEAD_DIM = 128
MEM_WIDTH = MEM_HEADS * MEM_HEAD_DIM
N_EXPERTS = 32
TOP_K = 4
D_FF = D_MODEL
SWIGLU_LIMIT = 7.0
SWIGLU_ALPHA = 1.702
MOE_BLOCK = 128
NORM_EPS = 1e-6

kernel_name = 'hymba_dilated_ssd_moe_decode_step'


def rmsnorm(x, gain):
    xf = x.astype(jnp.float32)
    return xf * lax.rsqrt(jnp.mean(xf * xf, axis=-1, keepdims=True) + NORM_EPS) * gain.astype(jnp.float32)


def partial_rope(x, pos):
    half = ROPE_DIM // 2
    inv_freq = jnp.power(ROPE_THETA, -jnp.arange(half, dtype=jnp.float32) / half)
    ang = pos.astype(jnp.float32)[:, None] * inv_freq[None, :]
    cos = jnp.cos(ang)[None, :, None, :]
    sin = jnp.sin(ang)[None, :, None, :]
    x1 = x[..., :half]
    x2 = x[..., half:ROPE_DIM]
    return jnp.concatenate([x1 * cos - x2 * sin, x2 * cos + x1 * sin, x[..., ROPE_DIM:]], axis=-1)


def mixer_projections(h, pos, w_in, q_gain, k_gain):
    b, l, _ = h.shape
    proj = h @ w_in
    cuts = [ATT_WIDTH, 2 * ATT_WIDTH, 3 * ATT_WIDTH, 3 * ATT_WIDTH + SSM_WIDTH,
            3 * ATT_WIDTH + SSM_WIDTH + SSM_HEADS]
    q, k, v, z, dt_raw, xbc = jnp.split(proj, cuts, axis=-1)
    q = partial_rope(rmsnorm(q.reshape(b, l, ATT_HEADS, ATT_HEAD_DIM), q_gain), pos)
    k = partial_rope(rmsnorm(k.reshape(b, l, ATT_HEADS, ATT_HEAD_DIM), k_gain), pos)
    v = v.reshape(b, l, ATT_HEADS, ATT_HEAD_DIM)
    return q, k, v, z, dt_raw, xbc


def combine_by_denominator(parts):
    m_all = parts[0][1]
    for _, m, _ in parts[1:]:
        m_all = jnp.maximum(m_all, m)
    scales = [jnp.exp(m - m_all) for _, m, _ in parts]
    num = sum(a[..., None] * o for a, (o, _, _) in zip(scales, parts))
    den = sum(a * l for a, (_, _, l) in zip(scales, parts))
    return num / den[..., None]


def dilated_band_prompt(q, k, v, window, dilation):
    b, s, h, dh = q.shape
    n = s // dilation
    n_back = window // dilation
    npad = -(-n // ATT_BLOCK) * ATT_BLOCK
    nb = npad // ATT_BLOCK

    def to_blocks(a):
        a = a.reshape(b, n, dilation, h, dh).transpose(0, 2, 1, 3, 4)
        a = jnp.pad(a, ((0, 0), (0, 0), (0, npad - n), (0, 0), (0, 0)))
        return a.reshape(b, dilation, nb, ATT_BLOCK, h, dh)

    def with_prev(a):
        prev = jnp.pad(a[:, :, :-1], ((0, 0), (0, 0), (1, 0), (0, 0), (0, 0), (0, 0)))
        return jnp.concatenate([prev, a], axis=3)

    qb = to_blocks(q)
    kk = with_prev(to_blocks(k))
    vv = with_prev(to_blocks(v))
    sc = jnp.einsum('brnqhd,brnkhd->brnhqk', qb, kk) * (dh ** -0.5)
    qi = jnp.arange(ATT_BLOCK)[:, None] + ATT_BLOCK
    ki = jnp.arange(2 * ATT_BLOCK)[None, :]
    dist = qi - ki
    band = (dist >= 0) & (dist <= n_back)
    first = (jnp.arange(nb)[:, None, None] > 0) | (ki[None] >= ATT_BLOCK)
    mask = band[None] & first
    sc = jnp.where(mask[None, None, :, None], sc, -jnp.inf)
    m = jnp.max(sc, axis=-1)
    p = jnp.exp(sc - m[..., None])
    den = jnp.sum(p, axis=-1)
    num = jnp.einsum('brnhqk,brnkhd->brnqhd', p, vv)

    def back(a):
        a = a.reshape((b, dilation, npad) + a.shape[4:])[:, :, :n]
        a = jnp.swapaxes(a, 1, 2)
        return a.reshape((b, s) + a.shape[3:])

    return back(num), back(jnp.swapaxes(m, 3, 4)), back(jnp.swapaxes(den, 3, 4))


def dilated_attention_prompt(q, k, v):
    return combine_by_denominator([dilated_band_prompt(q, k, v, w, d) for w, d in DILATIONS])


def dilated_gather_sample(q, k_all, v_all, n_buf, window, dilation):
    t = q.shape[1]
    n_back = window // dilation
    qrow = n_buf + jnp.arange(t)
    idx = qrow[:, None] - dilation * jnp.arange(n_back + 1)[None, :]
    valid = idx >= 0
    idx = jnp.maximum(idx, 0)
    kg = jnp.take(k_all, idx, axis=1)
    vg = jnp.take(v_all, idx, axis=1)
    sc = jnp.einsum('bthd,btjhd->bthj', q, kg) * (q.shape[-1] ** -0.5)
    sc = jnp.where(valid[None, :, None, :], sc, -jnp.inf)
    m = jnp.max(sc, axis=-1)
    p = jnp.exp(sc - m[..., None])
    return jnp.einsum('bthj,btjhd->bthd', p, vg), m, jnp.sum(p, axis=-1)


def dilated_attention_sample(q, k, v, k_past, v_past):
    n_buf = k_past.shape[1]
    k_all = jnp.concatenate([k_past.astype(jnp.float32), k], axis=1)
    v_all = jnp.concatenate([v_past.astype(jnp.float32), v], axis=1)
    return combine_by_denominator([dilated_gather_sample(q, k_all, v_all, n_buf, w, d) for w, d in DILATIONS])


def ssd_scan(x, dt, a, b_in, c_in, h0):
    bsz, l, nh, hp = x.shape
    rep = nh // SSM_GROUPS
    chunk = SSM_CHUNK if l % SSM_CHUNK == 0 else l
    nc = l // chunk
    bh = jnp.repeat(b_in, rep, axis=2).reshape(bsz, nc, chunk, nh, SSM_STATE)
    ch = jnp.repeat(c_in, rep, axis=2).reshape(bsz, nc, chunk, nh, SSM_STATE)
    xd = (x * dt[..., None]).reshape(bsz, nc, chunk, nh, hp)
    acum = jnp.cumsum((dt * a).reshape(bsz, nc, chunk, nh), axis=2)
    seg = acum[:, :, :, None, :] - acum[:, :, None, :, :]
    causal = jnp.tril(jnp.ones((chunk, chunk), dtype=bool))[None, None, :, :, None]
    decay = jnp.exp(jnp.where(causal, seg, -jnp.inf))
    scores = jnp.einsum('bcqhn,bckhn->bcqkh', ch, bh) * decay
    y_diag = jnp.einsum('bcqkh,bckhp->bcqhp', scores, xd)
    to_end = jnp.exp(acum[:, :, -1:, :] - acum)
    chunk_states = jnp.einsum('bckhn,bckh,bckhp->bchpn', bh, to_end, xd)
    chunk_decay = jnp.exp(acum[:, :, -1, :])

    def step(h, inp):
        dec, st = inp
        return h * dec[:, :, None, None] + st, h

    h_last, h_in = lax.scan(step, h0, (jnp.moveaxis(chunk_decay, 1, 0), jnp.moveaxis(chunk_states, 1, 0)))
    h_in = jnp.moveaxis(h_in, 0, 1)
    y_off = jnp.einsum('bcqhn,bchpn->bcqhp', ch, h_in) * jnp.exp(acum)[..., None]
    return (y_diag + y_off).reshape(bsz, l, nh, hp), h_last


def ssm_branch(xbc, dt_raw, z, conv_prev, h0, conv_w, conv_b, dt_bias, a_log, d_skip, out_gain):
    bsz, l, _ = xbc.shape
    full = jnp.concatenate([conv_prev.astype(jnp.float32), xbc], axis=1)
    conv = conv_b + full[:, 0:l] * conv_w[0]
    for i in range(1, CONV_WIDTH):
        conv = conv + full[:, i:i + l] * conv_w[i]
    conv_state = full[:, l:]
    u = jax.nn.silu(conv)
    xs, b_in, c_in = jnp.split(u, [SSM_WIDTH, SSM_WIDTH + SSM_GROUPS * SSM_STATE], axis=-1)
    xs = xs.reshape(bsz, l, SSM_HEADS, SSM_HEAD_DIM)
    b_in = b_in.reshape(bsz, l, SSM_GROUPS, SSM_STATE)
    c_in = c_in.reshape(bsz, l, SSM_GROUPS, SSM_STATE)
    dt = jax.nn.softplus(dt_raw + dt_bias)
    a = -jnp.exp(a_log.astype(jnp.float32))
    y, h_last = ssd_scan(xs, dt, a, b_in, c_in, h0.astype(jnp.float32))
    y = y + xs * d_skip[:, None]
    y = rmsnorm(y.reshape(bsz, l, SSM_WIDTH) * jax.nn.silu(z), out_gain)
    return y, conv_state, h_last


def memory_kv(mem, in_gain, w_k, w_v, k_gain):
    bsz, n, _ = mem.shape
    hm = rmsnorm(mem, in_gain)
    k = rmsnorm((hm @ w_k).reshape(bsz, n, MEM_HEADS, MEM_HEAD_DIM), k_gain)
    v = (hm @ w_v).reshape(bsz, n, MEM_HEADS, MEM_HEAD_DIM)
    return k, v


def memory_attend(h, mem_k, mem_v, w_q, q_gain, w_o):
    bsz, l, _ = h.shape
    q = rmsnorm((h @ w_q).reshape(bsz, l, MEM_HEADS, MEM_HEAD_DIM), q_gain)
    sc = jnp.einsum('blhd,bmhd->bhlm', q, mem_k.astype(jnp.float32)) * (MEM_HEAD_DIM ** -0.5)
    p = jax.nn.softmax(sc, axis=-1)
    o = jnp.einsum('bhlm,bmhd->blhd', p, mem_v.astype(jnp.float32)).reshape(bsz, l, MEM_WIDTH)
    return o @ w_o


def moe_ffn(h, w_router, b_router, w_gate_up, b_gate_up, w_down, b_down):
    b, l, d = h.shape
    n_tok = b * l
    tok = h.reshape(n_tok, d)
    logits = tok @ w_router + b_router
    top_val, top_idx = lax.top_k(logits, TOP_K)
    top_w = jax.nn.softmax(top_val, axis=-1)
    n_assign = n_tok * TOP_K
    flat_e = top_idx.reshape(-1)
    flat_t = jnp.repeat(jnp.arange(n_tok, dtype=jnp.int32), TOP_K)
    flat_w = top_w.reshape(-1)
    order = jnp.argsort(flat_e)
    se = flat_e[order]
    counts = jnp.bincount(flat_e, length=N_EXPERTS)
    padded = (counts + MOE_BLOCK - 1) // MOE_BLOCK * MOE_BLOCK
    pend = jnp.cumsum(padded)
    pstart = pend - padded
    start = jnp.cumsum(counts) - counts
    dest = pstart[se] + jnp.arange(n_assign) - start[se]
    n_blocks = -(-(n_assign + N_EXPERTS * (MOE_BLOCK - 1)) // MOE_BLOCK)
    n_rows = n_blocks * MOE_BLOCK
    row_tok = jnp.zeros((n_rows,), jnp.int32).at[dest].set(flat_t[order])
    row_w = jnp.zeros((n_rows,), jnp.float32).at[dest].set(flat_w[order])
    block_e = jnp.minimum(jnp.searchsorted(pend, jnp.arange(n_blocks) * MOE_BLOCK, side='right'), N_EXPERTS - 1)
    xb = tok[row_tok].reshape(n_blocks, MOE_BLOCK, d)

    def expert_block(args):
        xe, e = args
        gu = xe @ w_gate_up[e] + b_gate_up[e]
        g = jnp.minimum(gu[:, 0::2], SWIGLU_LIMIT)
        u = jnp.clip(gu[:, 1::2], -SWIGLU_LIMIT, SWIGLU_LIMIT)
        act = (u + 1.0) * (g * jax.nn.sigmoid(SWIGLU_ALPHA * g))
        return act @ w_down[e] + b_down[e]

    yb = lax.map(expert_block, (xb, block_e))
    out = jnp.zeros((n_tok, d), jnp.float32).at[row_tok].add(yb.reshape(n_rows, d) * row_w[:, None])
    return out.reshape(b, l, d)


def decoder_layer(x, pos, mem_k, mem_v, win_k_past, win_v_past, conv_prev, ssm_prev,
                  norm_mix, w_in, q_gain, k_gain, conv_w, conv_b, dt_bias, a_log, d_skip, ssm_out_gain, w_out,
                  norm_mem, w_mem_q, mem_q_gain, w_mem_o,
                  norm_ffn, w_router, b_router, w_gate_up, b_gate_up, w_down, b_down):
    b, l, _ = x.shape
    h = rmsnorm(x, norm_mix)
    q, k, v, z, dt_raw, xbc = mixer_projections(h, pos, w_in, q_gain, k_gain)
    if win_k_past is None:
        att = dilated_attention_prompt(q, k, v)
    else:
        att = dilated_attention_sample(q, k, v, win_k_past, win_v_past)
    y_ssm, conv_state, ssm_state = ssm_branch(xbc, dt_raw, z, conv_prev, ssm_prev, conv_w, conv_b,
                                              dt_bias, a_log, d_skip, ssm_out_gain)
    mixed = jnp.concatenate([att.reshape(b, l, ATT_WIDTH), y_ssm], axis=-1)
    x = x + mixed @ w_out
    x = x + memory_attend(rmsnorm(x, norm_mem), mem_k, mem_v, w_mem_q, mem_q_gain, w_mem_o)
    x = x + moe_ffn(rmsnorm(x, norm_ffn), w_router, b_router, w_gate_up, b_gate_up, w_down, b_down)
    return x, k, v, conv_state, ssm_state


def setup_inputs(seed: int = 0) -> dict:
    key = jax.random.key(seed)
    it = iter(jax.random.split(key, 48))

    def normal(shape, scale):
        return jax.random.normal(next(it), shape, jnp.float32) * scale

    def gain(shape):
        return 1.0 + normal(shape, 0.02)

    win_buf = min(MAX_WINDOW, PAST_LEN)
    dt_init = jnp.exp(jax.random.uniform(next(it), (DEPTH, SSM_HEADS), jnp.float32,
                                         math.log(1e-3), math.log(1e-1)))
    a_init = jax.random.uniform(next(it), (DEPTH, SSM_HEADS), jnp.float32, 1.0, 16.0)
    return {
        'x_prompt': normal((BATCH, SEQ, D_MODEL), 1.0),
        'x_sample': normal((DEC_BATCH, DEC_SEQ, D_MODEL), 1.0),
        'cache_win_k': normal((DEPTH, DEC_BATCH, win_buf, ATT_HEADS, ATT_HEAD_DIM), 1.0),
        'cache_win_v': normal((DEPTH, DEC_BATCH, win_buf, ATT_HEADS, ATT_HEAD_DIM), 1.0),
        'state_conv': normal((DEPTH, DEC_BATCH, CONV_WIDTH - 1, CONV_CH), 1.0),
        'state_ssm': normal((DEPTH, DEC_BATCH, SSM_HEADS, SSM_HEAD_DIM, SSM_STATE), 0.1),
        'cache_mem_k': normal((DEPTH, DEC_BATCH, N_MEM, MEM_HEADS, MEM_HEAD_DIM), 1.0),
        'cache_mem_v': normal((DEPTH, DEC_BATCH, N_MEM, MEM_HEADS, MEM_HEAD_DIM), 1.0),
        'mem_prompt': normal((BATCH, N_MEM, D_MODEL), 1.0),
        'norm_mix': gain((DEPTH, D_MODEL)),
        'w_in': normal((DEPTH, D_MODEL, IN_WIDTH), D_MODEL ** -0.5),
        'q_gain': gain((DEPTH, ATT_HEAD_DIM)),
        'k_gain': gain((DEPTH, ATT_HEAD_DIM)),
        'conv_w': normal((DEPTH, CONV_WIDTH, CONV_CH), CONV_WIDTH ** -0.5),
        'conv_b': normal((DEPTH, CONV_CH), 0.02),
        'dt_bias': dt_init + jnp.log(-jnp.expm1(-dt_init)),
        'a_log': jnp.log(a_init),
        'd_skip': gain((DEPTH, SSM_HEADS)),
        'ssm_out_gain': gain((DEPTH, SSM_WIDTH)),
        'w_out': normal((DEPTH, MIX_WIDTH, D_MODEL), MIX_WIDTH ** -0.5),
        'norm_mem': gain((DEPTH, D_MODEL)),
        'mem_in_gain': gain((DEPTH, D_MODEL)),
        'w_mem_q': normal((DEPTH, D_MODEL, MEM_WIDTH), D_MODEL ** -0.5),
        'w_mem_k': normal((DEPTH, D_MODEL, MEM_WIDTH), D_MODEL ** -0.5),
        'w_mem_v': normal((DEPTH, D_MODEL, MEM_WIDTH), D_MODEL ** -0.5),
        'mem_q_gain': gain((DEPTH, MEM_HEAD_DIM)),
        'mem_k_gain': gain((DEPTH, MEM_HEAD_DIM)),
        'w_mem_o': normal((DEPTH, MEM_WIDTH, D_MODEL), MEM_WIDTH ** -0.5),
        'norm_ffn': gain((DEPTH, D_MODEL)),
        'w_router': normal((DEPTH, D_MODEL, N_EXPERTS), D_MODEL ** -0.5),
        'b_router': normal((DEPTH, N_EXPERTS), 0.01),
        'w_gate_up': normal((DEPTH, N_EXPERTS, D_MODEL, 2 * D_FF), D_MODEL ** -0.5),
        'b_gate_up': normal((DEPTH, N_EXPERTS, 2 * D_FF), 0.01),
        'w_down': normal((DEPTH, N_EXPERTS, D_FF, D_MODEL), D_FF ** -0.5),
        'b_down': normal((DEPTH, N_EXPERTS, D_MODEL), 0.01),
    }


def reference(x_prompt, x_sample, cache_win_k, cache_win_v, state_conv, state_ssm, cache_mem_k, cache_mem_v,
              mem_prompt, norm_mix, w_in, q_gain, k_gain, conv_w, conv_b, dt_bias, a_log, d_skip, ssm_out_gain,
              w_out, norm_mem, mem_in_gain, w_mem_q, w_mem_k, w_mem_v, mem_q_gain, mem_k_gain, w_mem_o,
              norm_ffn, w_router, b_router, w_gate_up, b_gate_up, w_down, b_down):
    f32 = jnp.float32
    xp = x_prompt.astype(f32)
    xs = x_sample.astype(f32)
    bp, lp = xp.shape[0], xp.shape[1]
    ls = xs.shape[1]
    pos_p = jnp.arange(lp, dtype=jnp.int32)
    pos_s = PAST_LEN + jnp.arange(ls, dtype=jnp.int32)
    keep = min(MAX_WINDOW, lp)
    wk_p, wv_p, wk_s, wv_s, cv_p, cv_s, st_p, st_s, mk_l, mv_l = ([] for _ in range(10))
    for i in range(DEPTH):
        lw = (norm_mix[i], w_in[i], q_gain[i], k_gain[i], conv_w[i], conv_b[i], dt_bias[i], a_log[i],
              d_skip[i], ssm_out_gain[i], w_out[i], norm_mem[i], w_mem_q[i], mem_q_gain[i], w_mem_o[i],
              norm_ffn[i], w_router[i], b_router[i], w_gate_up[i], b_gate_up[i], w_down[i], b_down[i])
        mk_p, mv_p = memory_kv(mem_prompt, mem_in_gain[i], w_mem_k[i], w_mem_v[i], mem_k_gain[i])
        xp, kp, vp, cp, sp = decoder_layer(
            xp, pos_p, mk_p, mv_p, None, None,
            jnp.zeros((bp, CONV_WIDTH - 1, CONV_CH), f32),
            jnp.zeros((bp, SSM_HEADS, SSM_HEAD_DIM, SSM_STATE), f32), *lw)
        xs, k_new, v_new, c_new, s_new = decoder_layer(
            xs, pos_s, cache_mem_k[i], cache_mem_v[i], cache_win_k[i], cache_win_v[i],
            state_conv[i], state_ssm[i], *lw)
        wk_p.append(kp[:, lp - keep:])
        wv_p.append(vp[:, lp - keep:])
        wk_s.append(k_new)
        wv_s.append(v_new)
        cv_p.append(cp)
        cv_s.append(c_new)
        st_p.append(sp)
        st_s.append(s_new)
        mk_l.append(mk_p)
        mv_l.append(mv_p)
    return (xp.astype(x_prompt.dtype), xs.astype(x_sample.dtype),
            jnp.stack(wk_p), jnp.stack(wv_p), jnp.stack(wk_s), jnp.stack(wv_s),
            jnp.stack(cv_p), jnp.stack(cv_s), jnp.stack(st_p), jnp.stack(st_s),
            jnp.stack(mk_l), jnp.stack(mv_l))
```

```python
import functools
import math

import numpy as np
import jax
import jax.numpy as jnp
from jax import lax
from jax.experimental import pallas as pl
from jax.experimental.pallas import tpu as pltpu

F32 = jnp.float32
BF16 = jnp.bfloat16

ATT_HEAD_DIM = 64
DILATIONS = ((128, 1), (512, 4), (2048, 16))
ATT_BLOCK = 128
ROPE_DIM = ATT_HEAD_DIM // 4
ROPE_THETA = 500000.0
PAST_LEN = 8192
SSM_HEAD_DIM = 64
SSM_GROUPS = 2
SSM_STATE = 128
CONV_WIDTH = 4
SSM_CHUNK = 128
MEM_HEAD_DIM = 128
TOP_K = 4
SWIGLU_LIMIT = 7.0
SWIGLU_ALPHA = 1.702
NORM_EPS = 1e-6

LANES = 128
SUBLANES = 8
VMEM_LIMIT = 56 * 1024 * 1024

TOKEN_TILE = 512
MOE_TILE = 256
COMBINE_TILE = 128
NEG = -1e30


def _cparams(sem):
    return pltpu.CompilerParams(dimension_semantics=sem, vmem_limit_bytes=VMEM_LIMIT)


def _rms(x, gain):
    return x * lax.rsqrt(jnp.mean(x * x, axis=-1, keepdims=True) + NORM_EPS) * gain


def _dot(a, b):
    return jnp.dot(a, b, preferred_element_type=F32)


def _dot_nt(a, b):
    return lax.dot_general(a, b, (((1,), (1,)), ((), ())), preferred_element_type=F32)


def _dot_tn(a, b):
    return lax.dot_general(a, b, (((0,), (0,)), ((), ())), preferred_element_type=F32)


def _dot_f32(a, b):
    return jnp.dot(a, b, preferred_element_type=F32, precision=lax.Precision.HIGHEST)


def _proj_kernel(x_ref, g_ref, w_ref, qg_ref, kg_ref, seg_ref, cos_ref, s1_ref, s2_ref,
                 q_ref, k_ref, v_ref, z_ref, xbc_ref, dt_ref, *, att_w, ssm_w, conv_ch):
    h = _rms(x_ref[...], g_ref[...]).astype(BF16)
    seg = seg_ref[...]
    cos, s1, s2 = cos_ref[...], s1_ref[...], s2_ref[...]

    def head_norm_rope(t, gain):
        sq = t * t
        hi = sq.astype(BF16)
        lo = (sq - hi.astype(F32)).astype(BF16)
        ms = (_dot(hi, seg) + _dot(lo, seg)) * (1.0 / ATT_HEAD_DIM)
        tn = t * lax.rsqrt(ms + NORM_EPS) * gain
        half = ROPE_DIM // 2
        return (tn * cos + pltpu.roll(tn, half, 1) * s1
                + pltpu.roll(tn, att_w - half, 1) * s2)

    q = head_norm_rope(_dot(h, w_ref[:, 0:att_w]), qg_ref[...])
    q_ref[...] = q * (ATT_HEAD_DIM ** -0.5)
    k_ref[...] = head_norm_rope(_dot(h, w_ref[:, att_w:2 * att_w]), kg_ref[...])
    v_ref[...] = _dot(h, w_ref[:, 2 * att_w:3 * att_w])
    o = 3 * att_w
    z_ref[...] = _dot(h, w_ref[:, o:o + ssm_w])
    o += ssm_w
    xbc_ref[...] = _dot(h, w_ref[:, o:o + conv_ch])
    o += conv_ch
    dt_ref[...] = _dot(h, w_ref[:, o:o + LANES])


def _rope_tables(pos, n_heads):
    half = ROPE_DIM // 2
    inv_freq = jnp.power(ROPE_THETA, -jnp.arange(half, dtype=F32) / half)
    ang = pos.astype(F32)[:, None] * inv_freq[None, :]
    cos, sin = jnp.cos(ang), jnp.sin(ang)
    n = pos.shape[0]
    rest = ATT_HEAD_DIM - ROPE_DIM
    c = jnp.concatenate([cos, cos, jnp.ones((n, rest), F32)], axis=-1)
    s1 = jnp.concatenate([jnp.zeros((n, half), F32), sin, jnp.zeros((n, rest), F32)], axis=-1)
    s2 = jnp.concatenate([-sin, jnp.zeros((n, half + rest), F32)], axis=-1)
    return tuple(jnp.tile(t, (1, n_heads)) for t in (c, s1, s2))


def _projections(x, norm_mix, w_in, q_gain, k_gain, seq, dec_seq, past_len, n_prompt):
    n, d = x.shape
    d_half = d // 2
    att_w, ssm_w = d_half, d - d_half
    n_heads = att_w // ATT_HEAD_DIM
    ssm_heads = ssm_w // SSM_HEAD_DIM
    conv_ch = ssm_w + 2 * SSM_GROUPS * SSM_STATE
    tm = TOKEN_TILE
    assert n % tm == 0 and n_prompt % tm == 0 and seq % tm == 0 and (n - n_prompt) == tm
    assert tm % dec_seq == 0
    c0 = 3 * att_w + ssm_w
    w = jnp.concatenate([w_in[:, :c0], w_in[:, c0 + ssm_heads:],
                         w_in[:, c0:c0 + ssm_heads],
                         jnp.zeros((d, LANES - ssm_heads), w_in.dtype)], axis=1).astype(BF16)
    wn = w.shape[1]
    pos = jnp.concatenate([jnp.arange(seq, dtype=jnp.int32),
                           past_len + jnp.arange(tm, dtype=jnp.int32) % dec_seq])
    cos, s1, s2 = _rope_tables(pos, n_heads)
    tiles_per_seq = seq // tm
    n_prompt_tiles = n_prompt // tm
    head_id = np.arange(att_w) // ATT_HEAD_DIM
    seg = jnp.asarray(head_id[:, None] == head_id[None, :], BF16)

    def tab_map(i):
        return (jnp.where(i < n_prompt_tiles, i % tiles_per_seq, tiles_per_seq), 0)

    row = lambda i: (i, 0)
    fix = lambda i: (0, 0)
    tab = pl.BlockSpec((tm, att_w), tab_map)
    kern = functools.partial(_proj_kernel, att_w=att_w, ssm_w=ssm_w, conv_ch=conv_ch)
    return pl.pallas_call(
        kern,
        grid=(n // tm,),
        in_specs=[pl.BlockSpec((tm, d), row), pl.BlockSpec((1, d), fix),
                  pl.BlockSpec((d, wn), fix), pl.BlockSpec((1, att_w), fix),
                  pl.BlockSpec((1, att_w), fix), pl.BlockSpec((att_w, att_w), fix),
                  tab, tab, tab],
        out_specs=[pl.BlockSpec((tm, att_w), row), pl.BlockSpec((tm, att_w), row),
                   pl.BlockSpec((tm, att_w), row), pl.BlockSpec((tm, ssm_w), row),
                   pl.BlockSpec((tm, conv_ch), row), pl.BlockSpec((tm, LANES), row)],
        out_shape=[jax.ShapeDtypeStruct((n, att_w), F32), jax.ShapeDtypeStruct((n, att_w), F32),
                   jax.ShapeDtypeStruct((n, att_w), F32), jax.ShapeDtypeStruct((n, ssm_w), F32),
                   jax.ShapeDtypeStruct((n, conv_ch), F32), jax.ShapeDtypeStruct((n, LANES), F32)],
        compiler_params=_cparams(("parallel",)),
        name="proj",
    )(x, norm_mix.reshape(1, d), w, jnp.tile(q_gain, n_heads).reshape(1, att_w),
      jnp.tile(k_gain, n_heads).reshape(1, att_w), seg, cos, s1, s2)


def _attn_prompt_kernel(q_ref, k_ref, v_ref, o_ref, num_ref, m_ref, den_ref, *, seq):
    blk = ATT_BLOCK
    lane = lax.broadcasted_iota(jnp.int32, (blk, LANES), 1)
    head0 = lane < ATT_HEAD_DIM
    qi = lax.broadcasted_iota(jnp.int32, (blk, 2 * blk), 0) + blk
    ki = lax.broadcasted_iota(jnp.int32, (blk, 2 * blk), 1)
    dist = qi - ki
    band = (dist >= 0) & (dist <= blk)
    own = ki >= blk

    for di, (window, dil) in enumerate(DILATIONS):
        assert window // dil == blk
        nb = seq // dil // blk

        def body(i, carry, dil=dil, nb=nb, di=di):
            r = i // nb
            j = i % nb
            if dil > 1:
                start = r + dil * blk * j
                prev = r + dil * blk * jnp.maximum(j - 1, 0)
                rows = pl.ds(start, blk, stride=dil)
                prows = pl.ds(prev, blk, stride=dil)
            else:
                rows = pl.ds(pl.multiple_of(blk * j, blk), blk)
                prows = pl.ds(pl.multiple_of(blk * jnp.maximum(j - 1, 0), blk), blk)
            qb = q_ref[rows, :].astype(BF16)
            k2 = jnp.concatenate([k_ref[prows, :], k_ref[rows, :]], axis=0).astype(BF16)
            v2 = jnp.concatenate([v_ref[prows, :], v_ref[rows, :]], axis=0).astype(BF16)
            mask = band & (own | (j > 0))
            res = []
            for hd in range(2):
                sel = head0 if hd == 0 else ~head0
                qh = jnp.where(sel, qb, jnp.zeros_like(qb))
                s = jnp.where(mask, _dot_nt(qh, k2), -jnp.inf)
                m = jnp.max(s, axis=-1, keepdims=True)
                p = jnp.exp(s - m)
                den = jnp.sum(p, axis=-1, keepdims=True)
                res.append((_dot(p.astype(BF16), v2), m, den))
            num_ref[di, rows, :] = jnp.where(head0, res[0][0], res[1][0])
            m_ref[di, rows, :] = jnp.where(head0, res[0][1], res[1][1])
            den_ref[di, rows, :] = jnp.where(head0, res[0][2], res[1][2])
            return carry

        lax.fori_loop(0, dil * nb, body, 0)

    m_all = jnp.maximum(jnp.maximum(m_ref[0], m_ref[1]), m_ref[2])
    num = jnp.zeros((seq, LANES), F32)
    den = jnp.zeros((seq, LANES), F32)
    for di in range(len(DILATIONS)):
        a = jnp.exp(m_ref[di] - m_all)
        num = num + a * num_ref[di]
        den = den + a * den_ref[di]
    o_ref[...] = (num / den).astype(o_ref.dtype)


def _attn_prompt(q, k, v, batch, seq):
    att_w = q.shape[1]
    pairs = att_w // LANES
    nd = len(DILATIONS)
    blk = pl.BlockSpec((seq, LANES), lambda b, h: (b, h))
    return pl.pallas_call(
        functools.partial(_attn_prompt_kernel, seq=seq),
        grid=(batch, pairs),
        in_specs=[blk, blk, blk],
        out_specs=blk,
        out_shape=jax.ShapeDtypeStruct((batch * seq, att_w), BF16),
        scratch_shapes=[pltpu.VMEM((nd, seq, LANES), F32), pltpu.VMEM((nd, seq, LANES), F32),
                        pltpu.VMEM((nd, seq, LANES), F32)],
        compiler_params=_cparams(("parallel", "parallel")),
        name="attn_prompt",
    )(q, k, v)


def _attn_sample_kernel(q_ref, kn_ref, vn_ref, kc_ref, vc_ref, cc_ref, cn_ref, o_ref, *,
                        dec_seq, n_heads):
    w = q_ref.shape[-1]
    rows = dec_seq * n_heads
    q = q_ref[0]
    qm = jnp.concatenate([jnp.broadcast_to(q[t:t + 1], (n_heads, w)) for t in range(dec_seq)], axis=0)
    lane_head = lax.broadcasted_iota(jnp.int32, (rows, w), 1) // ATT_HEAD_DIM
    row_head = lax.broadcasted_iota(jnp.int32, (rows, w), 0) % n_heads
    own = lane_head == row_head
    qm = jnp.where(own, qm, 0.0).astype(BF16)
    zpad = jnp.zeros((SUBLANES - dec_seq, w), F32)
    kn = jnp.concatenate([kn_ref[0], zpad], axis=0).astype(BF16)
    vn = jnp.concatenate([vn_ref[0], zpad], axis=0).astype(BF16)
    cc, cn = cc_ref[...], cn_ref[...]
    s_c = jnp.where(cc > 0, _dot_nt(qm, kc_ref[0].astype(BF16)), -jnp.inf)
    s_n = jnp.where(cn > 0, _dot_nt(qm, kn), -jnp.inf)
    m = jnp.maximum(jnp.max(s_c, axis=-1, keepdims=True), jnp.max(s_n, axis=-1, keepdims=True))
    p_c = cc * jnp.exp(s_c - m)
    p_n = cn * jnp.exp(s_n - m)
    den = jnp.sum(p_c, axis=-1, keepdims=True) + jnp.sum(p_n, axis=-1, keepdims=True)
    o = _dot(p_c.astype(BF16), vc_ref[0].astype(BF16)) + _dot(p_n.astype(BF16), vn)
    o = jnp.where(own, o / den, 0.0)
    o_ref[0] = jnp.sum(o.reshape(dec_seq, n_heads, w), axis=1).astype(o_ref.dtype)


def _attn_sample(q, k_new, v_new, k_cache, v_cache):
    b, t, w = q.shape
    n_buf = k_cache.shape[1]
    n_heads = w // ATT_HEAD_DIM
    assert n_heads == SUBLANES and t <= SUBLANES
    assert n_buf >= max(win for win, _ in DILATIONS)

    def count(dist):
        return sum(((dist >= 0) & (dist % dil == 0) & (dist <= win)).astype(np.float32)
                   for win, dil in DILATIONS)

    tq = np.repeat(np.arange(t), n_heads)[:, None]
    cc = count(n_buf + tq - np.arange(n_buf)[None, :])
    jn = np.arange(SUBLANES)[None, :]
    cn = np.where(jn < t, count(tq - jn), 0.0).astype(np.float32)
    rows = t * n_heads
    new = pl.BlockSpec((1, t, w), lambda i: (i, 0, 0))
    cache = pl.BlockSpec((1, n_buf, w), lambda i: (i, 0, 0))
    return pl.pallas_call(
        functools.partial(_attn_sample_kernel, dec_seq=t, n_heads=n_heads),
        grid=(b,),
        in_specs=[new, new, new, cache, cache,
                  pl.BlockSpec((rows, n_buf), lambda i: (0, 0)),
                  pl.BlockSpec((rows, SUBLANES), lambda i: (0, 0))],
        out_specs=new,
        out_shape=jax.ShapeDtypeStruct((b, t, w), BF16),
        compiler_params=_cparams(("parallel",)),
        name="attn_sample",
    )(q, k_new, v_new, k_cache, v_cache, jnp.asarray(cc), jnp.asarray(cn))


def _softplus(x):
    return jnp.maximum(x, 0.0) + jnp.log1p(jnp.exp(-jnp.abs(x)))


def _silu(x):
    return x * jax.nn.sigmoid(x)


def _ssd_kernel(xbc_ref, dt_ref, dtt_ref, z_ref, cp_ref, h0_ref, cw_ref, cb_ref, dtb_ref, dtbt_ref,
                al_ref, alt_ref, dsk_ref, og_ref, y_ref, hl_ref, xpad_ref, h_ref, *,
                q, valid, ssm_w, n_pairs):
    c = pl.program_id(1)
    tail = CONV_WIDTH - 1

    @pl.when(c == 0)
    def _():
        h_ref[...] = h0_ref[0]
        xpad_ref[0:SUBLANES, :] = jnp.zeros((SUBLANES, xpad_ref.shape[1]), F32)
        xpad_ref[SUBLANES - tail:SUBLANES, :] = cp_ref[0]

    x = xbc_ref[0]
    xpad_ref[SUBLANES:SUBLANES + q, :] = x
    conv = cb_ref[...]
    for j in range(CONV_WIDTH):
        o = SUBLANES - tail + j
        conv = conv + xpad_ref[o:o + q, :] * cw_ref[j:j + 1, :]
    xpad_ref[0:SUBLANES, :] = x[q - SUBLANES:q, :]
    u = _silu(conv)
    xs = u[:, :ssm_w]
    gw = SSM_STATE
    bm = u[:, ssm_w:ssm_w + SSM_GROUPS * gw].astype(BF16)
    cm = u[:, ssm_w + SSM_GROUPS * gw:].astype(BF16)

    dt = _softplus(dt_ref[0] + dtb_ref[...])
    dtt = _softplus(dtt_ref[0] + dtbt_ref[...])
    if valid < q:
        dt = jnp.where(lax.broadcasted_iota(jnp.int32, dt.shape, 0) < valid, dt, 0.0)
        dtt = jnp.where(lax.broadcasted_iota(jnp.int32, dtt.shape, 1) < valid, dtt, 0.0)
    ri = lax.broadcasted_iota(jnp.int32, (q, q), 0)
    ci = lax.broadcasted_iota(jnp.int32, (q, q), 1)
    causal = ci <= ri
    acum = _dot_f32(causal.astype(F32), dt * -jnp.exp(al_ref[...]))
    acumt = _dot_f32(dtt * -jnp.exp(alt_ref[...]), (ri <= ci).astype(F32))

    left = lax.broadcasted_iota(jnp.int32, (q, LANES), 1) < SSM_HEAD_DIM
    top = lax.broadcasted_iota(jnp.int32, (LANES, LANES), 0) < SSM_HEAD_DIM
    rep = 2 * n_pairs // SSM_GROUPS
    gmat = [_dot_nt(cm[:, g * gw:(g + 1) * gw], bm[:, g * gw:(g + 1) * gw]) for g in range(SSM_GROUPS)]
    ys = []
    for pr in range(n_pairs):
        ha, hb = 2 * pr, 2 * pr + 1
        g = ha // rep
        bg = bm[:, g * gw:(g + 1) * gw]
        cg = cm[:, g * gw:(g + 1) * gw]
        xpair = xs[:, pr * LANES:(pr + 1) * LANES]
        xd = xpair * jnp.where(left, dt[:, ha:ha + 1], dt[:, hb:hb + 1])
        xdb = xd.astype(BF16)
        yd = []
        for hh in (ha, hb):
            seg = acum[:, hh:hh + 1] - acumt[hh:hh + 1, :]
            decay = jnp.exp(jnp.where(causal, seg, -jnp.inf))
            yd.append(_dot((gmat[g] * decay).astype(BF16), xdb))
        ac = jnp.where(left, acum[:, ha:ha + 1], acum[:, hb:hb + 1])
        hprev = h_ref[pr]
        y_off = _dot_nt(cg, hprev.astype(BF16)) * jnp.exp(ac)
        to_end = jnp.exp(ac[q - 1:q, :] - ac)
        upd = _dot_tn((xd * to_end).astype(BF16), bg)
        cdec = jnp.where(top, jnp.exp(acum[q - 1:q, ha:ha + 1]), jnp.exp(acum[q - 1:q, hb:hb + 1]))
        h_ref[pr] = hprev * cdec + upd
        ys.append(jnp.where(left, yd[0], yd[1]) + y_off
                  + xpair * dsk_ref[:, pr * LANES:(pr + 1) * LANES])
    y = jnp.concatenate(ys, axis=1) * _silu(z_ref[0])
    y_ref[0] = _rms(y, og_ref[...]).astype(y_ref.dtype)

    @pl.when(c == pl.num_programs(1) - 1)
    def _():
        hl_ref[0] = h_ref[...]


def _ssd(xbc, dt_raw, z, conv_prev, h0, conv_w, conv_b, dt_bias, a_log, d_skip, out_gain, valid_len):
    b, l, conv_ch = xbc.shape
    ssm_w = z.shape[-1]
    n_heads = ssm_w // SSM_HEAD_DIM
    n_pairs = n_heads // 2
    assert 2 * SSM_HEAD_DIM == LANES and SSM_STATE == LANES and n_heads <= SUBLANES
    assert (n_heads // SSM_GROUPS) % 2 == 0
    q = SSM_CHUNK if l % SSM_CHUNK == 0 else l
    assert l % q == 0 and q % SUBLANES == 0 and (valid_len == l or q == l)
    nc = l // q
    dtt = jnp.swapaxes(dt_raw[..., :n_heads], 1, 2)
    pad_h = lambda a: jnp.pad(a.reshape(1, n_heads), ((0, 0), (0, LANES - n_heads)))
    per_lane = lambda a: jnp.repeat(a, SSM_HEAD_DIM).reshape(1, ssm_w)
    h0p = h0.reshape(b, n_pairs, LANES, SSM_STATE).astype(F32)
    tail = CONV_WIDTH - 1
    tok = lambda w: pl.BlockSpec((1, q, w), lambda i, j: (i, j, 0))
    fix2 = lambda r, w: pl.BlockSpec((r, w), lambda i, j: (0, 0))
    st = pl.BlockSpec((1, n_pairs, LANES, SSM_STATE), lambda i, j: (i, 0, 0, 0))
    kern = functools.partial(_ssd_kernel, q=q, valid=valid_len if q == l else q, ssm_w=ssm_w, n_pairs=n_pairs)
    y, h_last = pl.pallas_call(
        kern,
        grid=(b, nc),
        in_specs=[tok(conv_ch), tok(LANES), pl.BlockSpec((1, n_heads, q), lambda i, j: (i, 0, j)),
                  tok(ssm_w), pl.BlockSpec((1, tail, conv_ch), lambda i, j: (i, 0, 0)), st,
                  fix2(CONV_WIDTH, conv_ch), fix2(1, conv_ch), fix2(1, LANES), fix2(n_heads, 1),
                  fix2(1, LANES), fix2(n_heads, 1), fix2(1, ssm_w), fix2(1, ssm_w)],
        out_specs=[tok(ssm_w), st],
        out_shape=[jax.ShapeDtypeStruct((b, l, ssm_w), BF16),
                   jax.ShapeDtypeStruct((b, n_pairs, LANES, SSM_STATE), F32)],
        scratch_shapes=[pltpu.VMEM((q + SUBLANES, conv_ch), F32),
                        pltpu.VMEM((n_pairs, LANES, SSM_STATE), F32)],
        compiler_params=_cparams(("parallel", "arbitrary")),
        name="ssd",
    )(xbc, dt_raw, dtt, z, conv_prev.astype(F32), h0p, conv_w, conv_b.reshape(1, conv_ch),
      pad_h(dt_bias), dt_bias.reshape(n_heads, 1), pad_h(a_log), a_log.reshape(n_heads, 1),
      per_lane(d_skip), out_gain.reshape(1, ssm_w))
    return y, h_last.reshape(b, n_heads, SSM_HEAD_DIM, SSM_STATE)


def _head_rms_store(dst_ref, t, gain, scale=None):
    for hd in range(t.shape[1] // MEM_HEAD_DIM):
        sl = slice(hd * MEM_HEAD_DIM, (hd + 1) * MEM_HEAD_DIM)
        r = _rms(t[:, sl], gain)
        if scale is not None:
            r = r * scale
        dst_ref[:, sl] = r.astype(dst_ref.dtype)


def _mem_kv_kernel(m_ref, g_ref, wk_ref, wv_ref, kg_ref, k_ref, v_ref):
    h = _rms(m_ref[...], g_ref[...]).astype(BF16)
    _head_rms_store(k_ref, _dot(h, wk_ref[...]), kg_ref[...])
    v_ref[...] = _dot(h, wv_ref[...])


def _mem_kv(mem, in_gain, w_k, w_v, k_gain):
    n, d = mem.shape
    mw = w_k.shape[1]
    tm = TOKEN_TILE
    assert n % tm == 0 and MEM_HEAD_DIM == LANES
    row = lambda i: (i, 0)
    fix = lambda i: (0, 0)
    return pl.pallas_call(
        _mem_kv_kernel,
        grid=(n // tm,),
        in_specs=[pl.BlockSpec((tm, d), row), pl.BlockSpec((1, d), fix), pl.BlockSpec((d, mw), fix),
                  pl.BlockSpec((d, mw), fix), pl.BlockSpec((1, MEM_HEAD_DIM), fix)],
        out_specs=[pl.BlockSpec((tm, mw), row), pl.BlockSpec((tm, mw), row)],
        out_shape=[jax.ShapeDtypeStruct((n, mw), F32), jax.ShapeDtypeStruct((n, mw), F32)],
        compiler_params=_cparams(("parallel",)),
        name="mem_kv",
    )(mem, in_gain.reshape(1, d), w_k.astype(BF16), w_v.astype(BF16), k_gain.reshape(1, MEM_HEAD_DIM))


def _out_proj_kernel(x_ref, att_ref, ys_ref, wo_ref, g_ref, wq_ref, qg_ref, x1_ref, q_ref):
    att_w = att_ref.shape[1]
    x1 = x_ref[...] + _dot(att_ref[...], wo_ref[0:att_w, :]) + _dot(ys_ref[...], wo_ref[att_w:, :])
    x1_ref[...] = x1
    h = _rms(x1, g_ref[...]).astype(BF16)
    _head_rms_store(q_ref, _dot(h, wq_ref[...]), qg_ref[...], MEM_HEAD_DIM ** -0.5)


def _out_proj(x, att, yssm, w_out, norm_mem, w_mem_q, mem_q_gain):
    n, d = x.shape
    att_w, ssm_w = att.shape[1], yssm.shape[1]
    mw = w_mem_q.shape[1]
    tm = TOKEN_TILE
    row = lambda i: (i, 0)
    fix = lambda i: (0, 0)
    return pl.pallas_call(
        _out_proj_kernel,
        grid=(n // tm,),
        in_specs=[pl.BlockSpec((tm, d), row), pl.BlockSpec((tm, att_w), row), pl.BlockSpec((tm, ssm_w), row),
                  pl.BlockSpec((att_w + ssm_w, d), fix), pl.BlockSpec((1, d), fix),
                  pl.BlockSpec((d, mw), fix), pl.BlockSpec((1, MEM_HEAD_DIM), fix)],
        out_specs=[pl.BlockSpec((tm, d), row), pl.BlockSpec((tm, mw), row)],
        out_shape=[jax.ShapeDtypeStruct((n, d), F32), jax.ShapeDtypeStruct((n, mw), BF16)],
        compiler_params=_cparams(("parallel",)),
        name="out_proj",
    )(x, att, yssm, w_out.astype(BF16), norm_mem.reshape(1, d), w_mem_q.astype(BF16),
      mem_q_gain.reshape(1, MEM_HEAD_DIM))


def _mem_attn_kernel(q_ref, k_ref, v_ref, o_ref):
    for hd in range(q_ref.shape[-1] // MEM_HEAD_DIM):
        sl = slice(hd * MEM_HEAD_DIM, (hd + 1) * MEM_HEAD_DIM)
        s = _dot_nt(q_ref[0, :, sl], k_ref[0, :, sl].astype(BF16))
        p = jnp.exp(s - jnp.max(s, axis=-1, keepdims=True))
        den = jnp.sum(p, axis=-1, keepdims=True)
        o_ref[0, :, sl] = (_dot(p.astype(BF16), v_ref[0, :, sl].astype(BF16)) / den).astype(o_ref.dtype)


def _mem_attn(q, mem_k, mem_v, tq):
    b, l, w = q.shape
    n_mem = mem_k.shape[1]
    assert l % tq == 0
    qs = pl.BlockSpec((1, tq, w), lambda i, j: (i, j, 0))
    ms = pl.BlockSpec((1, n_mem, w), lambda i, j: (i, 0, 0))
    return pl.pallas_call(
        _mem_attn_kernel,
        grid=(b, l // tq),
        in_specs=[qs, ms, ms],
        out_specs=qs,
        out_shape=jax.ShapeDtypeStruct((b, l, w), BF16),
        compiler_params=_cparams(("parallel", "parallel")),
        name="mem_attn",
    )(q, mem_k, mem_v)


def _post_kernel(x1_ref, o_ref, wo_ref, g_ref, wrh_ref, wrl_ref, br_ref, x2_ref, hf_ref, idx_ref, w_ref):
    x2 = x1_ref[...] + _dot(o_ref[...], wo_ref[...])
    x2_ref[...] = x2
    hf = _rms(x2, g_ref[...])
    hf_ref[...] = hf
    hi = hf.astype(BF16)
    lo = (hf - hi.astype(F32)).astype(BF16)
    logits = (_dot(hi, wrh_ref[...]) + _dot(lo, wrh_ref[...]) + _dot(hi, wrl_ref[...])) + br_ref[...]
    lane = lax.broadcasted_iota(jnp.int32, logits.shape, 1)
    vals, idxs = [], []
    for _ in range(TOP_K):
        m = jnp.max(logits, axis=-1, keepdims=True)
        ix = jnp.min(jnp.where(logits == m, lane, LANES), axis=-1, keepdims=True)
        vals.append(m)
        idxs.append(ix)
        logits = jnp.where(lane == ix, -jnp.inf, logits)
    es = [jnp.exp(v - vals[0]) for v in vals]
    tot = es[0]
    for e in es[1:]:
        tot = tot + e
    wout = jnp.zeros(logits.shape, F32)
    iout = jnp.zeros(logits.shape, jnp.int32)
    for kk in range(TOP_K):
        wout = jnp.where(lane == kk, es[kk] / tot, wout)
        iout = jnp.where(lane == kk, idxs[kk], iout)
    w_ref[...] = wout
    idx_ref[...] = iout


def _post(x1, o, w_mem_o, norm_ffn, w_router, b_router):
    n, d = x1.shape
    mw = o.shape[1]
    n_exp = w_router.shape[1]
    assert n_exp <= LANES
    tm = TOKEN_TILE
    wr = jnp.pad(w_router, ((0, 0), (0, LANES - n_exp)))
    wrh = wr.astype(BF16)
    wrl = (wr - wrh.astype(F32)).astype(BF16)
    br = jnp.concatenate([b_router.astype(F32), jnp.full((LANES - n_exp,), NEG, F32)]).reshape(1, LANES)
    row = lambda i: (i, 0)
    fix = lambda i: (0, 0)
    return pl.pallas_call(
        _post_kernel,
        grid=(n // tm,),
        in_specs=[pl.BlockSpec((tm, d), row), pl.BlockSpec((tm, mw), row), pl.BlockSpec((mw, d), fix),
                  pl.BlockSpec((1, d), fix), pl.BlockSpec((d, LANES), fix), pl.BlockSpec((d, LANES), fix),
                  pl.BlockSpec((1, LANES), fix)],
        out_specs=[pl.BlockSpec((tm, d), row), pl.BlockSpec((tm, d), row),
                   pl.BlockSpec((tm, LANES), row), pl.BlockSpec((tm, LANES), row)],
        out_shape=[jax.ShapeDtypeStruct((n, d), F32), jax.ShapeDtypeStruct((n, d), F32),
                   jax.ShapeDtypeStruct((n, LANES), jnp.int32), jax.ShapeDtypeStruct((n, LANES), F32)],
        compiler_params=_cparams(("parallel",)),
        name="post",
    )(x1, o, w_mem_o.astype(BF16), norm_ffn.reshape(1, d), wrh, wrl, br)


def _row_gather(src_hbm, idx_at, dst_ref, sem, n_rows):
    def copy(r, src_row):
        return pltpu.make_async_copy(src_hbm.at[pl.ds(src_row, 1), :], dst_ref.at[pl.ds(r, 1), :], sem)

    def start(r, carry):
        copy(r, idx_at(r)).start()
        return carry

    def wait(r, carry):
        copy(r, 0).wait()
        return carry

    lax.fori_loop(0, n_rows, start, 0, unroll=8)
    lax.fori_loop(0, n_rows, wait, 0, unroll=8)


def _experts_kernel(be_ref, nact_ref, idx_ref, rw_ref, wg_ref, bg_ref, wu_ref, bu_ref, wd_ref, bd_ref,
                    hf_hbm, y_ref, xbuf, sem):
    i = pl.program_id(0)
    tm = xbuf.shape[0]

    @pl.when(i < nact_ref[0])
    def _():
        _row_gather(hf_hbm, lambda r: idx_ref[0, 0, r], xbuf, sem, tm)
        x = xbuf[...].astype(BF16)
        g = jnp.minimum(_dot(x, wg_ref[0]) + bg_ref[0], SWIGLU_LIMIT)
        u = jnp.clip(_dot(x, wu_ref[0]) + bu_ref[0], -SWIGLU_LIMIT, SWIGLU_LIMIT)
        act = (u + 1.0) * (g * jax.nn.sigmoid(SWIGLU_ALPHA * g))
        y = _dot(act.astype(BF16), wd_ref[0]) + bd_ref[0]
        y_ref[...] = y * rw_ref[...]

    @pl.when(i >= nact_ref[0])
    def _():
        y_ref[...] = jnp.zeros(y_ref.shape, y_ref.dtype)


def _combine_kernel(dest_ref, x2_ref, y_hbm, o_ref, buf, sem):
    tc = x2_ref.shape[0]
    _row_gather(y_hbm, lambda r: dest_ref[0, 0, r], buf, sem, TOP_K * tc)
    acc = x2_ref[...]
    for kk in range(TOP_K):
        acc = acc + buf[kk * tc:(kk + 1) * tc, :]
    o_ref[...] = acc


def _moe(hf, x2, top_idx, top_w, w_gate_up, b_gate_up, w_down, b_down):
    n_tok, d = hf.shape
    n_exp, _, ff2 = w_gate_up.shape
    ff = ff2 // 2
    tm, tc = MOE_TILE, COMBINE_TILE
    assert n_tok % tc == 0
    n_assign = n_tok * TOP_K
    n_blocks = -(-(n_assign + n_exp * (tm - 1)) // tm)
    n_rows = n_blocks * tm

    flat_e = top_idx[:, :TOP_K].reshape(-1)
    flat_w = top_w[:, :TOP_K].reshape(-1)
    order = jnp.argsort(flat_e)
    se = flat_e[order]
    counts = jnp.bincount(flat_e, length=n_exp)
    padded = (counts + tm - 1) // tm * tm
    pend = jnp.cumsum(padded)
    pstart = pend - padded
    start = jnp.cumsum(counts) - counts
    dest_sorted = (pstart[se] + jnp.arange(n_assign) - start[se]).astype(jnp.int32)
    row_tok = jnp.zeros((n_rows,), jnp.int32).at[dest_sorted].set((order // TOP_K).astype(jnp.int32))
    row_w = jnp.zeros((n_rows,), F32).at[dest_sorted].set(flat_w[order])
    block_e = jnp.minimum(jnp.searchsorted(pend, jnp.arange(n_blocks) * tm, side='right'),
                          n_exp - 1).astype(jnp.int32)
    n_active = (pend[-1] // tm).astype(jnp.int32).reshape(1)
    dest = jnp.zeros((n_assign,), jnp.int32).at[order].set(dest_sorted)
    dest_tiles = dest.reshape(n_tok // tc, tc, TOP_K).transpose(0, 2, 1).reshape(n_tok // tc, 1, TOP_K * tc)

    wg = w_gate_up[:, :, 0::2].astype(BF16)
    wu = w_gate_up[:, :, 1::2].astype(BF16)
    bg = b_gate_up[:, 0::2].reshape(n_exp, 1, ff)
    bu = b_gate_up[:, 1::2].reshape(n_exp, 1, ff)
    wd = w_down.astype(BF16)
    bd = b_down.reshape(n_exp, 1, d)

    by_e = lambda i, be, na: (be[i], 0, 0)
    yb = pl.pallas_call(
        _experts_kernel,
        grid_spec=pltpu.PrefetchScalarGridSpec(
            num_scalar_prefetch=2,
            grid=(n_blocks,),
            in_specs=[pl.BlockSpec((1, 1, tm), lambda i, be, na: (i, 0, 0), memory_space=pltpu.SMEM),
                      pl.BlockSpec((tm, 1), lambda i, be, na: (i, 0)),
                      pl.BlockSpec((1, d, ff), by_e), pl.BlockSpec((1, 1, ff), by_e),
                      pl.BlockSpec((1, d, ff), by_e), pl.BlockSpec((1, 1, ff), by_e),
                      pl.BlockSpec((1, ff, d), by_e), pl.BlockSpec((1, 1, d), by_e),
                      pl.BlockSpec(memory_space=pl.ANY)],
            out_specs=pl.BlockSpec((tm, d), lambda i, be, na: (i, 0)),
            scratch_shapes=[pltpu.VMEM((tm, d), F32), pltpu.SemaphoreType.DMA(())]),
        out_shape=jax.ShapeDtypeStruct((n_rows, d), F32),
        compiler_params=_cparams(("arbitrary",)),
        name="experts",
    )(block_e, n_active, row_tok.reshape(n_blocks, 1, tm), row_w.reshape(n_rows, 1),
      wg, bg, wu, bu, wd, bd, hf)

    return pl.pallas_call(
        _combine_kernel,
        grid=(n_tok // tc,),
        in_specs=[pl.BlockSpec((1, 1, TOP_K * tc), lambda i: (i, 0, 0), memory_space=pltpu.SMEM),
                  pl.BlockSpec((tc, d), lambda i: (i, 0)),
                  pl.BlockSpec(memory_space=pl.ANY)],
        out_specs=pl.BlockSpec((tc, d), lambda i: (i, 0)),
        out_shape=jax.ShapeDtypeStruct((n_tok, d), F32),
        scratch_shapes=[pltpu.VMEM((TOP_K * tc, d), F32), pltpu.SemaphoreType.DMA(())],
        compiler_params=_cparams(("arbitrary",)),
        name="combine",
    )(dest_tiles, x2, yb)


def kernel(x_prompt, x_sample, cache_win_k, cache_win_v, state_conv, state_ssm, cache_mem_k, cache_mem_v,
           mem_prompt, norm_mix, w_in, q_gain, k_gain, conv_w, conv_b, dt_bias, a_log, d_skip, ssm_out_gain,
           w_out, norm_mem, mem_in_gain, w_mem_q, w_mem_k, w_mem_v, mem_q_gain, mem_k_gain, w_mem_o,
           norm_ffn, w_router, b_router, w_gate_up, b_gate_up, w_down, b_down):
    bp, lp, d = x_prompt.shape
    bs, ls, _ = x_sample.shape
    depth = norm_mix.shape[0]
    n_buf = cache_win_k.shape[2]
    past_len = PAST_LEN
    n_mem = mem_prompt.shape[1]
    npr, nsm = bp * lp, bs * ls
    att_w = d // 2
    n_heads = att_w // ATT_HEAD_DIM
    ssm_w = d - att_w
    conv_ch = ssm_w + 2 * SSM_GROUPS * SSM_STATE
    tail = CONV_WIDTH - 1
    keep = min(max(w for w, _ in DILATIONS), lp)
    ls_pad = SUBLANES

    x = jnp.concatenate([x_prompt.reshape(npr, d), x_sample.reshape(nsm, d)]).astype(F32)
    outs = [[] for _ in range(10)]
    for i in range(depth):
        q, k, v, z, xbc, dt_raw = _projections(x, norm_mix[i], w_in[i], q_gain[i], k_gain[i],
                                               lp, ls, past_len, npr)
        smp = lambda a: a[npr:].reshape(bs, ls, a.shape[-1])
        pad_s = lambda a: jnp.pad(smp(a), ((0, 0), (0, ls_pad - ls), (0, 0)))
        att_p = _attn_prompt(q, k, v, bp, lp)
        att_s = _attn_sample(smp(q), smp(k), smp(v), cache_win_k[i].reshape(bs, n_buf, att_w),
                             cache_win_v[i].reshape(bs, n_buf, att_w))
        ssm_par = (conv_w[i], conv_b[i], dt_bias[i], a_log[i], d_skip[i], ssm_out_gain[i])
        y_p, st_p = _ssd(xbc[:npr].reshape(bp, lp, conv_ch), dt_raw[:npr].reshape(bp, lp, LANES),
                         z[:npr].reshape(bp, lp, ssm_w), jnp.zeros((bp, tail, conv_ch), F32),
                         jnp.zeros((bp, ssm_w // SSM_HEAD_DIM, SSM_HEAD_DIM, SSM_STATE), F32), *ssm_par, lp)
        y_s, st_s = _ssd(pad_s(xbc), pad_s(dt_raw), pad_s(z), state_conv[i], state_ssm[i], *ssm_par, ls)
        att = jnp.concatenate([att_p, att_s.reshape(nsm, att_w)])
        yssm = jnp.concatenate([y_p.reshape(npr, ssm_w), y_s[:, :ls].reshape(nsm, ssm_w)])
        x1, qm = _out_proj(x, att, yssm, w_out[i], norm_mem[i], w_mem_q[i], mem_q_gain[i])
        mk_p, mv_p = _mem_kv(mem_prompt.reshape(bp * n_mem, d).astype(F32), mem_in_gain[i], w_mem_k[i],
                             w_mem_v[i], mem_k_gain[i])
        mw = mk_p.shape[-1]
        o_p = _mem_attn(qm[:npr].reshape(bp, lp, mw), mk_p.reshape(bp, n_mem, mw),
                        mv_p.reshape(bp, n_mem, mw), TOKEN_TILE)
        o_s = _mem_attn(pad_s(qm), cache_mem_k[i].reshape(bs, n_mem, mw),
                        cache_mem_v[i].reshape(bs, n_mem, mw), ls_pad)
        o = jnp.concatenate([o_p.reshape(npr, mw), o_s[:, :ls].reshape(nsm, mw)])
        x2, hf, top_idx, top_w = _post(x1, o, w_mem_o[i], norm_ffn[i], w_router[i], b_router[i])
        x = _moe(hf, x2, top_idx, top_w, w_gate_up[i], b_gate_up[i], w_down[i], b_down[i])

        heads = lambda a, b, l: a.reshape(b, l, n_heads, ATT_HEAD_DIM)
        xbc_p = xbc[:npr].reshape(bp, lp, conv_ch)
        full_s = jnp.concatenate([state_conv[i].astype(F32), smp(xbc)], axis=1)
        new = (heads(k[:npr], bp, lp)[:, lp - keep:], heads(v[:npr], bp, lp)[:, lp - keep:],
               heads(k[npr:], bs, ls), heads(v[npr:], bs, ls),
               xbc_p[:, lp - tail:], full_s[:, ls:],
               st_p, st_s,
               mk_p.reshape(bp, n_mem, mw // MEM_HEAD_DIM, MEM_HEAD_DIM),
               mv_p.reshape(bp, n_mem, mw // MEM_HEAD_DIM, MEM_HEAD_DIM))
        for lst, val in zip(outs, new):
            lst.append(val)
    y_p = x[:npr].reshape(bp, lp, d).astype(x_prompt.dtype)
    y_s = x[npr:].reshape(bs, ls, d).astype(x_sample.dtype)
    return (y_p, y_s) + tuple(jnp.stack(o) for o in outs)
```

```python
import functools
import math

import numpy as np
import jax
import jax.numpy as jnp
from jax import lax
from jax.experimental import pallas as pl
from jax.experimental.pallas import tpu as pltpu

F32 = jnp.float32
BF16 = jnp.bfloat16

ATT_HEAD_DIM = 64
DILATIONS = ((128, 1), (512, 4), (2048, 16))
ATT_BLOCK = 128
ROPE_DIM = ATT_HEAD_DIM // 4
ROPE_THETA = 500000.0
PAST_LEN = 8192
SSM_HEAD_DIM = 64
SSM_GROUPS = 2
SSM_STATE = 128
CONV_WIDTH = 4
SSM_CHUNK = 128
MEM_HEAD_DIM = 128
TOP_K = 4
SWIGLU_LIMIT = 7.0
SWIGLU_ALPHA = 1.702
NORM_EPS = 1e-6

LANES = 128
SUBLANES = 8
VMEM_LIMIT = 56 * 1024 * 1024

TOKEN_TILE = 512
MOE_TILE = 256
COMBINE_TILE = 128
ATTN_UNROLL = 8
LOG2E = math.log2(math.e)
NEG = -1e30


def _cparams(sem):
    return pltpu.CompilerParams(dimension_semantics=sem, vmem_limit_bytes=VMEM_LIMIT)


def _rms(x, gain):
    return x * lax.rsqrt(jnp.mean(x * x, axis=-1, keepdims=True) + NORM_EPS) * gain


def _dot(a, b):
    return jnp.dot(a, b, preferred_element_type=F32)


def _dot_nt(a, b):
    return lax.dot_general(a, b, (((1,), (1,)), ((), ())), preferred_element_type=F32)


def _dot_tn(a, b):
    return lax.dot_general(a, b, (((0,), (0,)), ((), ())), preferred_element_type=F32)


def _dot_f32(a, b):
    return jnp.dot(a, b, preferred_element_type=F32, precision=lax.Precision.HIGHEST)


def _proj_kernel(x_ref, g_ref, w_ref, qg_ref, kg_ref, seg_ref, cos_ref, s1_ref, s2_ref,
                 q_ref, k_ref, v_ref, z_ref, xbc_ref, dt_ref, *, att_w, ssm_w, conv_ch):
    h = _rms(x_ref[...], g_ref[...]).astype(BF16)
    seg = seg_ref[...]
    cos, s1, s2 = cos_ref[...], s1_ref[...], s2_ref[...]

    def head_norm_rope(t, gain):
        sq = t * t
        hi = sq.astype(BF16)
        lo = (sq - hi.astype(F32)).astype(BF16)
        ms = (_dot(hi, seg) + _dot(lo, seg)) * (1.0 / ATT_HEAD_DIM)
        tn = t * lax.rsqrt(ms + NORM_EPS) * gain
        half = ROPE_DIM // 2
        return (tn * cos + pltpu.roll(tn, half, 1) * s1
                + pltpu.roll(tn, att_w - half, 1) * s2)

    q = head_norm_rope(_dot(h, w_ref[:, 0:att_w]), qg_ref[...])
    q_ref[...] = q * (ATT_HEAD_DIM ** -0.5 * LOG2E)
    k_ref[...] = head_norm_rope(_dot(h, w_ref[:, att_w:2 * att_w]), kg_ref[...])
    v_ref[...] = _dot(h, w_ref[:, 2 * att_w:3 * att_w])
    o = 3 * att_w
    z_ref[...] = _dot(h, w_ref[:, o:o + ssm_w])
    o += ssm_w
    xbc_ref[...] = _dot(h, w_ref[:, o:o + conv_ch])
    o += conv_ch
    dt_ref[...] = _dot(h, w_ref[:, o:o + LANES])


def _rope_tables(pos, n_heads):
    half = ROPE_DIM // 2
    inv_freq = jnp.power(ROPE_THETA, -jnp.arange(half, dtype=F32) / half)
    ang = pos.astype(F32)[:, None] * inv_freq[None, :]
    cos, sin = jnp.cos(ang), jnp.sin(ang)
    n = pos.shape[0]
    rest = ATT_HEAD_DIM - ROPE_DIM
    c = jnp.concatenate([cos, cos, jnp.ones((n, rest), F32)], axis=-1)
    s1 = jnp.concatenate([jnp.zeros((n, half), F32), sin, jnp.zeros((n, rest), F32)], axis=-1)
    s2 = jnp.concatenate([-sin, jnp.zeros((n, half + rest), F32)], axis=-1)
    return tuple(jnp.tile(t, (1, n_heads)) for t in (c, s1, s2))


def _projections(x, norm_mix, w_in, q_gain, k_gain, seq, dec_seq, past_len, n_prompt):
    n, d = x.shape
    d_half = d // 2
    att_w, ssm_w = d_half, d - d_half
    n_heads = att_w // ATT_HEAD_DIM
    ssm_heads = ssm_w // SSM_HEAD_DIM
    conv_ch = ssm_w + 2 * SSM_GROUPS * SSM_STATE
    tm = TOKEN_TILE
    assert n % tm == 0 and n_prompt % tm == 0 and seq % tm == 0 and (n - n_prompt) == tm
    assert tm % dec_seq == 0
    c0 = 3 * att_w + ssm_w
    w = jnp.concatenate([w_in[:, :c0], w_in[:, c0 + ssm_heads:],
                         w_in[:, c0:c0 + ssm_heads],
                         jnp.zeros((d, LANES - ssm_heads), w_in.dtype)], axis=1).astype(BF16)
    wn = w.shape[1]
    pos = jnp.concatenate([jnp.arange(seq, dtype=jnp.int32),
                           past_len + jnp.arange(tm, dtype=jnp.int32) % dec_seq])
    cos, s1, s2 = _rope_tables(pos, n_heads)
    tiles_per_seq = seq // tm
    n_prompt_tiles = n_prompt // tm
    head_id = np.arange(att_w) // ATT_HEAD_DIM
    seg = jnp.asarray(head_id[:, None] == head_id[None, :], BF16)

    def tab_map(i):
        return (jnp.where(i < n_prompt_tiles, i % tiles_per_seq, tiles_per_seq), 0)

    row = lambda i: (i, 0)
    fix = lambda i: (0, 0)
    tab = pl.BlockSpec((tm, att_w), tab_map)
    kern = functools.partial(_proj_kernel, att_w=att_w, ssm_w=ssm_w, conv_ch=conv_ch)
    return pl.pallas_call(
        kern,
        grid=(n // tm,),
        in_specs=[pl.BlockSpec((tm, d), row), pl.BlockSpec((1, d), fix),
                  pl.BlockSpec((d, wn), fix), pl.BlockSpec((1, att_w), fix),
                  pl.BlockSpec((1, att_w), fix), pl.BlockSpec((att_w, att_w), fix),
                  tab, tab, tab],
        out_specs=[pl.BlockSpec((tm, att_w), row), pl.BlockSpec((tm, att_w), row),
                   pl.BlockSpec((tm, att_w), row), pl.BlockSpec((tm, ssm_w), row),
                   pl.BlockSpec((tm, conv_ch), row), pl.BlockSpec((tm, LANES), row)],
        out_shape=[jax.ShapeDtypeStruct((n, att_w), F32), jax.ShapeDtypeStruct((n, att_w), F32),
                   jax.ShapeDtypeStruct((n, att_w), F32), jax.ShapeDtypeStruct((n, ssm_w), F32),
                   jax.ShapeDtypeStruct((n, conv_ch), F32), jax.ShapeDtypeStruct((n, LANES), F32)],
        compiler_params=_cparams(("parallel",)),
        name="proj",
    )(x, norm_mix.reshape(1, d), w, jnp.tile(q_gain, n_heads).reshape(1, att_w),
      jnp.tile(k_gain, n_heads).reshape(1, att_w), seg, cos, s1, s2)


def _attn_prompt_kernel(q_ref, k_ref, v_ref, o_ref, num_ref, m_ref, den_ref, *, seq):
    blk = ATT_BLOCK
    lane = lax.broadcasted_iota(jnp.int32, (blk, LANES), 1)
    head0 = lane < ATT_HEAD_DIM
    qi = lax.broadcasted_iota(jnp.int32, (blk, 2 * blk), 0) + blk
    ki = lax.broadcasted_iota(jnp.int32, (blk, 2 * blk), 1)
    dist = qi - ki
    band = (dist >= 0) & (dist <= blk)
    own = ki >= blk

    for di, (window, dil) in enumerate(DILATIONS):
        assert window // dil == blk
        nb = seq // dil // blk

        def body(i, carry, dil=dil, nb=nb, di=di):
            r = i // nb
            j = i % nb
            if dil > 1:
                start = r + dil * blk * j
                prev = r + dil * blk * jnp.maximum(j - 1, 0)
                rows = pl.ds(start, blk, stride=dil)
                prows = pl.ds(prev, blk, stride=dil)
            else:
                rows = pl.ds(pl.multiple_of(blk * j, blk), blk)
                prows = pl.ds(pl.multiple_of(blk * jnp.maximum(j - 1, 0), blk), blk)
            qb = q_ref[rows, :].astype(BF16)
            k2 = jnp.concatenate([k_ref[prows, :], k_ref[rows, :]], axis=0).astype(BF16)
            v2 = jnp.concatenate([v_ref[prows, :], v_ref[rows, :]], axis=0).astype(BF16)
            mask = band & (own | (j > 0))
            res = []
            for hd in range(2):
                sel = head0 if hd == 0 else ~head0
                qh = jnp.where(sel, qb, jnp.zeros_like(qb))
                s = jnp.where(mask, _dot_nt(qh, k2), -jnp.inf)
                m = jnp.max(s, axis=-1, keepdims=True)
                p = jnp.exp2(s - m)
                den = jnp.sum(p, axis=-1, keepdims=True)
                res.append((_dot(p.astype(BF16), v2), m, den))
            num_ref[di, rows, :] = jnp.where(head0, res[0][0], res[1][0])
            m_ref[di, rows, :] = jnp.where(head0, res[0][1], res[1][1])
            den_ref[di, rows, :] = jnp.where(head0, res[0][2], res[1][2])
            return carry

        lax.fori_loop(0, dil * nb, body, 0, unroll=ATTN_UNROLL)

    m_all = jnp.maximum(jnp.maximum(m_ref[0], m_ref[1]), m_ref[2])
    num = jnp.zeros((seq, LANES), F32)
    den = jnp.zeros((seq, LANES), F32)
    for di in range(len(DILATIONS)):
        a = jnp.exp2(m_ref[di] - m_all)
        num = num + a * num_ref[di]
        den = den + a * den_ref[di]
    o_ref[...] = (num / den).astype(o_ref.dtype)


def _attn_prompt(q, k, v, batch, seq):
    att_w = q.shape[1]
    pairs = att_w // LANES
    nd = len(DILATIONS)
    blk = pl.BlockSpec((seq, LANES), lambda b, h: (b, h))
    return pl.pallas_call(
        functools.partial(_attn_prompt_kernel, seq=seq),
        grid=(batch, pairs),
        in_specs=[blk, blk, blk],
        out_specs=blk,
        out_shape=jax.ShapeDtypeStruct((batch * seq, att_w), BF16),
        scratch_shapes=[pltpu.VMEM((nd, seq, LANES), F32), pltpu.VMEM((nd, seq, LANES), F32),
                        pltpu.VMEM((nd, seq, LANES), F32)],
        compiler_params=_cparams(("parallel", "parallel")),
        name="attn_prompt",
    )(q, k, v)


def _attn_sample_kernel(q_ref, kn_ref, vn_ref, kc_ref, vc_ref, cc_ref, cn_ref, o_ref, *,
                        dec_seq, n_heads):
    w = q_ref.shape[-1]
    rows = dec_seq * n_heads
    q = q_ref[0]
    qm = jnp.concatenate([jnp.broadcast_to(q[t:t + 1], (n_heads, w)) for t in range(dec_seq)], axis=0)
    lane_head = lax.broadcasted_iota(jnp.int32, (rows, w), 1) // ATT_HEAD_DIM
    row_head = lax.broadcasted_iota(jnp.int32, (rows, w), 0) % n_heads
    own = lane_head == row_head
    qm = jnp.where(own, qm, 0.0).astype(BF16)
    zpad = jnp.zeros((SUBLANES - dec_seq, w), F32)
    kn = jnp.concatenate([kn_ref[0], zpad], axis=0).astype(BF16)
    vn = jnp.concatenate([vn_ref[0], zpad], axis=0).astype(BF16)
    cc, cn = cc_ref[...], cn_ref[...]
    s_c = jnp.where(cc > 0, _dot_nt(qm, kc_ref[0].astype(BF16)), -jnp.inf)
    s_n = jnp.where(cn > 0, _dot_nt(qm, kn), -jnp.inf)
    m = jnp.maximum(jnp.max(s_c, axis=-1, keepdims=True), jnp.max(s_n, axis=-1, keepdims=True))
    p_c = cc * jnp.exp2(s_c - m)
    p_n = cn * jnp.exp2(s_n - m)
    den = jnp.sum(p_c, axis=-1, keepdims=True) + jnp.sum(p_n, axis=-1, keepdims=True)
    o = _dot(p_c.astype(BF16), vc_ref[0].astype(BF16)) + _dot(p_n.astype(BF16), vn)
    o = jnp.where(own, o / den, 0.0)
    o_ref[0] = jnp.sum(o.reshape(dec_seq, n_heads, w), axis=1).astype(o_ref.dtype)


def _attn_sample(q, k_new, v_new, k_cache, v_cache):
    b, t, w = q.shape
    n_buf = k_cache.shape[1]
    n_heads = w // ATT_HEAD_DIM
    assert n_heads == SUBLANES and t <= SUBLANES
    assert n_buf >= max(win for win, _ in DILATIONS)

    def count(dist):
        return sum(((dist >= 0) & (dist % dil == 0) & (dist <= win)).astype(np.float32)
                   for win, dil in DILATIONS)

    tq = np.repeat(np.arange(t), n_heads)[:, None]
    cc = count(n_buf + tq - np.arange(n_buf)[None, :])
    jn = np.arange(SUBLANES)[None, :]
    cn = np.where(jn < t, count(tq - jn), 0.0).astype(np.float32)
    rows = t * n_heads
    new = pl.BlockSpec((1, t, w), lambda i: (i, 0, 0))
    cache = pl.BlockSpec((1, n_buf, w), lambda i: (i, 0, 0))
    return pl.pallas_call(
        functools.partial(_attn_sample_kernel, dec_seq=t, n_heads=n_heads),
        grid=(b,),
        in_specs=[new, new, new, cache, cache,
                  pl.BlockSpec((rows, n_buf), lambda i: (0, 0)),
                  pl.BlockSpec((rows, SUBLANES), lambda i: (0, 0))],
        out_specs=new,
        out_shape=jax.ShapeDtypeStruct((b, t, w), BF16),
        compiler_params=_cparams(("parallel",)),
        name="attn_sample",
    )(q, k_new, v_new, k_cache, v_cache, jnp.asarray(cc), jnp.asarray(cn))


def _softplus(x):
    return jnp.maximum(x, 0.0) + jnp.log1p(jnp.exp(-jnp.abs(x)))


def _silu(x):
    return x * jax.nn.sigmoid(x)


def _ssd_kernel(xbc_ref, dt_ref, dtt_ref, z_ref, cp_ref, h0_ref, cw_ref, cb_ref, dtb_ref, dtbt_ref,
                al_ref, alt_ref, dsk_ref, og_ref, y_ref, hl_ref, xpad_ref, h_ref, *,
                q, valid, ssm_w, n_pairs):
    c = pl.program_id(1)
    tail = CONV_WIDTH - 1

    @pl.when(c == 0)
    def _():
        h_ref[...] = h0_ref[0]
        xpad_ref[0:SUBLANES, :] = jnp.zeros((SUBLANES, xpad_ref.shape[1]), F32)
        xpad_ref[SUBLANES - tail:SUBLANES, :] = cp_ref[0]

    x = xbc_ref[0]
    xpad_ref[SUBLANES:SUBLANES + q, :] = x
    conv = cb_ref[...]
    for j in range(CONV_WIDTH):
        o = SUBLANES - tail + j
        conv = conv + xpad_ref[o:o + q, :] * cw_ref[j:j + 1, :]
    xpad_ref[0:SUBLANES, :] = x[q - SUBLANES:q, :]
    u = _silu(conv)
    xs = u[:, :ssm_w]
    gw = SSM_STATE
    bm = u[:, ssm_w:ssm_w + SSM_GROUPS * gw].astype(BF16)
    cm = u[:, ssm_w + SSM_GROUPS * gw:].astype(BF16)

    dt = _softplus(dt_ref[0] + dtb_ref[...])
    dtt = _softplus(dtt_ref[0] + dtbt_ref[...])
    if valid < q:
        dt = jnp.where(lax.broadcasted_iota(jnp.int32, dt.shape, 0) < valid, dt, 0.0)
        dtt = jnp.where(lax.broadcasted_iota(jnp.int32, dtt.shape, 1) < valid, dtt, 0.0)
    ri = lax.broadcasted_iota(jnp.int32, (q, q), 0)
    ci = lax.broadcasted_iota(jnp.int32, (q, q), 1)
    causal = ci <= ri
    acum = _dot_f32(causal.astype(F32), dt * -jnp.exp(al_ref[...]))
    acumt = _dot_f32(dtt * -jnp.exp(alt_ref[...]), (ri <= ci).astype(F32))

    left = lax.broadcasted_iota(jnp.int32, (q, LANES), 1) < SSM_HEAD_DIM
    top = lax.broadcasted_iota(jnp.int32, (LANES, LANES), 0) < SSM_HEAD_DIM
    rep = 2 * n_pairs // SSM_GROUPS
    gmat = [_dot_nt(cm[:, g * gw:(g + 1) * gw], bm[:, g * gw:(g + 1) * gw]) for g in range(SSM_GROUPS)]
    ys = []
    for pr in range(n_pairs):
        ha, hb = 2 * pr, 2 * pr + 1
        g = ha // rep
        bg = bm[:, g * gw:(g + 1) * gw]
        cg = cm[:, g * gw:(g + 1) * gw]
        xpair = xs[:, pr * LANES:(pr + 1) * LANES]
        xd = xpair * jnp.where(left, dt[:, ha:ha + 1], dt[:, hb:hb + 1])
        xdb = xd.astype(BF16)
        yd = []
        for hh in (ha, hb):
            seg = acum[:, hh:hh + 1] - acumt[hh:hh + 1, :]
            decay = jnp.exp(jnp.where(causal, seg, -jnp.inf))
            yd.append(_dot((gmat[g] * decay).astype(BF16), xdb))
        ac = jnp.where(left, acum[:, ha:ha + 1], acum[:, hb:hb + 1])
        hprev = h_ref[pr]
        y_off = _dot_nt(cg, hprev.astype(BF16)) * jnp.exp(ac)
        to_end = jnp.exp(ac[q - 1:q, :] - ac)
        upd = _dot_tn((xd * to_end).astype(BF16), bg)
        cdec = jnp.where(top, jnp.exp(acum[q - 1:q, ha:ha + 1]), jnp.exp(acum[q - 1:q, hb:hb + 1]))
        h_ref[pr] = hprev * cdec + upd
        ys.append(jnp.where(left, yd[0], yd[1]) + y_off
                  + xpair * dsk_ref[:, pr * LANES:(pr + 1) * LANES])
    y = jnp.concatenate(ys, axis=1) * _silu(z_ref[0])
    y_ref[0] = _rms(y, og_ref[...]).astype(y_ref.dtype)

    @pl.when(c == pl.num_programs(1) - 1)
    def _():
        hl_ref[0] = h_ref[...]


def _ssd(xbc, dt_raw, z, conv_prev, h0, conv_w, conv_b, dt_bias, a_log, d_skip, out_gain, valid_len):
    b, l, conv_ch = xbc.shape
    ssm_w = z.shape[-1]
    n_heads = ssm_w // SSM_HEAD_DIM
    n_pairs = n_heads // 2
    assert 2 * SSM_HEAD_DIM == LANES and SSM_STATE == LANES and n_heads <= SUBLANES
    assert (n_heads // SSM_GROUPS) % 2 == 0
    q = SSM_CHUNK if l % SSM_CHUNK == 0 else l
    assert l % q == 0 and q % SUBLANES == 0 and (valid_len == l or q == l)
    nc = l // q
    dtt = jnp.swapaxes(dt_raw[..., :n_heads], 1, 2)
    pad_h = lambda a: jnp.pad(a.reshape(1, n_heads), ((0, 0), (0, LANES - n_heads)))
    per_lane = lambda a: jnp.repeat(a, SSM_HEAD_DIM).reshape(1, ssm_w)
    h0p = h0.reshape(b, n_pairs, LANES, SSM_STATE).astype(F32)
    tail = CONV_WIDTH - 1
    tok = lambda w: pl.BlockSpec((1, q, w), lambda i, j: (i, j, 0))
    fix2 = lambda r, w: pl.BlockSpec((r, w), lambda i, j: (0, 0))
    st = pl.BlockSpec((1, n_pairs, LANES, SSM_STATE), lambda i, j: (i, 0, 0, 0))
    kern = functools.partial(_ssd_kernel, q=q, valid=valid_len if q == l else q, ssm_w=ssm_w, n_pairs=n_pairs)
    y, h_last = pl.pallas_call(
        kern,
        grid=(b, nc),
        in_specs=[tok(conv_ch), tok(LANES), pl.BlockSpec((1, n_heads, q), lambda i, j: (i, 0, j)),
                  tok(ssm_w), pl.BlockSpec((1, tail, conv_ch), lambda i, j: (i, 0, 0)), st,
                  fix2(CONV_WIDTH, conv_ch), fix2(1, conv_ch), fix2(1, LANES), fix2(n_heads, 1),
                  fix2(1, LANES), fix2(n_heads, 1), fix2(1, ssm_w), fix2(1, ssm_w)],
        out_specs=[tok(ssm_w), st],
        out_shape=[jax.ShapeDtypeStruct((b, l, ssm_w), BF16),
                   jax.ShapeDtypeStruct((b, n_pairs, LANES, SSM_STATE), F32)],
        scratch_shapes=[pltpu.VMEM((q + SUBLANES, conv_ch), F32),
                        pltpu.VMEM((n_pairs, LANES, SSM_STATE), F32)],
        compiler_params=_cparams(("parallel", "arbitrary")),
        name="ssd",
    )(xbc, dt_raw, dtt, z, conv_prev.astype(F32), h0p, conv_w, conv_b.reshape(1, conv_ch),
      pad_h(dt_bias), dt_bias.reshape(n_heads, 1), pad_h(a_log), a_log.reshape(n_heads, 1),
      per_lane(d_skip), out_gain.reshape(1, ssm_w))
    return y, h_last.reshape(b, n_heads, SSM_HEAD_DIM, SSM_STATE)


def _head_rms_store(dst_ref, t, gain, scale=None):
    for hd in range(t.shape[1] // MEM_HEAD_DIM):
        sl = slice(hd * MEM_HEAD_DIM, (hd + 1) * MEM_HEAD_DIM)
        r = _rms(t[:, sl], gain)
        if scale is not None:
            r = r * scale
        dst_ref[:, sl] = r.astype(dst_ref.dtype)


def _mem_kv_kernel(m_ref, g_ref, wk_ref, wv_ref, kg_ref, k_ref, v_ref):
    h = _rms(m_ref[...], g_ref[...]).astype(BF16)
    _head_rms_store(k_ref, _dot(h, wk_ref[...]), kg_ref[...])
    v_ref[...] = _dot(h, wv_ref[...])


def _mem_kv(mem, in_gain, w_k, w_v, k_gain):
    n, d = mem.shape
    mw = w_k.shape[1]
    tm = TOKEN_TILE
    assert n % tm == 0 and MEM_HEAD_DIM == LANES
    row = lambda i: (i, 0)
    fix = lambda i: (0, 0)
    return pl.pallas_call(
        _mem_kv_kernel,
        grid=(n // tm,),
        in_specs=[pl.BlockSpec((tm, d), row), pl.BlockSpec((1, d), fix), pl.BlockSpec((d, mw), fix),
                  pl.BlockSpec((d, mw), fix), pl.BlockSpec((1, MEM_HEAD_DIM), fix)],
        out_specs=[pl.BlockSpec((tm, mw), row), pl.BlockSpec((tm, mw), row)],
        out_shape=[jax.ShapeDtypeStruct((n, mw), F32), jax.ShapeDtypeStruct((n, mw), F32)],
        compiler_params=_cparams(("parallel",)),
        name="mem_kv",
    )(mem, in_gain.reshape(1, d), w_k.astype(BF16), w_v.astype(BF16), k_gain.reshape(1, MEM_HEAD_DIM))


def _out_proj_kernel(x_ref, att_ref, ys_ref, wo_ref, g_ref, wq_ref, qg_ref, x1_ref, q_ref):
    att_w = att_ref.shape[1]
    x1 = x_ref[...] + _dot(att_ref[...], wo_ref[0:att_w, :]) + _dot(ys_ref[...], wo_ref[att_w:, :])
    x1_ref[...] = x1
    h = _rms(x1, g_ref[...]).astype(BF16)
    _head_rms_store(q_ref, _dot(h, wq_ref[...]), qg_ref[...], MEM_HEAD_DIM ** -0.5)


def _out_proj(x, att, yssm, w_out, norm_mem, w_mem_q, mem_q_gain):
    n, d = x.shape
    att_w, ssm_w = att.shape[1], yssm.shape[1]
    mw = w_mem_q.shape[1]
    tm = TOKEN_TILE
    row = lambda i: (i, 0)
    fix = lambda i: (0, 0)
    return pl.pallas_call(
        _out_proj_kernel,
        grid=(n // tm,),
        in_specs=[pl.BlockSpec((tm, d), row), pl.BlockSpec((tm, att_w), row), pl.BlockSpec((tm, ssm_w), row),
                  pl.BlockSpec((att_w + ssm_w, d), fix), pl.BlockSpec((1, d), fix),
                  pl.BlockSpec((d, mw), fix), pl.BlockSpec((1, MEM_HEAD_DIM), fix)],
        out_specs=[pl.BlockSpec((tm, d), row), pl.BlockSpec((tm, mw), row)],
        out_shape=[jax.ShapeDtypeStruct((n, d), F32), jax.ShapeDtypeStruct((n, mw), BF16)],
        compiler_params=_cparams(("parallel",)),
        name="out_proj",
    )(x, att, yssm, w_out.astype(BF16), norm_mem.reshape(1, d), w_mem_q.astype(BF16),
      mem_q_gain.reshape(1, MEM_HEAD_DIM))


def _mem_attn_kernel(q_ref, k_ref, v_ref, o_ref):
    for hd in range(q_ref.shape[-1] // MEM_HEAD_DIM):
        sl = slice(hd * MEM_HEAD_DIM, (hd + 1) * MEM_HEAD_DIM)
        s = _dot_nt(q_ref[0, :, sl], k_ref[0, :, sl].astype(BF16))
        p = jnp.exp(s - jnp.max(s, axis=-1, keepdims=True))
        den = jnp.sum(p, axis=-1, keepdims=True)
        o_ref[0, :, sl] = (_dot(p.astype(BF16), v_ref[0, :, sl].astype(BF16)) / den).astype(o_ref.dtype)


def _mem_attn(q, mem_k, mem_v, tq):
    b, l, w = q.shape
    n_mem = mem_k.shape[1]
    assert l % tq == 0
    qs = pl.BlockSpec((1, tq, w), lambda i, j: (i, j, 0))
    ms = pl.BlockSpec((1, n_mem, w), lambda i, j: (i, 0, 0))
    return pl.pallas_call(
        _mem_attn_kernel,
        grid=(b, l // tq),
        in_specs=[qs, ms, ms],
        out_specs=qs,
        out_shape=jax.ShapeDtypeStruct((b, l, w), BF16),
        compiler_params=_cparams(("parallel", "parallel")),
        name="mem_attn",
    )(q, mem_k, mem_v)


def _post_kernel(x1_ref, o_ref, wo_ref, g_ref, wrh_ref, wrl_ref, br_ref, x2_ref, hf_ref, idx_ref, w_ref):
    x2 = x1_ref[...] + _dot(o_ref[...], wo_ref[...])
    x2_ref[...] = x2
    hf = _rms(x2, g_ref[...])
    hf_ref[...] = hf
    hi = hf.astype(BF16)
    lo = (hf - hi.astype(F32)).astype(BF16)
    logits = (_dot(hi, wrh_ref[...]) + _dot(lo, wrh_ref[...]) + _dot(hi, wrl_ref[...])) + br_ref[...]
    lane = lax.broadcasted_iota(jnp.int32, logits.shape, 1)
    vals, idxs = [], []
    for _ in range(TOP_K):
        m = jnp.max(logits, axis=-1, keepdims=True)
        ix = jnp.min(jnp.where(logits == m, lane, LANES), axis=-1, keepdims=True)
        vals.append(m)
        idxs.append(ix)
        logits = jnp.where(lane == ix, -jnp.inf, logits)
    es = [jnp.exp(v - vals[0]) for v in vals]
    tot = es[0]
    for e in es[1:]:
        tot = tot + e
    wout = jnp.zeros(logits.shape, F32)
    iout = jnp.zeros(logits.shape, jnp.int32)
    for kk in range(TOP_K):
        wout = jnp.where(lane == kk, es[kk] / tot, wout)
        iout = jnp.where(lane == kk, idxs[kk], iout)
    w_ref[...] = wout
    idx_ref[...] = iout


def _post(x1, o, w_mem_o, norm_ffn, w_router, b_router):
    n, d = x1.shape
    mw = o.shape[1]
    n_exp = w_router.shape[1]
    assert n_exp <= LANES
    tm = TOKEN_TILE
    wr = jnp.pad(w_router, ((0, 0), (0, LANES - n_exp)))
    wrh = wr.astype(BF16)
    wrl = (wr - wrh.astype(F32)).astype(BF16)
    br = jnp.concatenate([b_router.astype(F32), jnp.full((LANES - n_exp,), NEG, F32)]).reshape(1, LANES)
    row = lambda i: (i, 0)
    fix = lambda i: (0, 0)
    return pl.pallas_call(
        _post_kernel,
        grid=(n // tm,),
        in_specs=[pl.BlockSpec((tm, d), row), pl.BlockSpec((tm, mw), row), pl.BlockSpec((mw, d), fix),
                  pl.BlockSpec((1, d), fix), pl.BlockSpec((d, LANES), fix), pl.BlockSpec((d, LANES), fix),
                  pl.BlockSpec((1, LANES), fix)],
        out_specs=[pl.BlockSpec((tm, d), row), pl.BlockSpec((tm, d), row),
                   pl.BlockSpec((tm, LANES), row), pl.BlockSpec((tm, LANES), row)],
        out_shape=[jax.ShapeDtypeStruct((n, d), F32), jax.ShapeDtypeStruct((n, d), F32),
                   jax.ShapeDtypeStruct((n, LANES), jnp.int32), jax.ShapeDtypeStruct((n, LANES), F32)],
        compiler_params=_cparams(("parallel",)),
        name="post",
    )(x1, o, w_mem_o.astype(BF16), norm_ffn.reshape(1, d), wrh, wrl, br)


GU_CHUNK = 2 * LANES


def _regroup_kernel(w_ref, p_ref, o_ref):
    for c in range(w_ref.shape[-1] // GU_CHUNK):
        sl = slice(c * GU_CHUNK, (c + 1) * GU_CHUNK)
        o_ref[0, :, sl] = _dot(w_ref[0, :, sl].astype(BF16), p_ref[...]).astype(BF16)


def _regroup_gate_up(w_gate_up):
    n_exp, d, ff2 = w_gate_up.shape
    wblk = 2 * GU_CHUNK
    assert ff2 % wblk == 0
    src = np.arange(GU_CHUNK)
    dst = np.where(src % 2 == 0, src // 2, LANES + src // 2)
    perm = np.zeros((GU_CHUNK, GU_CHUNK), np.float32)
    perm[src, dst] = 1.0
    blk = pl.BlockSpec((1, d, wblk), lambda e, j: (e, 0, j))
    return pl.pallas_call(
        _regroup_kernel,
        grid=(n_exp, ff2 // wblk),
        in_specs=[blk, pl.BlockSpec((GU_CHUNK, GU_CHUNK), lambda e, j: (0, 0))],
        out_specs=blk,
        out_shape=jax.ShapeDtypeStruct((n_exp, d, ff2), BF16),
        compiler_params=_cparams(("parallel", "parallel")),
        name="regroup_gate_up",
    )(w_gate_up, jnp.asarray(perm, BF16))


def _rank_kernel(idx_ref, rank_ref, cnt_ref, carry_ref):
    @pl.when(pl.program_id(0) == 0)
    def _():
        carry_ref[...] = jnp.zeros(carry_ref.shape, F32)

    idx = idx_ref[...]
    tm = idx.shape[0]
    lane = lax.broadcasted_iota(jnp.int32, idx.shape, 1)
    hot = [(lane == idx[:, kk:kk + 1]).astype(F32) for kk in range(TOP_K)]
    tot = hot[0]
    for h in hot[1:]:
        tot = tot + h
    earlier = (lax.broadcasted_iota(jnp.int32, (tm, tm), 1)
               < lax.broadcasted_iota(jnp.int32, (tm, tm), 0)).astype(BF16)
    base = carry_ref[...] + _dot(earlier, tot.astype(BF16))
    out = jnp.zeros(idx.shape, jnp.int32)
    for kk in range(TOP_K):
        r = jnp.sum(hot[kk] * base, axis=-1, keepdims=True)
        out = jnp.where(lane == kk, r.astype(jnp.int32), out)
        base = base + hot[kk]
    rank_ref[...] = out
    carry_ref[...] = carry_ref[...] + jnp.sum(tot, axis=0, keepdims=True)
    cnt_ref[...] = carry_ref[...]


def _rank(top_idx):
    n, w = top_idx.shape
    tm = TOKEN_TILE
    return pl.pallas_call(
        _rank_kernel,
        grid=(n // tm,),
        in_specs=[pl.BlockSpec((tm, w), lambda i: (i, 0))],
        out_specs=[pl.BlockSpec((tm, w), lambda i: (i, 0)), pl.BlockSpec((1, w), lambda i: (0, 0))],
        out_shape=[jax.ShapeDtypeStruct((n, w), jnp.int32), jax.ShapeDtypeStruct((1, w), F32)],
        scratch_shapes=[pltpu.VMEM((1, w), F32)],
        compiler_params=_cparams(("arbitrary",)),
        name="rank",
    )(top_idx)


def _row_copies(n_rows, make_copy):
    def start(r, carry):
        make_copy(r).start()
        return carry

    def wait(r, carry):
        make_copy(r).wait()
        return carry

    lax.fori_loop(0, n_rows, start, 0, unroll=8)
    lax.fori_loop(0, n_rows, wait, 0, unroll=8)


def _dispatch_kernel(seg_start_ref, seg_len_ref, dest_ref, hf_ref, xb_out, zrow, sem, zsem):
    tc = hf_ref.shape[0]

    @pl.when(pl.program_id(0) == 0)
    def _():
        zrow[...] = jnp.zeros(zrow.shape, zrow.dtype)

        def zero_copy(row):
            return pltpu.make_async_copy(zrow.at[pl.ds(0, 1), :], xb_out.at[pl.ds(row, 1), :], zsem)

        def per_segment(op):
            def seg(s, carry):
                base = seg_start_ref[s]

                def row(r, c):
                    op(zero_copy(base + r))
                    return c

                return lax.fori_loop(0, seg_len_ref[s], row, carry)

            lax.fori_loop(0, seg_start_ref.shape[0], seg, 0)

        per_segment(lambda cp: cp.start())
        per_segment(lambda cp: cp.wait())

    def copy(r):
        return pltpu.make_async_copy(hf_ref.at[pl.ds(r % tc, 1), :],
                                     xb_out.at[pl.ds(dest_ref[0, 0, r], 1), :], sem)

    _row_copies(TOP_K * tc, copy)


def _experts_kernel(be_ref, nact_ref, x_ref, wgu_ref, bgu_ref, wd_ref, bd_ref, y_ref):
    i = pl.program_id(0)

    @pl.when(i < nact_ref[0])
    def _():
        gu = _dot(x_ref[...].astype(BF16), wgu_ref[0]) + bgu_ref[0]
        acts = []
        for c in range(gu.shape[1] // GU_CHUNK):
            g = jnp.minimum(gu[:, c * GU_CHUNK:c * GU_CHUNK + LANES], SWIGLU_LIMIT)
            u = jnp.clip(gu[:, c * GU_CHUNK + LANES:(c + 1) * GU_CHUNK], -SWIGLU_LIMIT, SWIGLU_LIMIT)
            acts.append(((u + 1.0) * (g * jax.nn.sigmoid(SWIGLU_ALPHA * g))).astype(BF16))
        y_ref[...] = _dot(jnp.concatenate(acts, axis=1), wd_ref[0]) + bd_ref[0]

    @pl.when(i >= nact_ref[0])
    def _():
        y_ref[...] = jnp.zeros(y_ref.shape, y_ref.dtype)


def _combine_kernel(dest_ref, x2_ref, w_ref, y_hbm, o_ref, buf, sem):
    tc = x2_ref.shape[0]

    def copy(r):
        return pltpu.make_async_copy(y_hbm.at[pl.ds(dest_ref[0, 0, r], 1), :], buf.at[pl.ds(r, 1), :], sem)

    _row_copies(TOP_K * tc, copy)
    acc = x2_ref[...]
    w = w_ref[...]
    for kk in range(TOP_K):
        acc = acc + w[:, kk:kk + 1] * buf[kk * tc:(kk + 1) * tc, :]
    o_ref[...] = acc


def _moe(hf, x2, top_idx, top_w, w_gate_up, b_gate_up, w_down, b_down):
    n_tok, d = hf.shape
    n_exp, _, ff2 = w_gate_up.shape
    ff = ff2 // 2
    tm, tc = MOE_TILE, COMBINE_TILE
    assert n_tok % tc == 0
    n_assign = n_tok * TOP_K
    n_blocks = -(-(n_assign + n_exp * (tm - 1)) // tm)
    n_rows = n_blocks * tm

    rank, cnt = _rank(top_idx)
    counts = cnt[0, :n_exp].astype(jnp.int32)
    padded = (counts + tm - 1) // tm * tm
    pend = jnp.cumsum(padded)
    pstart = pend - padded
    choice = top_idx[:, :TOP_K]
    first = jnp.sum(jnp.where(choice[:, :, None] == jnp.arange(n_exp, dtype=jnp.int32), pstart, 0), axis=-1)
    dest = (first + rank[:, :TOP_K]).astype(jnp.int32)
    dest_tiles = dest.reshape(n_tok // tc, tc, TOP_K).transpose(0, 2, 1).reshape(n_tok // tc, 1, TOP_K * tc)
    block_e = jnp.minimum(jnp.sum(jnp.arange(n_blocks, dtype=jnp.int32)[None, :] * tm >= pend[:, None], axis=0),
                          n_exp - 1).astype(jnp.int32)
    n_active = (pend[-1:] // tm).astype(jnp.int32)
    seg_start = jnp.concatenate([pstart + counts, pend[-1:]]).astype(jnp.int32)
    seg_len = jnp.concatenate([padded - counts, n_rows - pend[-1:]]).astype(jnp.int32)

    xb = pl.pallas_call(
        _dispatch_kernel,
        grid_spec=pltpu.PrefetchScalarGridSpec(
            num_scalar_prefetch=2,
            grid=(n_tok // tc,),
            in_specs=[pl.BlockSpec((1, 1, TOP_K * tc), lambda i, ss, sl: (i, 0, 0), memory_space=pltpu.SMEM),
                      pl.BlockSpec((tc, d), lambda i, ss, sl: (i, 0))],
            out_specs=pl.BlockSpec(memory_space=pl.ANY),
            scratch_shapes=[pltpu.VMEM((SUBLANES, d), F32), pltpu.SemaphoreType.DMA(()),
                            pltpu.SemaphoreType.DMA(())]),
        out_shape=jax.ShapeDtypeStruct((n_rows, d), F32),
        compiler_params=_cparams(("arbitrary",)),
        name="dispatch",
    )(seg_start, seg_len, dest_tiles, hf)

    wgu = _regroup_gate_up(w_gate_up)
    bgu = b_gate_up.reshape(n_exp, ff2 // GU_CHUNK, LANES, 2).transpose(0, 1, 3, 2).reshape(n_exp, 1, ff2)
    by_e = lambda i, be, na: (be[i], 0, 0)
    yb = pl.pallas_call(
        _experts_kernel,
        grid_spec=pltpu.PrefetchScalarGridSpec(
            num_scalar_prefetch=2,
            grid=(n_blocks,),
            in_specs=[pl.BlockSpec((tm, d), lambda i, be, na: (jnp.minimum(i, na[0] - 1), 0)),
                      pl.BlockSpec((1, d, ff2), by_e), pl.BlockSpec((1, 1, ff2), by_e),
                      pl.BlockSpec((1, ff, d), by_e), pl.BlockSpec((1, 1, d), by_e)],
            out_specs=pl.BlockSpec((tm, d), lambda i, be, na: (i, 0))),
        out_shape=jax.ShapeDtypeStruct((n_rows, d), F32),
        compiler_params=_cparams(("arbitrary",)),
        name="experts",
    )(block_e, n_active, xb, wgu, bgu, w_down.astype(BF16), b_down.reshape(n_exp, 1, d))

    return pl.pallas_call(
        _combine_kernel,
        grid=(n_tok // tc,),
        in_specs=[pl.BlockSpec((1, 1, TOP_K * tc), lambda i: (i, 0, 0), memory_space=pltpu.SMEM),
                  pl.BlockSpec((tc, d), lambda i: (i, 0)),
                  pl.BlockSpec((tc, LANES), lambda i: (i, 0)),
                  pl.BlockSpec(memory_space=pl.ANY)],
        out_specs=pl.BlockSpec((tc, d), lambda i: (i, 0)),
        out_shape=jax.ShapeDtypeStruct((n_tok, d), F32),
        scratch_shapes=[pltpu.VMEM((TOP_K * tc, d), F32), pltpu.SemaphoreType.DMA(())],
        compiler_params=_cparams(("arbitrary",)),
        name="combine",
    )(dest_tiles, x2, top_w, yb)


def kernel(x_prompt, x_sample, cache_win_k, cache_win_v, state_conv, state_ssm, cache_mem_k, cache_mem_v,
           mem_prompt, norm_mix, w_in, q_gain, k_gain, conv_w, conv_b, dt_bias, a_log, d_skip, ssm_out_gain,
           w_out, norm_mem, mem_in_gain, w_mem_q, w_mem_k, w_mem_v, mem_q_gain, mem_k_gain, w_mem_o,
           norm_ffn, w_router, b_router, w_gate_up, b_gate_up, w_down, b_down):
    bp, lp, d = x_prompt.shape
    bs, ls, _ = x_sample.shape
    depth = norm_mix.shape[0]
    n_buf = cache_win_k.shape[2]
    past_len = PAST_LEN
    n_mem = mem_prompt.shape[1]
    npr, nsm = bp * lp, bs * ls
    att_w = d // 2
    n_heads = att_w // ATT_HEAD_DIM
    ssm_w = d - att_w
    conv_ch = ssm_w + 2 * SSM_GROUPS * SSM_STATE
    tail = CONV_WIDTH - 1
    keep = min(max(w for w, _ in DILATIONS), lp)
    ls_pad = SUBLANES

    x = jnp.concatenate([x_prompt.reshape(npr, d), x_sample.reshape(nsm, d)]).astype(F32)
    outs = [[] for _ in range(10)]
    for i in range(depth):
        q, k, v, z, xbc, dt_raw = _projections(x, norm_mix[i], w_in[i], q_gain[i], k_gain[i],
                                               lp, ls, past_len, npr)
        smp = lambda a: a[npr:].reshape(bs, ls, a.shape[-1])
        pad_s = lambda a: jnp.pad(smp(a), ((0, 0), (0, ls_pad - ls), (0, 0)))
        att_p = _attn_prompt(q, k, v, bp, lp)
        att_s = _attn_sample(smp(q), smp(k), smp(v), cache_win_k[i].reshape(bs, n_buf, att_w),
                             cache_win_v[i].reshape(bs, n_buf, att_w))
        ssm_par = (conv_w[i], conv_b[i], dt_bias[i], a_log[i], d_skip[i], ssm_out_gain[i])
        y_p, st_p = _ssd(xbc[:npr].reshape(bp, lp, conv_ch), dt_raw[:npr].reshape(bp, lp, LANES),
                         z[:npr].reshape(bp, lp, ssm_w), jnp.zeros((bp, tail, conv_ch), F32),
                         jnp.zeros((bp, ssm_w // SSM_HEAD_DIM, SSM_HEAD_DIM, SSM_STATE), F32), *ssm_par, lp)
        y_s, st_s = _ssd(pad_s(xbc), pad_s(dt_raw), pad_s(z), state_conv[i], state_ssm[i], *ssm_par, ls)
        att = jnp.concatenate([att_p, att_s.reshape(nsm, att_w)])
        yssm = jnp.concatenate([y_p.reshape(npr, ssm_w), y_s[:, :ls].reshape(nsm, ssm_w)])
        x1, qm = _out_proj(x, att, yssm, w_out[i], norm_mem[i], w_mem_q[i], mem_q_gain[i])
        mk_p, mv_p = _mem_kv(mem_prompt.reshape(bp * n_mem, d).astype(F32), mem_in_gain[i], w_mem_k[i],
                             w_mem_v[i], mem_k_gain[i])
        mw = mk_p.shape[-1]
        o_p = _mem_attn(qm[:npr].reshape(bp, lp, mw), mk_p.reshape(bp, n_mem, mw),
                        mv_p.reshape(bp, n_mem, mw), TOKEN_TILE)
        o_s = _mem_attn(pad_s(qm), cache_mem_k[i].reshape(bs, n_mem, mw),
                        cache_mem_v[i].reshape(bs, n_mem, mw), ls_pad)
        o = jnp.concatenate([o_p.reshape(npr, mw), o_s[:, :ls].reshape(nsm, mw)])
        x2, hf, top_idx, top_w = _post(x1, o, w_mem_o[i], norm_ffn[i], w_router[i], b_router[i])
        x = _moe(hf, x2, top_idx, top_w, w_gate_up[i], b_gate_up[i], w_down[i], b_down[i])

        heads = lambda a, b, l: a.reshape(b, l, n_heads, ATT_HEAD_DIM)
        xbc_p = xbc[:npr].reshape(bp, lp, conv_ch)
        full_s = jnp.concatenate([state_conv[i].astype(F32), smp(xbc)], axis=1)
        new = (heads(k[:npr], bp, lp)[:, lp - keep:], heads(v[:npr], bp, lp)[:, lp - keep:],
               heads(k[npr:], bs, ls), heads(v[npr:], bs, ls),
               xbc_p[:, lp - tail:], full_s[:, ls:],
               st_p, st_s,
               mk_p.reshape(bp, n_mem, mw // MEM_HEAD_DIM, MEM_HEAD_DIM),
               mv_p.reshape(bp, n_mem, mw // MEM_HEAD_DIM, MEM_HEAD_DIM))
        for lst, val in zip(outs, new):
            lst.append(val)
    y_p = x[:npr].reshape(bp, lp, d).astype(x_prompt.dtype)
    y_s = x[npr:].reshape(bs, ls, d).astype(x_sample.dtype)
    return (y_p, y_s) + tuple(jnp.stack(o) for o in outs)
```

```python
import functools
import math

import numpy as np
import jax
import jax.numpy as jnp
from jax import lax
from jax.experimental import pallas as pl
from jax.experimental.pallas import tpu as pltpu

F32 = jnp.float32
BF16 = jnp.bfloat16

ATT_HEAD_DIM = 64
DILATIONS = ((128, 1), (512, 4), (2048, 16))
ATT_BLOCK = 128
ROPE_DIM = ATT_HEAD_DIM // 4
ROPE_THETA = 500000.0
PAST_LEN = 8192
SSM_HEAD_DIM = 64
SSM_GROUPS = 2
SSM_STATE = 128
CONV_WIDTH = 4
SSM_CHUNK = 128
MEM_HEAD_DIM = 128
TOP_K = 4
SWIGLU_LIMIT = 7.0
SWIGLU_ALPHA = 1.702
NORM_EPS = 1e-6

LANES = 128
SUBLANES = 8
VMEM_LIMIT = 56 * 1024 * 1024

TOKEN_TILE = 512
MOE_TILE = 256
COMBINE_TILE = 128
ATTN_UNROLL = 8
LOG2E = math.log2(math.e)
NEG = -1e30


def _cparams(sem):
    return pltpu.CompilerParams(dimension_semantics=sem, vmem_limit_bytes=VMEM_LIMIT)


def _rms(x, gain):
    return x * lax.rsqrt(jnp.mean(x * x, axis=-1, keepdims=True) + NORM_EPS) * gain


def _dot(a, b):
    return jnp.dot(a, b, preferred_element_type=F32)


def _dot_nt(a, b):
    return lax.dot_general(a, b, (((1,), (1,)), ((), ())), preferred_element_type=F32)


def _dot_tn(a, b):
    return lax.dot_general(a, b, (((0,), (0,)), ((), ())), preferred_element_type=F32)


def _dot_f32(a, b):
    return jnp.dot(a, b, preferred_element_type=F32, precision=lax.Precision.HIGHEST)


def _proj_kernel(x_ref, g_ref, w_ref, qg_ref, kg_ref, seg_ref, cos_ref, s1_ref, s2_ref,
                 q_ref, k_ref, v_ref, z_ref, xbc_ref, dt_ref, *, att_w, ssm_w, conv_ch):
    h = _rms(x_ref[...], g_ref[...]).astype(BF16)
    seg = seg_ref[...]
    cos, s1, s2 = cos_ref[...], s1_ref[...], s2_ref[...]

    def head_norm_rope(t, gain):
        sq = t * t
        hi = sq.astype(BF16)
        lo = (sq - hi.astype(F32)).astype(BF16)
        ms = (_dot(hi, seg) + _dot(lo, seg)) * (1.0 / ATT_HEAD_DIM)
        tn = t * lax.rsqrt(ms + NORM_EPS) * gain
        half = ROPE_DIM // 2
        return (tn * cos + pltpu.roll(tn, half, 1) * s1
                + pltpu.roll(tn, att_w - half, 1) * s2)

    q = head_norm_rope(_dot(h, w_ref[:, 0:att_w]), qg_ref[...])
    q_ref[...] = q * (ATT_HEAD_DIM ** -0.5 * LOG2E)
    k_ref[...] = head_norm_rope(_dot(h, w_ref[:, att_w:2 * att_w]), kg_ref[...])
    v_ref[...] = _dot(h, w_ref[:, 2 * att_w:3 * att_w])
    o = 3 * att_w
    z_ref[...] = _dot(h, w_ref[:, o:o + ssm_w])
    o += ssm_w
    xbc_ref[...] = _dot(h, w_ref[:, o:o + conv_ch])
    o += conv_ch
    dt_ref[...] = _dot(h, w_ref[:, o:o + LANES])


def _rope_tables(pos, n_heads):
    half = ROPE_DIM // 2
    inv_freq = jnp.power(ROPE_THETA, -jnp.arange(half, dtype=F32) / half)
    ang = pos.astype(F32)[:, None] * inv_freq[None, :]
    cos, sin = jnp.cos(ang), jnp.sin(ang)
    n = pos.shape[0]
    rest = ATT_HEAD_DIM - ROPE_DIM
    c = jnp.concatenate([cos, cos, jnp.ones((n, rest), F32)], axis=-1)
    s1 = jnp.concatenate([jnp.zeros((n, half), F32), sin, jnp.zeros((n, rest), F32)], axis=-1)
    s2 = jnp.concatenate([-sin, jnp.zeros((n, half + rest), F32)], axis=-1)
    return tuple(jnp.tile(t, (1, n_heads)) for t in (c, s1, s2))


def _projections(x, norm_mix, w_in, q_gain, k_gain, seq, dec_seq, past_len, n_prompt):
    n, d = x.shape
    d_half = d // 2
    att_w, ssm_w = d_half, d - d_half
    n_heads = att_w // ATT_HEAD_DIM
    ssm_heads = ssm_w // SSM_HEAD_DIM
    conv_ch = ssm_w + 2 * SSM_GROUPS * SSM_STATE
    tm = TOKEN_TILE
    assert n % tm == 0 and n_prompt % tm == 0 and seq % tm == 0 and (n - n_prompt) == tm
    assert tm % dec_seq == 0
    c0 = 3 * att_w + ssm_w
    w = jnp.concatenate([w_in[:, :c0], w_in[:, c0 + ssm_heads:],
                         w_in[:, c0:c0 + ssm_heads],
                         jnp.zeros((d, LANES - ssm_heads), w_in.dtype)], axis=1).astype(BF16)
    wn = w.shape[1]
    pos = jnp.concatenate([jnp.arange(seq, dtype=jnp.int32),
                           past_len + jnp.arange(tm, dtype=jnp.int32) % dec_seq])
    cos, s1, s2 = _rope_tables(pos, n_heads)
    tiles_per_seq = seq // tm
    n_prompt_tiles = n_prompt // tm
    head_id = np.arange(att_w) // ATT_HEAD_DIM
    seg = jnp.asarray(head_id[:, None] == head_id[None, :], BF16)

    def tab_map(i):
        return (jnp.where(i < n_prompt_tiles, i % tiles_per_seq, tiles_per_seq), 0)

    row = lambda i: (i, 0)
    fix = lambda i: (0, 0)
    tab = pl.BlockSpec((tm, att_w), tab_map)
    kern = functools.partial(_proj_kernel, att_w=att_w, ssm_w=ssm_w, conv_ch=conv_ch)
    return pl.pallas_call(
        kern,
        grid=(n // tm,),
        in_specs=[pl.BlockSpec((tm, d), row), pl.BlockSpec((1, d), fix),
                  pl.BlockSpec((d, wn), fix), pl.BlockSpec((1, att_w), fix),
                  pl.BlockSpec((1, att_w), fix), pl.BlockSpec((att_w, att_w), fix),
                  tab, tab, tab],
        out_specs=[pl.BlockSpec((tm, att_w), row), pl.BlockSpec((tm, att_w), row),
                   pl.BlockSpec((tm, att_w), row), pl.BlockSpec((tm, ssm_w), row),
                   pl.BlockSpec((tm, conv_ch), row), pl.BlockSpec((tm, LANES), row)],
        out_shape=[jax.ShapeDtypeStruct((n, att_w), F32), jax.ShapeDtypeStruct((n, att_w), F32),
                   jax.ShapeDtypeStruct((n, att_w), F32), jax.ShapeDtypeStruct((n, ssm_w), F32),
                   jax.ShapeDtypeStruct((n, conv_ch), F32), jax.ShapeDtypeStruct((n, LANES), F32)],
        compiler_params=_cparams(("parallel",)),
        name="proj",
    )(x, norm_mix.reshape(1, d), w, jnp.tile(q_gain, n_heads).reshape(1, att_w),
      jnp.tile(k_gain, n_heads).reshape(1, att_w), seg, cos, s1, s2)


def _attn_prompt_kernel(q_ref, k_ref, v_ref, o_ref, num_ref, m_ref, den_ref, *, seq):
    blk = ATT_BLOCK
    lane = lax.broadcasted_iota(jnp.int32, (blk, LANES), 1)
    head0 = lane < ATT_HEAD_DIM
    qi = lax.broadcasted_iota(jnp.int32, (blk, 2 * blk), 0) + blk
    ki = lax.broadcasted_iota(jnp.int32, (blk, 2 * blk), 1)
    dist = qi - ki
    band = (dist >= 0) & (dist <= blk)
    own = ki >= blk

    for di, (window, dil) in enumerate(DILATIONS):
        assert window // dil == blk
        nb = seq // dil // blk

        def body(i, carry, dil=dil, nb=nb, di=di):
            r = i // nb
            j = i % nb
            if dil > 1:
                start = r + dil * blk * j
                prev = r + dil * blk * jnp.maximum(j - 1, 0)
                rows = pl.ds(start, blk, stride=dil)
                prows = pl.ds(prev, blk, stride=dil)
            else:
                rows = pl.ds(pl.multiple_of(blk * j, blk), blk)
                prows = pl.ds(pl.multiple_of(blk * jnp.maximum(j - 1, 0), blk), blk)
            qb = q_ref[rows, :].astype(BF16)
            k2 = jnp.concatenate([k_ref[prows, :], k_ref[rows, :]], axis=0).astype(BF16)
            v2 = jnp.concatenate([v_ref[prows, :], v_ref[rows, :]], axis=0).astype(BF16)
            mask = band & (own | (j > 0))
            res = []
            for hd in range(2):
                sel = head0 if hd == 0 else ~head0
                qh = jnp.where(sel, qb, jnp.zeros_like(qb))
                s = jnp.where(mask, _dot_nt(qh, k2), -jnp.inf)
                m = jnp.max(s, axis=-1, keepdims=True)
                p = jnp.exp2(s - m)
                den = jnp.sum(p, axis=-1, keepdims=True)
                res.append((_dot(p.astype(BF16), v2), m, den))
            num_ref[di, rows, :] = jnp.where(head0, res[0][0], res[1][0])
            m_ref[di, rows, :] = jnp.where(head0, res[0][1], res[1][1])
            den_ref[di, rows, :] = jnp.where(head0, res[0][2], res[1][2])
            return carry

        lax.fori_loop(0, dil * nb, body, 0, unroll=ATTN_UNROLL)

    m_all = jnp.maximum(jnp.maximum(m_ref[0], m_ref[1]), m_ref[2])
    num = jnp.zeros((seq, LANES), F32)
    den = jnp.zeros((seq, LANES), F32)
    for di in range(len(DILATIONS)):
        a = jnp.exp2(m_ref[di] - m_all)
        num = num + a * num_ref[di]
        den = den + a * den_ref[di]
    o_ref[...] = (num / den).astype(o_ref.dtype)


def _attn_prompt(q, k, v, batch, seq):
    att_w = q.shape[1]
    pairs = att_w // LANES
    nd = len(DILATIONS)
    blk = pl.BlockSpec((seq, LANES), lambda b, h: (b, h))
    return pl.pallas_call(
        functools.partial(_attn_prompt_kernel, seq=seq),
        grid=(batch, pairs),
        in_specs=[blk, blk, blk],
        out_specs=blk,
        out_shape=jax.ShapeDtypeStruct((batch * seq, att_w), BF16),
        scratch_shapes=[pltpu.VMEM((nd, seq, LANES), F32), pltpu.VMEM((nd, seq, LANES), F32),
                        pltpu.VMEM((nd, seq, LANES), F32)],
        compiler_params=_cparams(("parallel", "parallel")),
        name="attn_prompt",
    )(q, k, v)


def _attn_sample_kernel(q_ref, kn_ref, vn_ref, ks_ref, vs_ref, kt_ref, vt_ref, cs_ref, ct_ref, cn_ref,
                        o_ref, *, dec_seq, n_heads):
    hd = ATT_HEAD_DIM
    q = q_ref[0]
    qd = jnp.concatenate([q[:, h * hd:(h + 1) * hd] for h in range(n_heads)], axis=0).astype(BF16)

    def rows_of(ref):
        a = ref[0]
        return a.reshape(a.size // hd, hd).astype(BF16)

    parts = [(rows_of(ks_ref), rows_of(vs_ref), cs_ref[...]),
             (rows_of(kt_ref), rows_of(vt_ref), ct_ref[...]),
             (rows_of(kn_ref), rows_of(vn_ref), cn_ref[...])]
    scores = [jnp.where(c > 0, _dot_nt(qd, k), -jnp.inf) for k, _, c in parts]
    m = jnp.max(scores[0], axis=-1, keepdims=True)
    for s in scores[1:]:
        m = jnp.maximum(m, jnp.max(s, axis=-1, keepdims=True))
    den = jnp.zeros_like(m)
    o = jnp.zeros((dec_seq * n_heads, hd), F32)
    for s, (_, v, c) in zip(scores, parts):
        p = c * jnp.exp2(s - m)
        den = den + jnp.sum(p, axis=-1, keepdims=True)
        o = o + _dot(p.astype(BF16), v)
    o = o / den
    o_ref[0] = jnp.concatenate([o[h * dec_seq:(h + 1) * dec_seq, :] for h in range(n_heads)],
                               axis=1).astype(o_ref.dtype)


def _attn_sample(q, k_new, v_new, k_cache, v_cache):
    b, t, w = q.shape
    n_buf, n_heads, hd = k_cache.shape[1:]
    (w1, d1), (w2, d2), (w3, d3) = DILATIONS
    tail = w2
    assert w1 <= tail and w3 <= n_buf and n_buf % d3 == 0 and tail % d3 == 0 and t <= d3
    groups = n_buf // d3

    def weight(dist, dils):
        return sum(((dist >= 0) & (dist % dil == 0) & (dist <= win)).astype(np.float32) for win, dil in dils)

    row_h = np.repeat(np.arange(n_heads), t)[:, None]
    row_t = np.tile(np.arange(t), n_heads)[:, None]

    def table(pos, dils):
        wgt = weight(n_buf + row_t - pos[None, :], dils)
        same = row_h == np.arange(n_heads)[None, :]
        return (wgt[:, :, None] * same[:, None, :]).reshape(wgt.shape[0], -1).astype(np.float32)

    pos_s = (np.arange(groups)[:, None] * d3 + np.arange(t)[None, :]).reshape(-1)
    cs = table(pos_s, DILATIONS[2:])
    ct = table(np.arange(n_buf - tail, n_buf), DILATIONS[:2])
    cn = table(n_buf + np.arange(t), DILATIONS)
    kc = k_cache.reshape(b, groups, d3, n_heads, hd)
    vc = v_cache.reshape(b, groups, d3, n_heads, hd)
    new = pl.BlockSpec((1, t, n_heads, hd), lambda i: (i, 0, 0, 0))
    strided = pl.BlockSpec((1, groups, t, n_heads, hd), lambda i: (i, 0, 0, 0, 0))
    tail_blk = pl.BlockSpec((1, tail // d3, d3, n_heads, hd), lambda i: (i, groups // (tail // d3) - 1, 0, 0, 0))
    fixed = lambda a: pl.BlockSpec(a.shape, lambda i: (0, 0))
    qo = pl.BlockSpec((1, t, w), lambda i: (i, 0, 0))
    return pl.pallas_call(
        functools.partial(_attn_sample_kernel, dec_seq=t, n_heads=n_heads),
        grid=(b,),
        in_specs=[qo, new, new, strided, strided, tail_blk, tail_blk, fixed(cs), fixed(ct), fixed(cn)],
        out_specs=qo,
        out_shape=jax.ShapeDtypeStruct((b, t, w), BF16),
        compiler_params=_cparams(("parallel",)),
        name="attn_sample",
    )(q, k_new, v_new, kc, vc, kc, vc, jnp.asarray(cs), jnp.asarray(ct), jnp.asarray(cn))


def _softplus(x):
    return jnp.maximum(x, 0.0) + jnp.log1p(jnp.exp(-jnp.abs(x)))


def _silu(x):
    return x * jax.nn.sigmoid(x)


def _ssd_kernel(xbc_ref, dt_ref, dtt_ref, z_ref, cp_ref, h0_ref, cw_ref, cb_ref, dtb_ref, dtbt_ref,
                al_ref, alt_ref, dsk_ref, og_ref, y_ref, hl_ref, xpad_ref, h_ref, *,
                q, valid, ssm_w, n_pairs):
    c = pl.program_id(1)
    tail = CONV_WIDTH - 1

    @pl.when(c == 0)
    def _():
        h_ref[...] = h0_ref[0]
        xpad_ref[0:SUBLANES, :] = jnp.zeros((SUBLANES, xpad_ref.shape[1]), F32)
        xpad_ref[SUBLANES - tail:SUBLANES, :] = cp_ref[0]

    x = xbc_ref[0]
    xpad_ref[SUBLANES:SUBLANES + q, :] = x
    conv = cb_ref[...]
    for j in range(CONV_WIDTH):
        o = SUBLANES - tail + j
        conv = conv + xpad_ref[o:o + q, :] * cw_ref[j:j + 1, :]
    xpad_ref[0:SUBLANES, :] = x[q - SUBLANES:q, :]
    u = _silu(conv)
    xs = u[:, :ssm_w]
    gw = SSM_STATE
    bm = u[:, ssm_w:ssm_w + SSM_GROUPS * gw].astype(BF16)
    cm = u[:, ssm_w + SSM_GROUPS * gw:].astype(BF16)

    dt = _softplus(dt_ref[0] + dtb_ref[...])
    dtt = _softplus(dtt_ref[0] + dtbt_ref[...])
    if valid < q:
        dt = jnp.where(lax.broadcasted_iota(jnp.int32, dt.shape, 0) < valid, dt, 0.0)
        dtt = jnp.where(lax.broadcasted_iota(jnp.int32, dtt.shape, 1) < valid, dtt, 0.0)
    ri = lax.broadcasted_iota(jnp.int32, (q, q), 0)
    ci = lax.broadcasted_iota(jnp.int32, (q, q), 1)
    causal = ci <= ri
    acum = _dot_f32(causal.astype(F32), dt * -jnp.exp(al_ref[...]))
    acumt = _dot_f32(dtt * -jnp.exp(alt_ref[...]), (ri <= ci).astype(F32))

    left = lax.broadcasted_iota(jnp.int32, (q, LANES), 1) < SSM_HEAD_DIM
    top = lax.broadcasted_iota(jnp.int32, (LANES, LANES), 0) < SSM_HEAD_DIM
    rep = 2 * n_pairs // SSM_GROUPS
    gmat = [_dot_nt(cm[:, g * gw:(g + 1) * gw], bm[:, g * gw:(g + 1) * gw]) for g in range(SSM_GROUPS)]
    ys = []
    for pr in range(n_pairs):
        ha, hb = 2 * pr, 2 * pr + 1
        g = ha // rep
        bg = bm[:, g * gw:(g + 1) * gw]
        cg = cm[:, g * gw:(g + 1) * gw]
        xpair = xs[:, pr * LANES:(pr + 1) * LANES]
        xd = xpair * jnp.where(left, dt[:, ha:ha + 1], dt[:, hb:hb + 1])
        xdb = xd.astype(BF16)
        yd = []
        for hh in (ha, hb):
            seg = acum[:, hh:hh + 1] - acumt[hh:hh + 1, :]
            decay = jnp.exp(jnp.where(causal, seg, -jnp.inf))
            yd.append(_dot((gmat[g] * decay).astype(BF16), xdb))
        ac = jnp.where(left, acum[:, ha:ha + 1], acum[:, hb:hb + 1])
        hprev = h_ref[pr]
        y_off = _dot_nt(cg, hprev.astype(BF16)) * jnp.exp(ac)
        to_end = jnp.exp(ac[q - 1:q, :] - ac)
        upd = _dot_tn((xd * to_end).astype(BF16), bg)
        cdec = jnp.where(top, jnp.exp(acum[q - 1:q, ha:ha + 1]), jnp.exp(acum[q - 1:q, hb:hb + 1]))
        h_ref[pr] = hprev * cdec + upd
        ys.append(jnp.where(left, yd[0], yd[1]) + y_off
                  + xpair * dsk_ref[:, pr * LANES:(pr + 1) * LANES])
    y = jnp.concatenate(ys, axis=1) * _silu(z_ref[0])
    y_ref[0] = _rms(y, og_ref[...]).astype(y_ref.dtype)

    @pl.when(c == pl.num_programs(1) - 1)
    def _():
        hl_ref[0] = h_ref[...]


def _ssd(xbc, dt_raw, z, conv_prev, h0, conv_w, conv_b, dt_bias, a_log, d_skip, out_gain, valid_len):
    b, l, conv_ch = xbc.shape
    ssm_w = z.shape[-1]
    n_heads = ssm_w // SSM_HEAD_DIM
    n_pairs = n_heads // 2
    assert 2 * SSM_HEAD_DIM == LANES and SSM_STATE == LANES and n_heads <= SUBLANES
    assert (n_heads // SSM_GROUPS) % 2 == 0
    q = SSM_CHUNK if l % SSM_CHUNK == 0 else l
    assert l % q == 0 and q % SUBLANES == 0 and (valid_len == l or q == l)
    nc = l // q
    dtt = jnp.swapaxes(dt_raw[..., :n_heads], 1, 2)
    pad_h = lambda a: jnp.pad(a.reshape(1, n_heads), ((0, 0), (0, LANES - n_heads)))
    per_lane = lambda a: jnp.repeat(a, SSM_HEAD_DIM).reshape(1, ssm_w)
    h0p = h0.reshape(b, n_pairs, LANES, SSM_STATE).astype(F32)
    tail = CONV_WIDTH - 1
    tok = lambda w: pl.BlockSpec((1, q, w), lambda i, j: (i, j, 0))
    fix2 = lambda r, w: pl.BlockSpec((r, w), lambda i, j: (0, 0))
    st = pl.BlockSpec((1, n_pairs, LANES, SSM_STATE), lambda i, j: (i, 0, 0, 0))
    kern = functools.partial(_ssd_kernel, q=q, valid=valid_len if q == l else q, ssm_w=ssm_w, n_pairs=n_pairs)
    y, h_last = pl.pallas_call(
        kern,
        grid=(b, nc),
        in_specs=[tok(conv_ch), tok(LANES), pl.BlockSpec((1, n_heads, q), lambda i, j: (i, 0, j)),
                  tok(ssm_w), pl.BlockSpec((1, tail, conv_ch), lambda i, j: (i, 0, 0)), st,
                  fix2(CONV_WIDTH, conv_ch), fix2(1, conv_ch), fix2(1, LANES), fix2(n_heads, 1),
                  fix2(1, LANES), fix2(n_heads, 1), fix2(1, ssm_w), fix2(1, ssm_w)],
        out_specs=[tok(ssm_w), st],
        out_shape=[jax.ShapeDtypeStruct((b, l, ssm_w), BF16),
                   jax.ShapeDtypeStruct((b, n_pairs, LANES, SSM_STATE), F32)],
        scratch_shapes=[pltpu.VMEM((q + SUBLANES, conv_ch), F32),
                        pltpu.VMEM((n_pairs, LANES, SSM_STATE), F32)],
        compiler_params=_cparams(("parallel", "arbitrary")),
        name="ssd",
    )(xbc, dt_raw, dtt, z, conv_prev.astype(F32), h0p, conv_w, conv_b.reshape(1, conv_ch),
      pad_h(dt_bias), dt_bias.reshape(n_heads, 1), pad_h(a_log), a_log.reshape(n_heads, 1),
      per_lane(d_skip), out_gain.reshape(1, ssm_w))
    return y, h_last.reshape(b, n_heads, SSM_HEAD_DIM, SSM_STATE)


def _head_rms_store(dst_ref, t, gain, scale=None):
    for hd in range(t.shape[1] // MEM_HEAD_DIM):
        sl = slice(hd * MEM_HEAD_DIM, (hd + 1) * MEM_HEAD_DIM)
        r = _rms(t[:, sl], gain)
        if scale is not None:
            r = r * scale
        dst_ref[:, sl] = r.astype(dst_ref.dtype)


def _mem_kv_kernel(m_ref, g_ref, wk_ref, wv_ref, kg_ref, k_ref, v_ref):
    h = _rms(m_ref[...], g_ref[...]).astype(BF16)
    _head_rms_store(k_ref, _dot(h, wk_ref[...]), kg_ref[...])
    v_ref[...] = _dot(h, wv_ref[...])


def _mem_kv(mem, in_gain, w_k, w_v, k_gain):
    n, d = mem.shape
    mw = w_k.shape[1]
    tm = TOKEN_TILE
    assert n % tm == 0 and MEM_HEAD_DIM == LANES
    row = lambda i: (i, 0)
    fix = lambda i: (0, 0)
    return pl.pallas_call(
        _mem_kv_kernel,
        grid=(n // tm,),
        in_specs=[pl.BlockSpec((tm, d), row), pl.BlockSpec((1, d), fix), pl.BlockSpec((d, mw), fix),
                  pl.BlockSpec((d, mw), fix), pl.BlockSpec((1, MEM_HEAD_DIM), fix)],
        out_specs=[pl.BlockSpec((tm, mw), row), pl.BlockSpec((tm, mw), row)],
        out_shape=[jax.ShapeDtypeStruct((n, mw), F32), jax.ShapeDtypeStruct((n, mw), F32)],
        compiler_params=_cparams(("parallel",)),
        name="mem_kv",
    )(mem, in_gain.reshape(1, d), w_k.astype(BF16), w_v.astype(BF16), k_gain.reshape(1, MEM_HEAD_DIM))


def _out_proj_kernel(x_ref, att_ref, ys_ref, wo_ref, g_ref, wq_ref, qg_ref, x1_ref, q_ref):
    att_w = att_ref.shape[1]
    x1 = x_ref[...] + _dot(att_ref[...], wo_ref[0:att_w, :]) + _dot(ys_ref[...], wo_ref[att_w:, :])
    x1_ref[...] = x1
    h = _rms(x1, g_ref[...]).astype(BF16)
    _head_rms_store(q_ref, _dot(h, wq_ref[...]), qg_ref[...], MEM_HEAD_DIM ** -0.5)


def _out_proj(x, att, yssm, w_out, norm_mem, w_mem_q, mem_q_gain):
    n, d = x.shape
    att_w, ssm_w = att.shape[1], yssm.shape[1]
    mw = w_mem_q.shape[1]
    tm = TOKEN_TILE
    row = lambda i: (i, 0)
    fix = lambda i: (0, 0)
    return pl.pallas_call(
        _out_proj_kernel,
        grid=(n // tm,),
        in_specs=[pl.BlockSpec((tm, d), row), pl.BlockSpec((tm, att_w), row), pl.BlockSpec((tm, ssm_w), row),
                  pl.BlockSpec((att_w + ssm_w, d), fix), pl.BlockSpec((1, d), fix),
                  pl.BlockSpec((d, mw), fix), pl.BlockSpec((1, MEM_HEAD_DIM), fix)],
        out_specs=[pl.BlockSpec((tm, d), row), pl.BlockSpec((tm, mw), row)],
        out_shape=[jax.ShapeDtypeStruct((n, d), F32), jax.ShapeDtypeStruct((n, mw), BF16)],
        compiler_params=_cparams(("parallel",)),
        name="out_proj",
    )(x, att, yssm, w_out.astype(BF16), norm_mem.reshape(1, d), w_mem_q.astype(BF16),
      mem_q_gain.reshape(1, MEM_HEAD_DIM))


def _mem_attn_kernel(q_ref, k_ref, v_ref, o_ref):
    for hd in range(q_ref.shape[-1] // MEM_HEAD_DIM):
        sl = slice(hd * MEM_HEAD_DIM, (hd + 1) * MEM_HEAD_DIM)
        s = _dot_nt(q_ref[0, :, sl], k_ref[0, :, sl].astype(BF16))
        p = jnp.exp(s - jnp.max(s, axis=-1, keepdims=True))
        den = jnp.sum(p, axis=-1, keepdims=True)
        o_ref[0, :, sl] = (_dot(p.astype(BF16), v_ref[0, :, sl].astype(BF16)) / den).astype(o_ref.dtype)


def _mem_attn(q, mem_k, mem_v, tq):
    b, l, w = q.shape
    n_mem = mem_k.shape[1]
    assert l % tq == 0
    qs = pl.BlockSpec((1, tq, w), lambda i, j: (i, j, 0))
    ms = pl.BlockSpec((1, n_mem, w), lambda i, j: (i, 0, 0))
    return pl.pallas_call(
        _mem_attn_kernel,
        grid=(b, l // tq),
        in_specs=[qs, ms, ms],
        out_specs=qs,
        out_shape=jax.ShapeDtypeStruct((b, l, w), BF16),
        compiler_params=_cparams(("parallel", "parallel")),
        name="mem_attn",
    )(q, mem_k, mem_v)


def _post_kernel(x1_ref, o_ref, wo_ref, g_ref, wrh_ref, wrl_ref, br_ref, x2_ref, hf_ref, idx_ref, w_ref):
    x2 = x1_ref[...] + _dot(o_ref[...], wo_ref[...])
    x2_ref[...] = x2
    hf = _rms(x2, g_ref[...])
    hf_ref[...] = hf
    hi = hf.astype(BF16)
    lo = (hf - hi.astype(F32)).astype(BF16)
    logits = (_dot(hi, wrh_ref[...]) + _dot(lo, wrh_ref[...]) + _dot(hi, wrl_ref[...])) + br_ref[...]
    lane = lax.broadcasted_iota(jnp.int32, logits.shape, 1)
    vals, idxs = [], []
    for _ in range(TOP_K):
        m = jnp.max(logits, axis=-1, keepdims=True)
        ix = jnp.min(jnp.where(logits == m, lane, LANES), axis=-1, keepdims=True)
        vals.append(m)
        idxs.append(ix)
        logits = jnp.where(lane == ix, -jnp.inf, logits)
    es = [jnp.exp(v - vals[0]) for v in vals]
    tot = es[0]
    for e in es[1:]:
        tot = tot + e
    wout = jnp.zeros(logits.shape, F32)
    iout = jnp.zeros(logits.shape, jnp.int32)
    for kk in range(TOP_K):
        wout = jnp.where(lane == kk, es[kk] / tot, wout)
        iout = jnp.where(lane == kk, idxs[kk], iout)
    w_ref[...] = wout
    idx_ref[...] = iout


def _post(x1, o, w_mem_o, norm_ffn, w_router, b_router):
    n, d = x1.shape
    mw = o.shape[1]
    n_exp = w_router.shape[1]
    assert n_exp <= LANES
    tm = TOKEN_TILE
    wr = jnp.pad(w_router, ((0, 0), (0, LANES - n_exp)))
    wrh = wr.astype(BF16)
    wrl = (wr - wrh.astype(F32)).astype(BF16)
    br = jnp.concatenate([b_router.astype(F32), jnp.full((LANES - n_exp,), NEG, F32)]).reshape(1, LANES)
    row = lambda i: (i, 0)
    fix = lambda i: (0, 0)
    return pl.pallas_call(
        _post_kernel,
        grid=(n // tm,),
        in_specs=[pl.BlockSpec((tm, d), row), pl.BlockSpec((tm, mw), row), pl.BlockSpec((mw, d), fix),
                  pl.BlockSpec((1, d), fix), pl.BlockSpec((d, LANES), fix), pl.BlockSpec((d, LANES), fix),
                  pl.BlockSpec((1, LANES), fix)],
        out_specs=[pl.BlockSpec((tm, d), row), pl.BlockSpec((tm, d), row),
                   pl.BlockSpec((tm, LANES), row), pl.BlockSpec((tm, LANES), row)],
        out_shape=[jax.ShapeDtypeStruct((n, d), F32), jax.ShapeDtypeStruct((n, d), F32),
                   jax.ShapeDtypeStruct((n, LANES), jnp.int32), jax.ShapeDtypeStruct((n, LANES), F32)],
        compiler_params=_cparams(("parallel",)),
        name="post",
    )(x1, o, w_mem_o.astype(BF16), norm_ffn.reshape(1, d), wrh, wrl, br)


GU_CHUNK = 2 * LANES


def _regroup_kernel(w_ref, p_ref, o_ref):
    for c in range(w_ref.shape[-1] // GU_CHUNK):
        sl = slice(c * GU_CHUNK, (c + 1) * GU_CHUNK)
        o_ref[0, :, sl] = _dot(w_ref[0, :, sl].astype(BF16), p_ref[...]).astype(BF16)


def _regroup_gate_up(w_gate_up):
    n_exp, d, ff2 = w_gate_up.shape
    wblk = 2 * GU_CHUNK
    assert ff2 % wblk == 0
    src = np.arange(GU_CHUNK)
    dst = np.where(src % 2 == 0, src // 2, LANES + src // 2)
    perm = np.zeros((GU_CHUNK, GU_CHUNK), np.float32)
    perm[src, dst] = 1.0
    blk = pl.BlockSpec((1, d, wblk), lambda e, j: (e, 0, j))
    return pl.pallas_call(
        _regroup_kernel,
        grid=(n_exp, ff2 // wblk),
        in_specs=[blk, pl.BlockSpec((GU_CHUNK, GU_CHUNK), lambda e, j: (0, 0))],
        out_specs=blk,
        out_shape=jax.ShapeDtypeStruct((n_exp, d, ff2), BF16),
        compiler_params=_cparams(("parallel", "parallel")),
        name="regroup_gate_up",
    )(w_gate_up, jnp.asarray(perm, BF16))


def _rank_kernel(idx_ref, rank_ref, cnt_ref, carry_ref):
    @pl.when(pl.program_id(0) == 0)
    def _():
        carry_ref[...] = jnp.zeros(carry_ref.shape, F32)

    idx = idx_ref[...]
    tm = idx.shape[0]
    lane = lax.broadcasted_iota(jnp.int32, idx.shape, 1)
    hot = [(lane == idx[:, kk:kk + 1]).astype(F32) for kk in range(TOP_K)]
    tot = hot[0]
    for h in hot[1:]:
        tot = tot + h
    earlier = (lax.broadcasted_iota(jnp.int32, (tm, tm), 1)
               < lax.broadcasted_iota(jnp.int32, (tm, tm), 0)).astype(BF16)
    base = carry_ref[...] + _dot(earlier, tot.astype(BF16))
    out = jnp.zeros(idx.shape, jnp.int32)
    for kk in range(TOP_K):
        r = jnp.sum(hot[kk] * base, axis=-1, keepdims=True)
        out = jnp.where(lane == kk, r.astype(jnp.int32), out)
        base = base + hot[kk]
    rank_ref[...] = out
    carry_ref[...] = carry_ref[...] + jnp.sum(tot, axis=0, keepdims=True)
    cnt_ref[...] = carry_ref[...]


def _rank(top_idx):
    n, w = top_idx.shape
    tm = TOKEN_TILE
    return pl.pallas_call(
        _rank_kernel,
        grid=(n // tm,),
        in_specs=[pl.BlockSpec((tm, w), lambda i: (i, 0))],
        out_specs=[pl.BlockSpec((tm, w), lambda i: (i, 0)), pl.BlockSpec((1, w), lambda i: (0, 0))],
        out_shape=[jax.ShapeDtypeStruct((n, w), jnp.int32), jax.ShapeDtypeStruct((1, w), F32)],
        scratch_shapes=[pltpu.VMEM((1, w), F32)],
        compiler_params=_cparams(("arbitrary",)),
        name="rank",
    )(top_idx)


def _dispatch_kernel(seg_start_ref, seg_len_ref, dest_ref, hf_ref, xb_out, zrow, sem, zsem):
    tc = hf_ref.shape[0]

    @pl.when(pl.program_id(0) == 0)
    def _():
        zrow[...] = jnp.zeros(zrow.shape, zrow.dtype)

        def zero_copy(row):
            return pltpu.make_async_copy(zrow.at[pl.ds(0, 1), :], xb_out.at[pl.ds(row, 1), :], zsem)

        def per_segment(op):
            def seg(s, carry):
                base = seg_start_ref[s]

                def row(r, c):
                    op(zero_copy(base + r))
                    return c

                return lax.fori_loop(0, seg_len_ref[s], row, carry)

            lax.fori_loop(0, seg_start_ref.shape[0], seg, 0)

        per_segment(lambda cp: cp.start())
        per_segment(lambda cp: cp.wait())

    for r in range(TOP_K * tc):
        pltpu.make_async_copy(hf_ref.at[pl.ds(r % tc, 1), :],
                              xb_out.at[pl.ds(dest_ref[0, 0, r], 1), :], sem).start(priority=r % 2)
    for _ in range(TOP_K):
        pltpu.make_async_copy(hf_ref, xb_out.at[pl.ds(0, tc), :], sem).wait()


def _experts_kernel(be_ref, nact_ref, x_ref, wgu_ref, bgu_ref, wd_ref, bd_ref, y_ref):
    i = pl.program_id(0)

    @pl.when(i < nact_ref[0])
    def _():
        gu = _dot(x_ref[...].astype(BF16), wgu_ref[0]) + bgu_ref[0]
        acts = []
        for c in range(gu.shape[1] // GU_CHUNK):
            g = jnp.minimum(gu[:, c * GU_CHUNK:c * GU_CHUNK + LANES], SWIGLU_LIMIT)
            u = jnp.clip(gu[:, c * GU_CHUNK + LANES:(c + 1) * GU_CHUNK], -SWIGLU_LIMIT, SWIGLU_LIMIT)
            acts.append(((u + 1.0) * (g * jax.nn.sigmoid(SWIGLU_ALPHA * g))).astype(BF16))
        y_ref[...] = _dot(jnp.concatenate(acts, axis=1), wd_ref[0]) + bd_ref[0]

    @pl.when(i >= nact_ref[0])
    def _():
        y_ref[...] = jnp.zeros(y_ref.shape, y_ref.dtype)


def _combine_kernel(dest_ref, x2_ref, w_ref, y_hbm, o_ref, buf, sem):
    tc = x2_ref.shape[0]

    for r in range(TOP_K * tc):
        pltpu.make_async_copy(y_hbm.at[pl.ds(dest_ref[0, 0, r], 1), :], buf.at[pl.ds(r, 1), :],
                              sem).start(priority=r % 2)
    pltpu.make_async_copy(y_hbm.at[pl.ds(0, TOP_K * tc), :], buf, sem).wait()
    acc = x2_ref[...]
    w = w_ref[...]
    for kk in range(TOP_K):
        acc = acc + w[:, kk:kk + 1] * buf[kk * tc:(kk + 1) * tc, :]
    o_ref[...] = acc


def _moe(hf, x2, top_idx, top_w, w_gate_up, b_gate_up, w_down, b_down):
    n_tok, d = hf.shape
    n_exp, _, ff2 = w_gate_up.shape
    ff = ff2 // 2
    tm, tc = MOE_TILE, COMBINE_TILE
    assert n_tok % tc == 0
    n_assign = n_tok * TOP_K
    n_blocks = -(-(n_assign + n_exp * (tm - 1)) // tm)
    n_rows = n_blocks * tm

    rank, cnt = _rank(top_idx)
    counts = cnt[0, :n_exp].astype(jnp.int32)
    padded = (counts + tm - 1) // tm * tm
    pend = jnp.cumsum(padded)
    pstart = pend - padded
    choice = top_idx[:, :TOP_K]
    first = jnp.sum(jnp.where(choice[:, :, None] == jnp.arange(n_exp, dtype=jnp.int32), pstart, 0), axis=-1)
    dest = (first + rank[:, :TOP_K]).astype(jnp.int32)
    dest_tiles = dest.reshape(n_tok // tc, tc, TOP_K).transpose(0, 2, 1).reshape(n_tok // tc, 1, TOP_K * tc)
    block_e = jnp.minimum(jnp.sum(jnp.arange(n_blocks, dtype=jnp.int32)[None, :] * tm >= pend[:, None], axis=0),
                          n_exp - 1).astype(jnp.int32)
    n_active = (pend[-1:] // tm).astype(jnp.int32)
    seg_start = jnp.concatenate([pstart + counts, pend[-1:]]).astype(jnp.int32)
    seg_len = jnp.concatenate([padded - counts, n_rows - pend[-1:]]).astype(jnp.int32)

    xb = pl.pallas_call(
        _dispatch_kernel,
        grid_spec=pltpu.PrefetchScalarGridSpec(
            num_scalar_prefetch=2,
            grid=(n_tok // tc,),
            in_specs=[pl.BlockSpec((1, 1, TOP_K * tc), lambda i, ss, sl: (i, 0, 0), memory_space=pltpu.SMEM),
                      pl.BlockSpec((tc, d), lambda i, ss, sl: (i, 0))],
            out_specs=pl.BlockSpec(memory_space=pl.ANY),
            scratch_shapes=[pltpu.VMEM((SUBLANES, d), F32), pltpu.SemaphoreType.DMA(()),
                            pltpu.SemaphoreType.DMA(())]),
        out_shape=jax.ShapeDtypeStruct((n_rows, d), F32),
        compiler_params=_cparams(("arbitrary",)),
        name="dispatch",
    )(seg_start, seg_len, dest_tiles, hf)

    wgu = _regroup_gate_up(w_gate_up)
    bgu = b_gate_up.reshape(n_exp, ff2 // GU_CHUNK, LANES, 2).transpose(0, 1, 3, 2).reshape(n_exp, 1, ff2)
    by_e = lambda i, be, na: (be[i], 0, 0)
    yb = pl.pallas_call(
        _experts_kernel,
        grid_spec=pltpu.PrefetchScalarGridSpec(
            num_scalar_prefetch=2,
            grid=(n_blocks,),
            in_specs=[pl.BlockSpec((tm, d), lambda i, be, na: (jnp.minimum(i, na[0] - 1), 0)),
                      pl.BlockSpec((1, d, ff2), by_e), pl.BlockSpec((1, 1, ff2), by_e),
                      pl.BlockSpec((1, ff, d), by_e), pl.BlockSpec((1, 1, d), by_e)],
            out_specs=pl.BlockSpec((tm, d), lambda i, be, na: (i, 0))),
        out_shape=jax.ShapeDtypeStruct((n_rows, d), F32),
        compiler_params=_cparams(("arbitrary",)),
        name="experts",
    )(block_e, n_active, xb, wgu, bgu, w_down.astype(BF16), b_down.reshape(n_exp, 1, d))

    return pl.pallas_call(
        _combine_kernel,
        grid=(n_tok // tc,),
        in_specs=[pl.BlockSpec((1, 1, TOP_K * tc), lambda i: (i, 0, 0), memory_space=pltpu.SMEM),
                  pl.BlockSpec((tc, d), lambda i: (i, 0)),
                  pl.BlockSpec((tc, LANES), lambda i: (i, 0)),
                  pl.BlockSpec(memory_space=pl.ANY)],
        out_specs=pl.BlockSpec((tc, d), lambda i: (i, 0)),
        out_shape=jax.ShapeDtypeStruct((n_tok, d), F32),
        scratch_shapes=[pltpu.VMEM((TOP_K * tc, d), F32), pltpu.SemaphoreType.DMA(())],
        compiler_params=_cparams(("arbitrary",)),
        name="combine",
    )(dest_tiles, x2, top_w, yb)


def kernel(x_prompt, x_sample, cache_win_k, cache_win_v, state_conv, state_ssm, cache_mem_k, cache_mem_v,
           mem_prompt, norm_mix, w_in, q_gain, k_gain, conv_w, conv_b, dt_bias, a_log, d_skip, ssm_out_gain,
           w_out, norm_mem, mem_in_gain, w_mem_q, w_mem_k, w_mem_v, mem_q_gain, mem_k_gain, w_mem_o,
           norm_ffn, w_router, b_router, w_gate_up, b_gate_up, w_down, b_down):
    bp, lp, d = x_prompt.shape
    bs, ls, _ = x_sample.shape
    depth = norm_mix.shape[0]
    n_buf = cache_win_k.shape[2]
    past_len = PAST_LEN
    n_mem = mem_prompt.shape[1]
    npr, nsm = bp * lp, bs * ls
    att_w = d // 2
    n_heads = att_w // ATT_HEAD_DIM
    ssm_w = d - att_w
    conv_ch = ssm_w + 2 * SSM_GROUPS * SSM_STATE
    tail = CONV_WIDTH - 1
    keep = min(max(w for w, _ in DILATIONS), lp)
    ls_pad = SUBLANES

    x = jnp.concatenate([x_prompt.reshape(npr, d), x_sample.reshape(nsm, d)]).astype(F32)
    outs = [[] for _ in range(10)]
    for i in range(depth):
        q, k, v, z, xbc, dt_raw = _projections(x, norm_mix[i], w_in[i], q_gain[i], k_gain[i],
                                               lp, ls, past_len, npr)
        smp = lambda a: a[npr:].reshape(bs, ls, a.shape[-1])
        pad_s = lambda a: jnp.pad(smp(a), ((0, 0), (0, ls_pad - ls), (0, 0)))
        att_p = _attn_prompt(q, k, v, bp, lp)
        heads = lambda a, b, l: a.reshape(b, l, n_heads, ATT_HEAD_DIM)
        k_new, v_new = heads(k[npr:], bs, ls), heads(v[npr:], bs, ls)
        att_s = _attn_sample(smp(q), k_new, v_new, cache_win_k[i], cache_win_v[i])
        ssm_par = (conv_w[i], conv_b[i], dt_bias[i], a_log[i], d_skip[i], ssm_out_gain[i])
        y_p, st_p = _ssd(xbc[:npr].reshape(bp, lp, conv_ch), dt_raw[:npr].reshape(bp, lp, LANES),
                         z[:npr].reshape(bp, lp, ssm_w), jnp.zeros((bp, tail, conv_ch), F32),
                         jnp.zeros((bp, ssm_w // SSM_HEAD_DIM, SSM_HEAD_DIM, SSM_STATE), F32), *ssm_par, lp)
        y_s, st_s = _ssd(pad_s(xbc), pad_s(dt_raw), pad_s(z), state_conv[i], state_ssm[i], *ssm_par, ls)
        att = jnp.concatenate([att_p, att_s.reshape(nsm, att_w)])
        yssm = jnp.concatenate([y_p.reshape(npr, ssm_w), y_s[:, :ls].reshape(nsm, ssm_w)])
        x1, qm = _out_proj(x, att, yssm, w_out[i], norm_mem[i], w_mem_q[i], mem_q_gain[i])
        mk_p, mv_p = _mem_kv(mem_prompt.reshape(bp * n_mem, d).astype(F32), mem_in_gain[i], w_mem_k[i],
                             w_mem_v[i], mem_k_gain[i])
        mw = mk_p.shape[-1]
        o_p = _mem_attn(qm[:npr].reshape(bp, lp, mw), mk_p.reshape(bp, n_mem, mw),
                        mv_p.reshape(bp, n_mem, mw), TOKEN_TILE)
        o_s = _mem_attn(pad_s(qm), cache_mem_k[i].reshape(bs, n_mem, mw),
                        cache_mem_v[i].reshape(bs, n_mem, mw), ls_pad)
        o = jnp.concatenate([o_p.reshape(npr, mw), o_s[:, :ls].reshape(nsm, mw)])
        x2, hf, top_idx, top_w = _post(x1, o, w_mem_o[i], norm_ffn[i], w_router[i], b_router[i])
        x = _moe(hf, x2, top_idx, top_w, w_gate_up[i], b_gate_up[i], w_down[i], b_down[i])

        xbc_p = xbc[:npr].reshape(bp, lp, conv_ch)
        full_s = jnp.concatenate([state_conv[i].astype(F32), smp(xbc)], axis=1)
        new = (heads(k[:npr], bp, lp)[:, lp - keep:], heads(v[:npr], bp, lp)[:, lp - keep:],
               k_new, v_new,
               xbc_p[:, lp - tail:], full_s[:, ls:],
               st_p, st_s,
               mk_p.reshape(bp, n_mem, mw // MEM_HEAD_DIM, MEM_HEAD_DIM),
               mv_p.reshape(bp, n_mem, mw // MEM_HEAD_DIM, MEM_HEAD_DIM))
        for lst, val in zip(outs, new):
            lst.append(val)
    y_p = x[:npr].reshape(bp, lp, d).astype(x_prompt.dtype)
    y_s = x[npr:].reshape(bs, ls, d).astype(x_sample.dtype)
    return (y_p, y_s) + tuple(jnp.stack(o) for o in outs)
```

```python
import functools
import math

import numpy as np
import jax
import jax.numpy as jnp
from jax import lax
from jax.experimental import pallas as pl
from jax.experimental.pallas import tpu as pltpu

F32 = jnp.float32
BF16 = jnp.bfloat16

ATT_HEAD_DIM = 64
DILATIONS = ((128, 1), (512, 4), (2048, 16))
ATT_BLOCK = 128
ROPE_DIM = ATT_HEAD_DIM // 4
ROPE_THETA = 500000.0
PAST_LEN = 8192
SSM_HEAD_DIM = 64
SSM_GROUPS = 2
SSM_STATE = 128
CONV_WIDTH = 4
SSM_CHUNK = 128
MEM_HEAD_DIM = 128
TOP_K = 4
SWIGLU_LIMIT = 7.0
SWIGLU_ALPHA = 1.702
NORM_EPS = 1e-6

LANES = 128
SUBLANES = 8
VMEM_LIMIT = 56 * 1024 * 1024

TOKEN_TILE = 512
MOE_TILE = 256
COMBINE_TILE = 256
ATTN_UNROLL = 8
LOG2E = math.log2(math.e)
NEG = -1e30


def _cparams(sem):
    return pltpu.CompilerParams(dimension_semantics=sem, vmem_limit_bytes=VMEM_LIMIT)


def _rms(x, gain):
    return x * lax.rsqrt(jnp.mean(x * x, axis=-1, keepdims=True) + NORM_EPS) * gain


def _dot(a, b):
    return jnp.dot(a, b, preferred_element_type=F32)


def _dot_nt(a, b):
    return lax.dot_general(a, b, (((1,), (1,)), ((), ())), preferred_element_type=F32)


def _dot_tn(a, b):
    return lax.dot_general(a, b, (((0,), (0,)), ((), ())), preferred_element_type=F32)


def _dot_f32(a, b):
    return jnp.dot(a, b, preferred_element_type=F32, precision=lax.Precision.HIGHEST)


def _proj_kernel(x_ref, g_ref, w_ref, qg_ref, kg_ref, seg_ref, cos_ref, s1_ref, s2_ref,
                 q_ref, k_ref, v_ref, z_ref, xbc_ref, dt_ref, *, att_w, ssm_w, conv_ch):
    h = _rms(x_ref[...], g_ref[...]).astype(BF16)
    seg = seg_ref[...]
    cos, s1, s2 = cos_ref[...], s1_ref[...], s2_ref[...]

    def head_norm_rope(t, gain):
        sq = t * t
        hi = sq.astype(BF16)
        lo = (sq - hi.astype(F32)).astype(BF16)
        ms = (_dot(hi, seg) + _dot(lo, seg)) * (1.0 / ATT_HEAD_DIM)
        tn = t * lax.rsqrt(ms + NORM_EPS) * gain
        half = ROPE_DIM // 2
        return (tn * cos + pltpu.roll(tn, half, 1) * s1
                + pltpu.roll(tn, att_w - half, 1) * s2)

    q = head_norm_rope(_dot(h, w_ref[:, 0:att_w]), qg_ref[...])
    q_ref[...] = q * (ATT_HEAD_DIM ** -0.5 * LOG2E)
    k_ref[...] = head_norm_rope(_dot(h, w_ref[:, att_w:2 * att_w]), kg_ref[...])
    v_ref[...] = _dot(h, w_ref[:, 2 * att_w:3 * att_w])
    o = 3 * att_w
    z_ref[...] = _dot(h, w_ref[:, o:o + ssm_w])
    o += ssm_w
    xbc_ref[...] = _dot(h, w_ref[:, o:o + conv_ch])
    o += conv_ch
    dt_ref[...] = _dot(h, w_ref[:, o:o + LANES])


def _rope_tables(pos, n_heads):
    half = ROPE_DIM // 2
    inv_freq = jnp.power(ROPE_THETA, -jnp.arange(half, dtype=F32) / half)
    ang = pos.astype(F32)[:, None] * inv_freq[None, :]
    cos, sin = jnp.cos(ang), jnp.sin(ang)
    n = pos.shape[0]
    rest = ATT_HEAD_DIM - ROPE_DIM
    c = jnp.concatenate([cos, cos, jnp.ones((n, rest), F32)], axis=-1)
    s1 = jnp.concatenate([jnp.zeros((n, half), F32), sin, jnp.zeros((n, rest), F32)], axis=-1)
    s2 = jnp.concatenate([-sin, jnp.zeros((n, half + rest), F32)], axis=-1)
    return tuple(jnp.tile(t, (1, n_heads)) for t in (c, s1, s2))


def _projections(x, norm_mix, w_in, q_gain, k_gain, seq, dec_seq, past_len, n_prompt):
    n, d = x.shape
    d_half = d // 2
    att_w, ssm_w = d_half, d - d_half
    n_heads = att_w // ATT_HEAD_DIM
    ssm_heads = ssm_w // SSM_HEAD_DIM
    conv_ch = ssm_w + 2 * SSM_GROUPS * SSM_STATE
    tm = TOKEN_TILE
    assert n % tm == 0 and n_prompt % tm == 0 and seq % tm == 0 and (n - n_prompt) == tm
    assert tm % dec_seq == 0
    c0 = 3 * att_w + ssm_w
    w = jnp.concatenate([w_in[:, :c0], w_in[:, c0 + ssm_heads:],
                         w_in[:, c0:c0 + ssm_heads],
                         jnp.zeros((d, LANES - ssm_heads), w_in.dtype)], axis=1).astype(BF16)
    wn = w.shape[1]
    pos = jnp.concatenate([jnp.arange(seq, dtype=jnp.int32),
                           past_len + jnp.arange(tm, dtype=jnp.int32) % dec_seq])
    cos, s1, s2 = _rope_tables(pos, n_heads)
    tiles_per_seq = seq // tm
    n_prompt_tiles = n_prompt // tm
    head_id = np.arange(att_w) // ATT_HEAD_DIM
    seg = jnp.asarray(head_id[:, None] == head_id[None, :], BF16)

    def tab_map(i):
        return (jnp.where(i < n_prompt_tiles, i % tiles_per_seq, tiles_per_seq), 0)

    row = lambda i: (i, 0)
    fix = lambda i: (0, 0)
    tab = pl.BlockSpec((tm, att_w), tab_map)
    kern = functools.partial(_proj_kernel, att_w=att_w, ssm_w=ssm_w, conv_ch=conv_ch)
    return pl.pallas_call(
        kern,
        grid=(n // tm,),
        in_specs=[pl.BlockSpec((tm, d), row), pl.BlockSpec((1, d), fix),
                  pl.BlockSpec((d, wn), fix), pl.BlockSpec((1, att_w), fix),
                  pl.BlockSpec((1, att_w), fix), pl.BlockSpec((att_w, att_w), fix),
                  tab, tab, tab],
        out_specs=[pl.BlockSpec((tm, att_w), row), pl.BlockSpec((tm, att_w), row),
                   pl.BlockSpec((tm, att_w), row), pl.BlockSpec((tm, ssm_w), row),
                   pl.BlockSpec((tm, conv_ch), row), pl.BlockSpec((tm, LANES), row)],
        out_shape=[jax.ShapeDtypeStruct((n, att_w), F32), jax.ShapeDtypeStruct((n, att_w), F32),
                   jax.ShapeDtypeStruct((n, att_w), F32), jax.ShapeDtypeStruct((n, ssm_w), F32),
                   jax.ShapeDtypeStruct((n, conv_ch), F32), jax.ShapeDtypeStruct((n, LANES), F32)],
        compiler_params=_cparams(("parallel",)),
        name="proj",
    )(x, norm_mix.reshape(1, d), w, jnp.tile(q_gain, n_heads).reshape(1, att_w),
      jnp.tile(k_gain, n_heads).reshape(1, att_w), seg, cos, s1, s2)


def _attn_prompt_kernel(q_ref, k_ref, v_ref, o_ref, num_ref, m_ref, den_ref, *, seq):
    blk = ATT_BLOCK
    lane = lax.broadcasted_iota(jnp.int32, (blk, LANES), 1)
    head0 = lane < ATT_HEAD_DIM
    qi = lax.broadcasted_iota(jnp.int32, (blk, 2 * blk), 0) + blk
    ki = lax.broadcasted_iota(jnp.int32, (blk, 2 * blk), 1)
    dist = qi - ki
    band = (dist >= 0) & (dist <= blk)
    own = ki >= blk

    for di, (window, dil) in enumerate(DILATIONS):
        assert window // dil == blk
        nb = seq // dil // blk

        def body(i, carry, dil=dil, nb=nb, di=di):
            r = i // nb
            j = i % nb
            if dil > 1:
                start = r + dil * blk * j
                prev = r + dil * blk * jnp.maximum(j - 1, 0)
                rows = pl.ds(start, blk, stride=dil)
                prows = pl.ds(prev, blk, stride=dil)
            else:
                rows = pl.ds(pl.multiple_of(blk * j, blk), blk)
                prows = pl.ds(pl.multiple_of(blk * jnp.maximum(j - 1, 0), blk), blk)
            qb = q_ref[rows, :].astype(BF16)
            k2 = jnp.concatenate([k_ref[prows, :], k_ref[rows, :]], axis=0).astype(BF16)
            v2 = jnp.concatenate([v_ref[prows, :], v_ref[rows, :]], axis=0).astype(BF16)
            mask = band & (own | (j > 0))
            res = []
            for hd in range(2):
                sel = head0 if hd == 0 else ~head0
                qh = jnp.where(sel, qb, jnp.zeros_like(qb))
                s = jnp.where(mask, _dot_nt(qh, k2), -jnp.inf)
                m = jnp.max(s, axis=-1, keepdims=True)
                p = jnp.exp2(s - m)
                den = jnp.sum(p, axis=-1, keepdims=True)
                res.append((_dot(p.astype(BF16), v2), m, den))
            num_ref[di, rows, :] = jnp.where(head0, res[0][0], res[1][0])
            m_ref[di, rows, :] = jnp.where(head0, res[0][1], res[1][1])
            den_ref[di, rows, :] = jnp.where(head0, res[0][2], res[1][2])
            return carry

        lax.fori_loop(0, dil * nb, body, 0, unroll=ATTN_UNROLL)

    m_all = jnp.maximum(jnp.maximum(m_ref[0], m_ref[1]), m_ref[2])
    num = jnp.zeros((seq, LANES), F32)
    den = jnp.zeros((seq, LANES), F32)
    for di in range(len(DILATIONS)):
        a = jnp.exp2(m_ref[di] - m_all)
        num = num + a * num_ref[di]
        den = den + a * den_ref[di]
    o_ref[...] = (num / den).astype(o_ref.dtype)


def _attn_prompt(q, k, v, batch, seq):
    att_w = q.shape[1]
    pairs = att_w // LANES
    nd = len(DILATIONS)
    blk = pl.BlockSpec((seq, LANES), lambda b, h: (b, h))
    return pl.pallas_call(
        functools.partial(_attn_prompt_kernel, seq=seq),
        grid=(batch, pairs),
        in_specs=[blk, blk, blk],
        out_specs=blk,
        out_shape=jax.ShapeDtypeStruct((batch * seq, att_w), BF16),
        scratch_shapes=[pltpu.VMEM((nd, seq, LANES), F32), pltpu.VMEM((nd, seq, LANES), F32),
                        pltpu.VMEM((nd, seq, LANES), F32)],
        compiler_params=_cparams(("parallel", "parallel")),
        name="attn_prompt",
    )(q, k, v)


def _attn_sample_kernel(q_ref, kn_ref, vn_ref, kc_ref, vc_ref, cc_ref, cn_ref, o_ref, *,
                        dec_seq, n_heads):
    w = q_ref.shape[-1]
    rows = dec_seq * n_heads
    q = q_ref[0]
    qm = jnp.concatenate([jnp.broadcast_to(q[t:t + 1], (n_heads, w)) for t in range(dec_seq)], axis=0)
    lane_head = lax.broadcasted_iota(jnp.int32, (rows, w), 1) // ATT_HEAD_DIM
    row_head = lax.broadcasted_iota(jnp.int32, (rows, w), 0) % n_heads
    own = lane_head == row_head
    qm = jnp.where(own, qm, 0.0).astype(BF16)
    zpad = jnp.zeros((SUBLANES - dec_seq, w), F32)
    kn = jnp.concatenate([kn_ref[0], zpad], axis=0).astype(BF16)
    vn = jnp.concatenate([vn_ref[0], zpad], axis=0).astype(BF16)
    cc, cn = cc_ref[...], cn_ref[...]
    s_c = jnp.where(cc > 0, _dot(qm, kc_ref[0].astype(BF16)), -jnp.inf)
    s_n = jnp.where(cn > 0, _dot_nt(qm, kn), -jnp.inf)
    m = jnp.maximum(jnp.max(s_c, axis=-1, keepdims=True), jnp.max(s_n, axis=-1, keepdims=True))
    p_c = cc * jnp.exp2(s_c - m)
    p_n = cn * jnp.exp2(s_n - m)
    den = jnp.sum(p_c, axis=-1, keepdims=True) + jnp.sum(p_n, axis=-1, keepdims=True)
    o = _dot_nt(p_c.astype(BF16), vc_ref[0].astype(BF16)) + _dot(p_n.astype(BF16), vn)
    o = jnp.where(own, o / den, 0.0)
    o_ref[0] = jnp.sum(o.reshape(dec_seq, n_heads, w), axis=1).astype(o_ref.dtype)


def _attn_sample(q, k_new, v_new, k_cache_t, v_cache_t):
    b, t, w = q.shape
    n_buf = k_cache_t.shape[2]
    n_heads = w // ATT_HEAD_DIM
    assert n_heads == SUBLANES and t <= SUBLANES
    assert n_buf >= max(win for win, _ in DILATIONS)

    def count(dist):
        return sum(((dist >= 0) & (dist % dil == 0) & (dist <= win)).astype(np.float32)
                   for win, dil in DILATIONS)

    tq = np.repeat(np.arange(t), n_heads)[:, None]
    cc = count(n_buf + tq - np.arange(n_buf)[None, :])
    jn = np.arange(SUBLANES)[None, :]
    cn = np.where(jn < t, count(tq - jn), 0.0).astype(np.float32)
    rows = t * n_heads
    new = pl.BlockSpec((1, t, w), lambda i: (i, 0, 0))
    cache = pl.BlockSpec((1, w, n_buf), lambda i: (i, 0, 0))
    return pl.pallas_call(
        functools.partial(_attn_sample_kernel, dec_seq=t, n_heads=n_heads),
        grid=(b,),
        in_specs=[new, new, new, cache, cache,
                  pl.BlockSpec((rows, n_buf), lambda i: (0, 0)),
                  pl.BlockSpec((rows, SUBLANES), lambda i: (0, 0))],
        out_specs=new,
        out_shape=jax.ShapeDtypeStruct((b, t, w), BF16),
        compiler_params=_cparams(("parallel",)),
        name="attn_sample",
    )(q, k_new, v_new, k_cache_t, v_cache_t, jnp.asarray(cc), jnp.asarray(cn))


def _softplus(x):
    return jnp.maximum(x, 0.0) + jnp.log1p(jnp.exp(-jnp.abs(x)))


def _silu(x):
    return x * jax.nn.sigmoid(x)


def _ssd_kernel(xbc_ref, dt_ref, dtt_ref, z_ref, cp_ref, h0_ref, cw_ref, cb_ref, dtb_ref, dtbt_ref,
                al_ref, alt_ref, dsk_ref, og_ref, y_ref, hl_ref, xpad_ref, h_ref, *,
                q, valid, ssm_w, n_pairs):
    c = pl.program_id(1)
    tail = CONV_WIDTH - 1

    @pl.when(c == 0)
    def _():
        h_ref[...] = h0_ref[0]
        xpad_ref[0:SUBLANES, :] = jnp.zeros((SUBLANES, xpad_ref.shape[1]), F32)
        xpad_ref[SUBLANES - tail:SUBLANES, :] = cp_ref[0]

    x = xbc_ref[...]
    xpad_ref[SUBLANES:SUBLANES + q, :] = x
    conv = cb_ref[...]
    for j in range(CONV_WIDTH):
        o = SUBLANES - tail + j
        conv = conv + xpad_ref[o:o + q, :] * cw_ref[j:j + 1, :]
    xpad_ref[0:SUBLANES, :] = x[q - SUBLANES:q, :]
    u = _silu(conv)
    xs = u[:, :ssm_w]
    gw = SSM_STATE
    bm = u[:, ssm_w:ssm_w + SSM_GROUPS * gw].astype(BF16)
    cm = u[:, ssm_w + SSM_GROUPS * gw:].astype(BF16)

    dt = _softplus(dt_ref[...] + dtb_ref[...])
    dtt = _softplus(dtt_ref[0] + dtbt_ref[...])
    if valid < q:
        dt = jnp.where(lax.broadcasted_iota(jnp.int32, dt.shape, 0) < valid, dt, 0.0)
        dtt = jnp.where(lax.broadcasted_iota(jnp.int32, dtt.shape, 1) < valid, dtt, 0.0)
    ri = lax.broadcasted_iota(jnp.int32, (q, q), 0)
    ci = lax.broadcasted_iota(jnp.int32, (q, q), 1)
    causal = ci <= ri
    acum = _dot_f32(causal.astype(F32), dt * -jnp.exp(al_ref[...]))
    acumt = _dot_f32(dtt * -jnp.exp(alt_ref[...]), (ri <= ci).astype(F32))

    left = lax.broadcasted_iota(jnp.int32, (q, LANES), 1) < SSM_HEAD_DIM
    top = lax.broadcasted_iota(jnp.int32, (LANES, LANES), 0) < SSM_HEAD_DIM
    rep = 2 * n_pairs // SSM_GROUPS
    gmat = [_dot_nt(cm[:, g * gw:(g + 1) * gw], bm[:, g * gw:(g + 1) * gw]) for g in range(SSM_GROUPS)]
    ys = []
    for pr in range(n_pairs):
        ha, hb = 2 * pr, 2 * pr + 1
        g = ha // rep
        bg = bm[:, g * gw:(g + 1) * gw]
        cg = cm[:, g * gw:(g + 1) * gw]
        xpair = xs[:, pr * LANES:(pr + 1) * LANES]
        xd = xpair * jnp.where(left, dt[:, ha:ha + 1], dt[:, hb:hb + 1])
        xdb = xd.astype(BF16)
        yd = []
        for hh in (ha, hb):
            seg = acum[:, hh:hh + 1] - acumt[hh:hh + 1, :]
            decay = jnp.exp(jnp.where(causal, seg, -jnp.inf))
            yd.append(_dot((gmat[g] * decay).astype(BF16), xdb))
        ac = jnp.where(left, acum[:, ha:ha + 1], acum[:, hb:hb + 1])
        hprev = h_ref[pr]
        y_off = _dot_nt(cg, hprev.astype(BF16)) * jnp.exp(ac)
        to_end = jnp.exp(ac[q - 1:q, :] - ac)
        upd = _dot_tn((xd * to_end).astype(BF16), bg)
        cdec = jnp.where(top, jnp.exp(acum[q - 1:q, ha:ha + 1]), jnp.exp(acum[q - 1:q, hb:hb + 1]))
        h_ref[pr] = hprev * cdec + upd
        ys.append(jnp.where(left, yd[0], yd[1]) + y_off
                  + xpair * dsk_ref[:, pr * LANES:(pr + 1) * LANES])
    y = jnp.concatenate(ys, axis=1) * _silu(z_ref[...])
    y_ref[...] = _rms(y, og_ref[...]).astype(y_ref.dtype)

    @pl.when(c == pl.num_programs(1) - 1)
    def _():
        hl_ref[0] = h_ref[...]


def _ssd(xbc, dt_raw, z, conv_prev, h0, conv_w, conv_b, dt_bias, a_log, d_skip, out_gain, b, l, valid_len):
    conv_ch = xbc.shape[-1]
    ssm_w = z.shape[-1]
    n_heads = ssm_w // SSM_HEAD_DIM
    n_pairs = n_heads // 2
    assert 2 * SSM_HEAD_DIM == LANES and SSM_STATE == LANES and n_heads <= SUBLANES
    assert (n_heads // SSM_GROUPS) % 2 == 0
    q = SSM_CHUNK if l % SSM_CHUNK == 0 else l
    assert l % q == 0 and q % SUBLANES == 0 and (valid_len == l or q == l)
    nc = l // q
    dtt = jnp.swapaxes(dt_raw[:b * l, :n_heads].reshape(b, l, n_heads), 1, 2)
    pad_h = lambda a: jnp.pad(a.reshape(1, n_heads), ((0, 0), (0, LANES - n_heads)))
    per_lane = lambda a: jnp.repeat(a, SSM_HEAD_DIM).reshape(1, ssm_w)
    h0p = h0.reshape(b, n_pairs, LANES, SSM_STATE).astype(F32)
    tail = CONV_WIDTH - 1
    tok = lambda w: pl.BlockSpec((q, w), lambda i, j: (i * nc + j, 0))
    fix2 = lambda r, w: pl.BlockSpec((r, w), lambda i, j: (0, 0))
    st = pl.BlockSpec((1, n_pairs, LANES, SSM_STATE), lambda i, j: (i, 0, 0, 0))
    kern = functools.partial(_ssd_kernel, q=q, valid=valid_len if q == l else q, ssm_w=ssm_w, n_pairs=n_pairs)
    y, h_last = pl.pallas_call(
        kern,
        grid=(b, nc),
        in_specs=[tok(conv_ch), tok(LANES), pl.BlockSpec((1, n_heads, q), lambda i, j: (i, 0, j)),
                  tok(ssm_w), pl.BlockSpec((1, tail, conv_ch), lambda i, j: (i, 0, 0)), st,
                  fix2(CONV_WIDTH, conv_ch), fix2(1, conv_ch), fix2(1, LANES), fix2(n_heads, 1),
                  fix2(1, LANES), fix2(n_heads, 1), fix2(1, ssm_w), fix2(1, ssm_w)],
        out_specs=[tok(ssm_w), st],
        out_shape=[jax.ShapeDtypeStruct((b * l, ssm_w), BF16),
                   jax.ShapeDtypeStruct((b, n_pairs, LANES, SSM_STATE), F32)],
        scratch_shapes=[pltpu.VMEM((q + SUBLANES, conv_ch), F32),
                        pltpu.VMEM((n_pairs, LANES, SSM_STATE), F32)],
        compiler_params=_cparams(("parallel", "arbitrary")),
        name="ssd",
    )(xbc, dt_raw, dtt, z, conv_prev.astype(F32), h0p, conv_w, conv_b.reshape(1, conv_ch),
      pad_h(dt_bias), dt_bias.reshape(n_heads, 1), pad_h(a_log), a_log.reshape(n_heads, 1),
      per_lane(d_skip), out_gain.reshape(1, ssm_w))
    return y, h_last.reshape(b, n_heads, SSM_HEAD_DIM, SSM_STATE)


def _head_rms_store(dst_ref, t, gain, scale=None):
    for hd in range(t.shape[1] // MEM_HEAD_DIM):
        sl = slice(hd * MEM_HEAD_DIM, (hd + 1) * MEM_HEAD_DIM)
        r = _rms(t[:, sl], gain)
        if scale is not None:
            r = r * scale
        dst_ref[:, sl] = r.astype(dst_ref.dtype)


def _mem_kv_kernel(m_ref, g_ref, wk_ref, wv_ref, kg_ref, k_ref, v_ref):
    h = _rms(m_ref[...], g_ref[...]).astype(BF16)
    _head_rms_store(k_ref, _dot(h, wk_ref[...]), kg_ref[...])
    v_ref[...] = _dot(h, wv_ref[...])


def _mem_kv(mem, in_gain, w_k, w_v, k_gain):
    n, d = mem.shape
    mw = w_k.shape[1]
    tm = TOKEN_TILE
    assert n % tm == 0 and MEM_HEAD_DIM == LANES
    row = lambda i: (i, 0)
    fix = lambda i: (0, 0)
    return pl.pallas_call(
        _mem_kv_kernel,
        grid=(n // tm,),
        in_specs=[pl.BlockSpec((tm, d), row), pl.BlockSpec((1, d), fix), pl.BlockSpec((d, mw), fix),
                  pl.BlockSpec((d, mw), fix), pl.BlockSpec((1, MEM_HEAD_DIM), fix)],
        out_specs=[pl.BlockSpec((tm, mw), row), pl.BlockSpec((tm, mw), row)],
        out_shape=[jax.ShapeDtypeStruct((n, mw), F32), jax.ShapeDtypeStruct((n, mw), F32)],
        compiler_params=_cparams(("parallel",)),
        name="mem_kv",
    )(mem, in_gain.reshape(1, d), w_k.astype(BF16), w_v.astype(BF16), k_gain.reshape(1, MEM_HEAD_DIM))


def _out_proj_kernel(x_ref, att_ref, ys_ref, wo_ref, g_ref, wq_ref, qg_ref, x1_ref, q_ref):
    att_w = att_ref.shape[1]
    x1 = x_ref[...] + _dot(att_ref[...], wo_ref[0:att_w, :]) + _dot(ys_ref[...], wo_ref[att_w:, :])
    x1_ref[...] = x1
    h = _rms(x1, g_ref[...]).astype(BF16)
    _head_rms_store(q_ref, _dot(h, wq_ref[...]), qg_ref[...], MEM_HEAD_DIM ** -0.5)


def _out_proj(x, att, yssm, w_out, norm_mem, w_mem_q, mem_q_gain):
    n, d = x.shape
    att_w, ssm_w = att.shape[1], yssm.shape[1]
    mw = w_mem_q.shape[1]
    tm = TOKEN_TILE
    row = lambda i: (i, 0)
    fix = lambda i: (0, 0)
    return pl.pallas_call(
        _out_proj_kernel,
        grid=(n // tm,),
        in_specs=[pl.BlockSpec((tm, d), row), pl.BlockSpec((tm, att_w), row), pl.BlockSpec((tm, ssm_w), row),
                  pl.BlockSpec((att_w + ssm_w, d), fix), pl.BlockSpec((1, d), fix),
                  pl.BlockSpec((d, mw), fix), pl.BlockSpec((1, MEM_HEAD_DIM), fix)],
        out_specs=[pl.BlockSpec((tm, d), row), pl.BlockSpec((tm, mw), row)],
        out_shape=[jax.ShapeDtypeStruct((n, d), F32), jax.ShapeDtypeStruct((n, mw), BF16)],
        compiler_params=_cparams(("parallel",)),
        name="out_proj",
    )(x, att, yssm, w_out.astype(BF16), norm_mem.reshape(1, d), w_mem_q.astype(BF16),
      mem_q_gain.reshape(1, MEM_HEAD_DIM))


def _mem_attn_kernel(q_ref, k_ref, v_ref, o_ref):
    for hd in range(q_ref.shape[-1] // MEM_HEAD_DIM):
        sl = slice(hd * MEM_HEAD_DIM, (hd + 1) * MEM_HEAD_DIM)
        s = _dot_nt(q_ref[:, sl], k_ref[0, :, sl].astype(BF16))
        p = jnp.exp(s - jnp.max(s, axis=-1, keepdims=True))
        den = jnp.sum(p, axis=-1, keepdims=True)
        o_ref[:, sl] = (_dot(p.astype(BF16), v_ref[0, :, sl].astype(BF16)) / den).astype(o_ref.dtype)


def _mem_attn(q, mem_k, mem_v, b, l, tq):
    w = q.shape[-1]
    n_mem = mem_k.shape[1]
    assert l % tq == 0
    qs = pl.BlockSpec((tq, w), lambda i, j: (i * (l // tq) + j, 0))
    ms = pl.BlockSpec((1, n_mem, w), lambda i, j: (i, 0, 0))
    return pl.pallas_call(
        _mem_attn_kernel,
        grid=(b, l // tq),
        in_specs=[qs, ms, ms],
        out_specs=qs,
        out_shape=jax.ShapeDtypeStruct((b * l, w), BF16),
        compiler_params=_cparams(("parallel", "parallel")),
        name="mem_attn",
    )(q, mem_k, mem_v)


def _post_kernel(x1_ref, o_ref, wo_ref, g_ref, wrh_ref, wrl_ref, br_ref, x2_ref, hf_ref, idx_ref, w_ref):
    x2 = x1_ref[...] + _dot(o_ref[...], wo_ref[...])
    x2_ref[...] = x2
    hf = _rms(x2, g_ref[...])
    hf_ref[...] = hf
    hi = hf.astype(BF16)
    lo = (hf - hi.astype(F32)).astype(BF16)
    logits = (_dot(hi, wrh_ref[...]) + _dot(lo, wrh_ref[...]) + _dot(hi, wrl_ref[...])) + br_ref[...]
    lane = lax.broadcasted_iota(jnp.int32, logits.shape, 1)
    vals, idxs = [], []
    for _ in range(TOP_K):
        m = jnp.max(logits, axis=-1, keepdims=True)
        ix = jnp.min(jnp.where(logits == m, lane, LANES), axis=-1, keepdims=True)
        vals.append(m)
        idxs.append(ix)
        logits = jnp.where(lane == ix, -jnp.inf, logits)
    es = [jnp.exp(v - vals[0]) for v in vals]
    tot = es[0]
    for e in es[1:]:
        tot = tot + e
    wout = jnp.zeros(logits.shape, F32)
    iout = jnp.zeros(logits.shape, jnp.int32)
    for kk in range(TOP_K):
        wout = jnp.where(lane == kk, es[kk] / tot, wout)
        iout = jnp.where(lane == kk, idxs[kk], iout)
    w_ref[...] = wout
    idx_ref[...] = iout


def _post(x1, o, w_mem_o, norm_ffn, w_router, b_router):
    n, d = x1.shape
    mw = o.shape[1]
    n_exp = w_router.shape[1]
    assert n_exp <= LANES
    tm = TOKEN_TILE
    wr = jnp.pad(w_router, ((0, 0), (0, LANES - n_exp)))
    wrh = wr.astype(BF16)
    wrl = (wr - wrh.astype(F32)).astype(BF16)
    br = jnp.concatenate([b_router.astype(F32), jnp.full((LANES - n_exp,), NEG, F32)]).reshape(1, LANES)
    row = lambda i: (i, 0)
    fix = lambda i: (0, 0)
    return pl.pallas_call(
        _post_kernel,
        grid=(n // tm,),
        in_specs=[pl.BlockSpec((tm, d), row), pl.BlockSpec((tm, mw), row), pl.BlockSpec((mw, d), fix),
                  pl.BlockSpec((1, d), fix), pl.BlockSpec((d, LANES), fix), pl.BlockSpec((d, LANES), fix),
                  pl.BlockSpec((1, LANES), fix)],
        out_specs=[pl.BlockSpec((tm, d), row), pl.BlockSpec((tm, d), row),
                   pl.BlockSpec((tm, LANES), row), pl.BlockSpec((tm, LANES), row)],
        out_shape=[jax.ShapeDtypeStruct((n, d), F32), jax.ShapeDtypeStruct((n, d), F32),
                   jax.ShapeDtypeStruct((n, LANES), jnp.int32), jax.ShapeDtypeStruct((n, LANES), F32)],
        compiler_params=_cparams(("parallel",)),
        name="post",
    )(x1, o, w_mem_o.astype(BF16), norm_ffn.reshape(1, d), wrh, wrl, br)


GU_CHUNK = 2 * LANES


def _regroup_kernel(w_ref, p_ref, o_ref):
    for c in range(w_ref.shape[-1] // GU_CHUNK):
        sl = slice(c * GU_CHUNK, (c + 1) * GU_CHUNK)
        o_ref[0, :, sl] = _dot(w_ref[0, :, sl].astype(BF16), p_ref[...]).astype(BF16)


def _regroup_gate_up(w_gate_up):
    n_exp, d, ff2 = w_gate_up.shape
    wblk = 2 * GU_CHUNK
    assert ff2 % wblk == 0
    src = np.arange(GU_CHUNK)
    dst = np.where(src % 2 == 0, src // 2, LANES + src // 2)
    perm = np.zeros((GU_CHUNK, GU_CHUNK), np.float32)
    perm[src, dst] = 1.0
    blk = pl.BlockSpec((1, d, wblk), lambda e, j: (e, 0, j))
    return pl.pallas_call(
        _regroup_kernel,
        grid=(n_exp, ff2 // wblk),
        in_specs=[blk, pl.BlockSpec((GU_CHUNK, GU_CHUNK), lambda e, j: (0, 0))],
        out_specs=blk,
        out_shape=jax.ShapeDtypeStruct((n_exp, d, ff2), BF16),
        compiler_params=_cparams(("parallel", "parallel")),
        name="regroup_gate_up",
    )(w_gate_up, jnp.asarray(perm, BF16))


def _rank_kernel(idx_ref, rank_ref, cnt_ref, carry_ref):
    @pl.when(pl.program_id(0) == 0)
    def _():
        carry_ref[...] = jnp.zeros(carry_ref.shape, F32)

    idx = idx_ref[...]
    tm = idx.shape[0]
    lane = lax.broadcasted_iota(jnp.int32, idx.shape, 1)
    hot = [(lane == idx[:, kk:kk + 1]).astype(F32) for kk in range(TOP_K)]
    tot = hot[0]
    for h in hot[1:]:
        tot = tot + h
    earlier = (lax.broadcasted_iota(jnp.int32, (tm, tm), 1)
               < lax.broadcasted_iota(jnp.int32, (tm, tm), 0)).astype(BF16)
    base = carry_ref[...] + _dot(earlier, tot.astype(BF16))
    out = jnp.zeros(idx.shape, jnp.int32)
    for kk in range(TOP_K):
        r = jnp.sum(hot[kk] * base, axis=-1, keepdims=True)
        out = jnp.where(lane == kk, r.astype(jnp.int32), out)
        base = base + hot[kk]
    rank_ref[...] = out
    carry_ref[...] = carry_ref[...] + jnp.sum(tot, axis=0, keepdims=True)
    cnt_ref[...] = carry_ref[...]


def _rank(top_idx):
    n, w = top_idx.shape
    tm = TOKEN_TILE
    return pl.pallas_call(
        _rank_kernel,
        grid=(n // tm,),
        in_specs=[pl.BlockSpec((tm, w), lambda i: (i, 0))],
        out_specs=[pl.BlockSpec((tm, w), lambda i: (i, 0)), pl.BlockSpec((1, w), lambda i: (0, 0))],
        out_shape=[jax.ShapeDtypeStruct((n, w), jnp.int32), jax.ShapeDtypeStruct((1, w), F32)],
        scratch_shapes=[pltpu.VMEM((1, w), F32)],
        compiler_params=_cparams(("arbitrary",)),
        name="rank",
    )(top_idx)


def _dispatch_kernel(seg_start_ref, seg_len_ref, dest_ref, hf_ref, xb_out, zrow, sem, zsem):
    tc = hf_ref.shape[0]

    @pl.when(pl.program_id(0) == 0)
    def _():
        zrow[...] = jnp.zeros(zrow.shape, zrow.dtype)

        def zero_copy(row):
            return pltpu.make_async_copy(zrow.at[pl.ds(0, 1), :], xb_out.at[pl.ds(row, 1), :], zsem)

        def per_segment(op):
            def seg(s, carry):
                base = seg_start_ref[s]

                def row(r, c):
                    op(zero_copy(base + r))
                    return c

                return lax.fori_loop(0, seg_len_ref[s], row, carry)

            lax.fori_loop(0, seg_start_ref.shape[0], seg, 0)

        per_segment(lambda cp: cp.start())
        per_segment(lambda cp: cp.wait())

    for r in range(TOP_K * tc):
        pltpu.make_async_copy(hf_ref.at[pl.ds(r % tc, 1), :],
                              xb_out.at[pl.ds(dest_ref[0, 0, r], 1), :], sem).start(priority=r % 2)
    for _ in range(TOP_K):
        pltpu.make_async_copy(hf_ref, xb_out.at[pl.ds(0, tc), :], sem).wait()


def _experts_kernel(be_ref, nact_ref, x_ref, wgu_ref, bgu_ref, wd_ref, bd_ref, y_ref):
    i = pl.program_id(0)

    @pl.when(i < nact_ref[0])
    def _():
        gu = _dot(x_ref[...].astype(BF16), wgu_ref[0]) + bgu_ref[0]
        acts = []
        for c in range(gu.shape[1] // GU_CHUNK):
            g = jnp.minimum(gu[:, c * GU_CHUNK:c * GU_CHUNK + LANES], SWIGLU_LIMIT)
            u = jnp.clip(gu[:, c * GU_CHUNK + LANES:(c + 1) * GU_CHUNK], -SWIGLU_LIMIT, SWIGLU_LIMIT)
            acts.append(((u + 1.0) * (g * jax.nn.sigmoid(SWIGLU_ALPHA * g))).astype(BF16))
        y_ref[...] = _dot(jnp.concatenate(acts, axis=1), wd_ref[0]) + bd_ref[0]

    @pl.when(i >= nact_ref[0])
    def _():
        y_ref[...] = jnp.zeros(y_ref.shape, y_ref.dtype)


def _combine_kernel(dest_ref, x2_ref, w_ref, y_hbm, op_ref, os_ref, buf, sem, *, n_ptiles):
    tc = x2_ref.shape[0]

    for r in range(TOP_K * tc):
        pltpu.make_async_copy(y_hbm.at[pl.ds(dest_ref[0, 0, r], 1), :], buf.at[pl.ds(r, 1), :],
                              sem).start(priority=r % 2)
    pltpu.make_async_copy(y_hbm.at[pl.ds(0, TOP_K * tc), :], buf, sem).wait()
    acc = x2_ref[...]
    w = w_ref[...]
    for kk in range(TOP_K):
        acc = acc + w[:, kk:kk + 1] * buf[kk * tc:(kk + 1) * tc, :]

    @pl.when(pl.program_id(0) < n_ptiles)
    def _():
        op_ref[...] = acc

    @pl.when(pl.program_id(0) >= n_ptiles)
    def _():
        os_ref[...] = acc


def _moe(hf, x2, top_idx, top_w, w_gate_up, b_gate_up, w_down, b_down, n_prompt):
    n_tok, d = hf.shape
    n_exp, _, ff2 = w_gate_up.shape
    ff = ff2 // 2
    tm, tc = MOE_TILE, COMBINE_TILE
    assert n_tok % tc == 0
    n_assign = n_tok * TOP_K
    n_blocks = -(-(n_assign + n_exp * (tm - 1)) // tm)
    n_rows = n_blocks * tm

    rank, cnt = _rank(top_idx)
    counts = cnt[0, :n_exp].astype(jnp.int32)
    padded = (counts + tm - 1) // tm * tm
    pend = jnp.cumsum(padded)
    pstart = pend - padded
    choice = top_idx[:, :TOP_K]
    first = jnp.sum(jnp.where(choice[:, :, None] == jnp.arange(n_exp, dtype=jnp.int32), pstart, 0), axis=-1)
    dest = (first + rank[:, :TOP_K]).astype(jnp.int32)
    dest_tiles = dest.reshape(n_tok // tc, tc, TOP_K).transpose(0, 2, 1).reshape(n_tok // tc, 1, TOP_K * tc)
    block_e = jnp.minimum(jnp.sum(jnp.arange(n_blocks, dtype=jnp.int32)[None, :] * tm >= pend[:, None], axis=0),
                          n_exp - 1).astype(jnp.int32)
    n_active = (pend[-1:] // tm).astype(jnp.int32)
    seg_start = jnp.concatenate([pstart + counts, pend[-1:]]).astype(jnp.int32)
    seg_len = jnp.concatenate([padded - counts, n_rows - pend[-1:]]).astype(jnp.int32)

    xb = pl.pallas_call(
        _dispatch_kernel,
        grid_spec=pltpu.PrefetchScalarGridSpec(
            num_scalar_prefetch=2,
            grid=(n_tok // tc,),
            in_specs=[pl.BlockSpec((1, 1, TOP_K * tc), lambda i, ss, sl: (i, 0, 0), memory_space=pltpu.SMEM),
                      pl.BlockSpec((tc, d), lambda i, ss, sl: (i, 0))],
            out_specs=pl.BlockSpec(memory_space=pl.ANY),
            scratch_shapes=[pltpu.VMEM((SUBLANES, d), F32), pltpu.SemaphoreType.DMA(()),
                            pltpu.SemaphoreType.DMA(())]),
        out_shape=jax.ShapeDtypeStruct((n_rows, d), F32),
        compiler_params=_cparams(("arbitrary",)),
        name="dispatch",
    )(seg_start, seg_len, dest_tiles, hf)

    wgu = _regroup_gate_up(w_gate_up)
    bgu = b_gate_up.reshape(n_exp, ff2 // GU_CHUNK, LANES, 2).transpose(0, 1, 3, 2).reshape(n_exp, 1, ff2)
    by_e = lambda i, be, na: (be[i], 0, 0)
    yb = pl.pallas_call(
        _experts_kernel,
        grid_spec=pltpu.PrefetchScalarGridSpec(
            num_scalar_prefetch=2,
            grid=(n_blocks,),
            in_specs=[pl.BlockSpec((tm, d), lambda i, be, na: (jnp.minimum(i, na[0] - 1), 0)),
                      pl.BlockSpec((1, d, ff2), by_e), pl.BlockSpec((1, 1, ff2), by_e),
                      pl.BlockSpec((1, ff, d), by_e), pl.BlockSpec((1, 1, d), by_e)],
            out_specs=pl.BlockSpec((tm, d), lambda i, be, na: (i, 0))),
        out_shape=jax.ShapeDtypeStruct((n_rows, d), F32),
        compiler_params=_cparams(("arbitrary",)),
        name="experts",
    )(block_e, n_active, xb, wgu, bgu, w_down.astype(BF16), b_down.reshape(n_exp, 1, d))

    assert n_prompt % tc == 0 and 0 < n_prompt < n_tok
    n_ptiles = n_prompt // tc
    return pl.pallas_call(
        functools.partial(_combine_kernel, n_ptiles=n_ptiles),
        grid=(n_tok // tc,),
        in_specs=[pl.BlockSpec((1, 1, TOP_K * tc), lambda i: (i, 0, 0), memory_space=pltpu.SMEM),
                  pl.BlockSpec((tc, d), lambda i: (i, 0)),
                  pl.BlockSpec((tc, LANES), lambda i: (i, 0)),
                  pl.BlockSpec(memory_space=pl.ANY)],
        out_specs=[pl.BlockSpec((tc, d), lambda i: (jnp.minimum(i, n_ptiles - 1), 0)),
                   pl.BlockSpec((tc, d), lambda i: (jnp.maximum(i - n_ptiles, 0), 0))],
        out_shape=[jax.ShapeDtypeStruct((n_prompt, d), F32),
                   jax.ShapeDtypeStruct((n_tok - n_prompt, d), F32)],
        scratch_shapes=[pltpu.VMEM((TOP_K * tc, d), F32), pltpu.SemaphoreType.DMA(())],
        compiler_params=_cparams(("arbitrary",)),
        name="combine",
    )(dest_tiles, x2, top_w, yb)


def kernel(x_prompt, x_sample, cache_win_k, cache_win_v, state_conv, state_ssm, cache_mem_k, cache_mem_v,
           mem_prompt, norm_mix, w_in, q_gain, k_gain, conv_w, conv_b, dt_bias, a_log, d_skip, ssm_out_gain,
           w_out, norm_mem, mem_in_gain, w_mem_q, w_mem_k, w_mem_v, mem_q_gain, mem_k_gain, w_mem_o,
           norm_ffn, w_router, b_router, w_gate_up, b_gate_up, w_down, b_down):
    bp, lp, d = x_prompt.shape
    bs, ls, _ = x_sample.shape
    depth = norm_mix.shape[0]
    n_buf = cache_win_k.shape[2]
    past_len = PAST_LEN
    n_mem = mem_prompt.shape[1]
    npr, nsm = bp * lp, bs * ls
    att_w = d // 2
    n_heads = att_w // ATT_HEAD_DIM
    ssm_w = d - att_w
    conv_ch = ssm_w + 2 * SSM_GROUPS * SSM_STATE
    tail = CONV_WIDTH - 1
    keep = min(max(w for w, _ in DILATIONS), lp)
    ls_pad = SUBLANES

    x = jnp.concatenate([x_prompt.reshape(npr, d), x_sample.reshape(nsm, d)]).astype(F32)
    outs = [[] for _ in range(10)]
    for i in range(depth):
        q, k, v, z, xbc, dt_raw = _projections(x, norm_mix[i], w_in[i], q_gain[i], k_gain[i],
                                               lp, ls, past_len, npr)
        smp = lambda a: a[npr:].reshape(bs, ls, a.shape[-1])
        pad_s = lambda a: jnp.pad(smp(a), ((0, 0), (0, ls_pad - ls), (0, 0)))
        pad_rows = lambda a: pad_s(a).reshape(bs * ls_pad, a.shape[-1])
        unpad = lambda a: a.reshape(bs, ls_pad, a.shape[-1])[:, :ls].reshape(nsm, a.shape[-1])
        att_p = _attn_prompt(q, k, v, bp, lp)
        heads = lambda a, b, l: a.reshape(b, l, n_heads, ATT_HEAD_DIM)
        k_new, v_new = heads(k[npr:], bs, ls), heads(v[npr:], bs, ls)
        cache_t = lambda c: jnp.transpose(c, (0, 2, 3, 1)).reshape(bs, att_w, n_buf)
        att_s = _attn_sample(smp(q), smp(k), smp(v), cache_t(cache_win_k[i]), cache_t(cache_win_v[i]))
        ssm_par = (conv_w[i], conv_b[i], dt_bias[i], a_log[i], d_skip[i], ssm_out_gain[i])
        y_p, st_p = _ssd(xbc, dt_raw, z, jnp.zeros((bp, tail, conv_ch), F32),
                         jnp.zeros((bp, ssm_w // SSM_HEAD_DIM, SSM_HEAD_DIM, SSM_STATE), F32), *ssm_par,
                         bp, lp, lp)
        y_s, st_s = _ssd(pad_rows(xbc), pad_rows(dt_raw), pad_rows(z), state_conv[i], state_ssm[i], *ssm_par,
                         bs, ls_pad, ls)
        att = jnp.concatenate([att_p, att_s.reshape(nsm, att_w)])
        yssm = jnp.concatenate([y_p, unpad(y_s)])
        x1, qm = _out_proj(x, att, yssm, w_out[i], norm_mem[i], w_mem_q[i], mem_q_gain[i])
        mk_p, mv_p = _mem_kv(mem_prompt.reshape(bp * n_mem, d).astype(F32), mem_in_gain[i], w_mem_k[i],
                             w_mem_v[i], mem_k_gain[i])
        mw = mk_p.shape[-1]
        o_p = _mem_attn(qm, mk_p.reshape(bp, n_mem, mw), mv_p.reshape(bp, n_mem, mw), bp, lp, TOKEN_TILE)
        o_s = _mem_attn(pad_rows(qm), cache_mem_k[i].reshape(bs, n_mem, mw),
                        cache_mem_v[i].reshape(bs, n_mem, mw), bs, ls_pad, ls_pad)
        o = jnp.concatenate([o_p, unpad(o_s)])
        x2, hf, top_idx, top_w = _post(x1, o, w_mem_o[i], norm_ffn[i], w_router[i], b_router[i])
        xp_out, xs_out = _moe(hf, x2, top_idx, top_w, w_gate_up[i], b_gate_up[i], w_down[i], b_down[i], npr)
        if i + 1 < depth:
            x = jnp.concatenate([xp_out, xs_out])

        xbc_p = xbc[:npr].reshape(bp, lp, conv_ch)
        full_s = jnp.concatenate([state_conv[i].astype(F32), smp(xbc)], axis=1)
        new = (heads(k[:npr], bp, lp)[:, lp - keep:], heads(v[:npr], bp, lp)[:, lp - keep:],
               k_new, v_new,
               xbc_p[:, lp - tail:], full_s[:, ls:],
               st_p, st_s,
               mk_p.reshape(bp, n_mem, mw // MEM_HEAD_DIM, MEM_HEAD_DIM),
               mv_p.reshape(bp, n_mem, mw // MEM_HEAD_DIM, MEM_HEAD_DIM))
        for lst, val in zip(outs, new):
            lst.append(val)
    y_p = xp_out.reshape(bp, lp, d).astype(x_prompt.dtype)
    y_s = xs_out.reshape(bs, ls, d).astype(x_sample.dtype)
    return (y_p, y_s) + tuple(jnp.stack(o) for o in outs)
```

```python
import functools
import math

import numpy as np
import jax
import jax.numpy as jnp
from jax import lax
from jax.experimental import pallas as pl
from jax.experimental.pallas import tpu as pltpu

F32 = jnp.float32
BF16 = jnp.bfloat16

ATT_HEAD_DIM = 64
DILATIONS = ((128, 1), (512, 4), (2048, 16))
ATT_BLOCK = 128
ROPE_DIM = ATT_HEAD_DIM // 4
ROPE_THETA = 500000.0
PAST_LEN = 8192
SSM_HEAD_DIM = 64
SSM_GROUPS = 2
SSM_STATE = 128
CONV_WIDTH = 4
SSM_CHUNK = 128
MEM_HEAD_DIM = 128
TOP_K = 4
SWIGLU_LIMIT = 7.0
SWIGLU_ALPHA = 1.702
NORM_EPS = 1e-6

LANES = 128
SUBLANES = 8
VMEM_LIMIT = 56 * 1024 * 1024

TOKEN_TILE = 512
MOE_TILE = 256
COMBINE_TILE = 256
ATTN_UNROLL = 8
LOG2E = math.log2(math.e)
NEG = -1e30


def _cparams(sem):
    return pltpu.CompilerParams(dimension_semantics=sem, vmem_limit_bytes=VMEM_LIMIT)


def _rms(x, gain):
    return x * lax.rsqrt(jnp.mean(x * x, axis=-1, keepdims=True) + NORM_EPS) * gain


def _dot(a, b):
    return jnp.dot(a, b, preferred_element_type=F32)


def _dot_nt(a, b):
    return lax.dot_general(a, b, (((1,), (1,)), ((), ())), preferred_element_type=F32)


def _dot_tn(a, b):
    return lax.dot_general(a, b, (((0,), (0,)), ((), ())), preferred_element_type=F32)


def _dot_f32(a, b):
    return jnp.dot(a, b, preferred_element_type=F32, precision=lax.Precision.HIGHEST)


def _two_source(i, n_first, first_ref, second_ref):
    return lax.cond(i < n_first, lambda: first_ref[...], lambda: second_ref[...])


def _proj_kernel(xp_ref, xs_ref, g_ref, w_ref, qg_ref, kg_ref, seg_ref, cos_ref, s1_ref, s2_ref,
                 q_ref, k_ref, v_ref, z_ref, xbc_ref, dt_ref, kt_ref, vt_ref, *,
                 att_w, ssm_w, conv_ch, n_ptiles):
    i = pl.program_id(0)
    h = _rms(_two_source(i, n_ptiles, xp_ref, xs_ref), g_ref[...]).astype(BF16)
    seg = seg_ref[...]
    cos, s1, s2 = cos_ref[...], s1_ref[...], s2_ref[...]

    def head_norm_rope(t, gain):
        sq = t * t
        hi = sq.astype(BF16)
        lo = (sq - hi.astype(F32)).astype(BF16)
        ms = (_dot(hi, seg) + _dot(lo, seg)) * (1.0 / ATT_HEAD_DIM)
        tn = t * lax.rsqrt(ms + NORM_EPS) * gain
        half = ROPE_DIM // 2
        return (tn * cos + pltpu.roll(tn, half, 1) * s1
                + pltpu.roll(tn, att_w - half, 1) * s2)

    q = head_norm_rope(_dot(h, w_ref[:, 0:att_w]), qg_ref[...])
    q_ref[...] = q * (ATT_HEAD_DIM ** -0.5 * LOG2E)
    k = head_norm_rope(_dot(h, w_ref[:, att_w:2 * att_w]), kg_ref[...])
    v = _dot(h, w_ref[:, 2 * att_w:3 * att_w])
    k_ref[...] = k
    v_ref[...] = v

    @pl.when(i < n_ptiles)
    def _():
        kt_ref[0] = k.T
        vt_ref[0] = v.T

    o = 3 * att_w
    z_ref[...] = _dot(h, w_ref[:, o:o + ssm_w])
    o += ssm_w
    xbc_ref[...] = _dot(h, w_ref[:, o:o + conv_ch])
    o += conv_ch
    dt_ref[...] = _dot(h, w_ref[:, o:o + LANES])


def _rope_tables(pos, n_heads):
    half = ROPE_DIM // 2
    inv_freq = jnp.power(ROPE_THETA, -jnp.arange(half, dtype=F32) / half)
    ang = pos.astype(F32)[:, None] * inv_freq[None, :]
    cos, sin = jnp.cos(ang), jnp.sin(ang)
    n = pos.shape[0]
    rest = ATT_HEAD_DIM - ROPE_DIM
    c = jnp.concatenate([cos, cos, jnp.ones((n, rest), F32)], axis=-1)
    s1 = jnp.concatenate([jnp.zeros((n, half), F32), sin, jnp.zeros((n, rest), F32)], axis=-1)
    s2 = jnp.concatenate([-sin, jnp.zeros((n, half + rest), F32)], axis=-1)
    return tuple(jnp.tile(t, (1, n_heads)) for t in (c, s1, s2))


def _projections(x_p, x_s, norm_mix, w_in, q_gain, k_gain, seq, dec_seq, past_len):
    n_prompt, d = x_p.shape
    n = n_prompt + x_s.shape[0]
    d_half = d // 2
    att_w, ssm_w = d_half, d - d_half
    n_heads = att_w // ATT_HEAD_DIM
    ssm_heads = ssm_w // SSM_HEAD_DIM
    conv_ch = ssm_w + 2 * SSM_GROUPS * SSM_STATE
    tm = TOKEN_TILE
    assert n % tm == 0 and n_prompt % tm == 0 and seq % tm == 0 and (n - n_prompt) == tm
    assert tm % dec_seq == 0
    c0 = 3 * att_w + ssm_w
    w = jnp.concatenate([w_in[:, :c0], w_in[:, c0 + ssm_heads:],
                         w_in[:, c0:c0 + ssm_heads],
                         jnp.zeros((d, LANES - ssm_heads), w_in.dtype)], axis=1).astype(BF16)
    wn = w.shape[1]
    pos = jnp.concatenate([jnp.arange(seq, dtype=jnp.int32),
                           past_len + jnp.arange(tm, dtype=jnp.int32) % dec_seq])
    cos, s1, s2 = _rope_tables(pos, n_heads)
    tiles_per_seq = seq // tm
    n_prompt_tiles = n_prompt // tm
    head_id = np.arange(att_w) // ATT_HEAD_DIM
    seg = jnp.asarray(head_id[:, None] == head_id[None, :], BF16)

    def tab_map(i):
        return (jnp.where(i < n_prompt_tiles, i % tiles_per_seq, tiles_per_seq), 0)

    row = lambda i: (i, 0)
    fix = lambda i: (0, 0)
    first = lambda i: (jnp.minimum(i, n_prompt_tiles - 1), 0)

    def t_map(i):
        j = jnp.minimum(i, n_prompt_tiles - 1)
        return (j // tiles_per_seq, 0, j % tiles_per_seq)

    tab = pl.BlockSpec((tm, att_w), tab_map)
    tr = pl.BlockSpec((1, att_w, tm), t_map)
    kern = functools.partial(_proj_kernel, att_w=att_w, ssm_w=ssm_w, conv_ch=conv_ch, n_ptiles=n_prompt_tiles)
    return pl.pallas_call(
        kern,
        grid=(n // tm,),
        in_specs=[pl.BlockSpec((tm, d), first), pl.BlockSpec((tm, d), fix), pl.BlockSpec((1, d), fix),
                  pl.BlockSpec((d, wn), fix), pl.BlockSpec((1, att_w), fix),
                  pl.BlockSpec((1, att_w), fix), pl.BlockSpec((att_w, att_w), fix),
                  tab, tab, tab],
        out_specs=[pl.BlockSpec((tm, att_w), row), pl.BlockSpec((tm, att_w), row),
                   pl.BlockSpec((tm, att_w), row), pl.BlockSpec((tm, ssm_w), row),
                   pl.BlockSpec((tm, conv_ch), row), pl.BlockSpec((tm, LANES), row), tr, tr],
        out_shape=[jax.ShapeDtypeStruct((n, att_w), F32), jax.ShapeDtypeStruct((n, att_w), F32),
                   jax.ShapeDtypeStruct((n, att_w), F32), jax.ShapeDtypeStruct((n, ssm_w), F32),
                   jax.ShapeDtypeStruct((n, conv_ch), F32), jax.ShapeDtypeStruct((n, LANES), F32),
                   jax.ShapeDtypeStruct((n_prompt // seq, att_w, seq), F32),
                   jax.ShapeDtypeStruct((n_prompt // seq, att_w, seq), F32)],
        compiler_params=_cparams(("arbitrary",)),
        name="proj",
    )(x_p, x_s, norm_mix.reshape(1, d), w, jnp.tile(q_gain, n_heads).reshape(1, att_w),
      jnp.tile(k_gain, n_heads).reshape(1, att_w), seg, cos, s1, s2)


def _attn_prompt_kernel(q_ref, k_ref, v_ref, o_ref, num_ref, m_ref, den_ref, *, seq):
    blk = ATT_BLOCK
    lane = lax.broadcasted_iota(jnp.int32, (blk, LANES), 1)
    head0 = lane < ATT_HEAD_DIM
    qi = lax.broadcasted_iota(jnp.int32, (blk, 2 * blk), 0) + blk
    ki = lax.broadcasted_iota(jnp.int32, (blk, 2 * blk), 1)
    dist = qi - ki
    band = (dist >= 0) & (dist <= blk)
    bias_rest = jnp.where(band, 0.0, NEG)
    bias_first = jnp.where(band & (ki >= blk), 0.0, NEG)

    for di, (window, dil) in enumerate(DILATIONS):
        assert window // dil == blk
        nb = seq // dil // blk

        def body(i, carry, dil=dil, nb=nb, di=di):
            r = i // nb
            j = i % nb
            if dil > 1:
                start = r + dil * blk * j
                prev = r + dil * blk * jnp.maximum(j - 1, 0)
                rows = pl.ds(start, blk, stride=dil)
                prows = pl.ds(prev, blk, stride=dil)
            else:
                rows = pl.ds(pl.multiple_of(blk * j, blk), blk)
                prows = pl.ds(pl.multiple_of(blk * jnp.maximum(j - 1, 0), blk), blk)
            qb = q_ref[rows, :].astype(BF16)
            k2 = jnp.concatenate([k_ref[prows, :], k_ref[rows, :]], axis=0).astype(BF16)
            v2 = jnp.concatenate([v_ref[prows, :], v_ref[rows, :]], axis=0).astype(BF16)
            bias = jnp.where(j > 0, bias_rest, bias_first)
            zero = jnp.zeros_like(qb)
            q2 = jnp.concatenate([jnp.where(head0, qb, zero), jnp.where(head0, zero, qb)], axis=0)
            s = _dot_nt(q2, k2) + jnp.concatenate([bias, bias], axis=0)
            m = jnp.max(s, axis=-1, keepdims=True)
            p = jnp.exp2(s - m)
            den = jnp.sum(p, axis=-1, keepdims=True)
            pv = _dot(p.astype(BF16), v2)
            num_ref[di, rows, :] = jnp.where(head0, pv[:blk], pv[blk:])
            m_ref[di, rows, :] = jnp.where(head0, m[:blk], m[blk:])
            den_ref[di, rows, :] = jnp.where(head0, den[:blk], den[blk:])
            return carry

        lax.fori_loop(0, dil * nb, body, 0, unroll=ATTN_UNROLL)

    m_all = jnp.maximum(jnp.maximum(m_ref[0], m_ref[1]), m_ref[2])
    num = jnp.zeros((seq, LANES), F32)
    den = jnp.zeros((seq, LANES), F32)
    for di in range(len(DILATIONS)):
        a = jnp.exp2(m_ref[di] - m_all)
        num = num + a * num_ref[di]
        den = den + a * den_ref[di]
    o_ref[...] = (num / den).astype(o_ref.dtype)


def _attn_prompt(q, k, v, batch, seq):
    att_w = q.shape[1]
    pairs = att_w // LANES
    nd = len(DILATIONS)
    blk = pl.BlockSpec((seq, LANES), lambda b, h: (b, h))
    return pl.pallas_call(
        functools.partial(_attn_prompt_kernel, seq=seq),
        grid=(batch, pairs),
        in_specs=[blk, blk, blk],
        out_specs=blk,
        out_shape=jax.ShapeDtypeStruct((batch * seq, att_w), BF16),
        scratch_shapes=[pltpu.VMEM((nd, seq, LANES), F32), pltpu.VMEM((nd, seq, LANES), F32),
                        pltpu.VMEM((nd, seq, LANES), F32)],
        compiler_params=_cparams(("parallel", "parallel")),
        name="attn_prompt",
    )(q, k, v)


def _attn_sample_kernel(q_ref, kn_ref, vn_ref, kc_ref, vc_ref, cc_ref, cn_ref, o_ref, *,
                        dec_seq, n_heads):
    w = q_ref.shape[-1]
    rows = dec_seq * n_heads
    q = q_ref[0]
    qm = jnp.concatenate([jnp.broadcast_to(q[t:t + 1], (n_heads, w)) for t in range(dec_seq)], axis=0)
    lane_head = lax.broadcasted_iota(jnp.int32, (rows, w), 1) // ATT_HEAD_DIM
    row_head = lax.broadcasted_iota(jnp.int32, (rows, w), 0) % n_heads
    own = lane_head == row_head
    qm = jnp.where(own, qm, 0.0).astype(BF16)
    zpad = jnp.zeros((SUBLANES - dec_seq, w), F32)
    kn = jnp.concatenate([kn_ref[0], zpad], axis=0).astype(BF16)
    vn = jnp.concatenate([vn_ref[0], zpad], axis=0).astype(BF16)
    cc, cn = cc_ref[...], cn_ref[...]
    s_c = jnp.where(cc > 0, _dot(qm, kc_ref[0].astype(BF16)), -jnp.inf)
    s_n = jnp.where(cn > 0, _dot_nt(qm, kn), -jnp.inf)
    m = jnp.maximum(jnp.max(s_c, axis=-1, keepdims=True), jnp.max(s_n, axis=-1, keepdims=True))
    p_c = cc * jnp.exp2(s_c - m)
    p_n = cn * jnp.exp2(s_n - m)
    den = jnp.sum(p_c, axis=-1, keepdims=True) + jnp.sum(p_n, axis=-1, keepdims=True)
    o = _dot_nt(p_c.astype(BF16), vc_ref[0].astype(BF16)) + _dot(p_n.astype(BF16), vn)
    o = jnp.where(own, o / den, 0.0)
    o_ref[0] = jnp.sum(o.reshape(dec_seq, n_heads, w), axis=1).astype(o_ref.dtype)


def _attn_sample(q, k_new, v_new, k_cache_t, v_cache_t):
    b, t, w = q.shape
    n_buf = k_cache_t.shape[2]
    n_heads = w // ATT_HEAD_DIM
    assert n_heads == SUBLANES and t <= SUBLANES
    assert n_buf >= max(win for win, _ in DILATIONS)

    def count(dist):
        return sum(((dist >= 0) & (dist % dil == 0) & (dist <= win)).astype(np.float32)
                   for win, dil in DILATIONS)

    tq = np.repeat(np.arange(t), n_heads)[:, None]
    cc = count(n_buf + tq - np.arange(n_buf)[None, :])
    jn = np.arange(SUBLANES)[None, :]
    cn = np.where(jn < t, count(tq - jn), 0.0).astype(np.float32)
    rows = t * n_heads
    new = pl.BlockSpec((1, t, w), lambda i: (i, 0, 0))
    cache = pl.BlockSpec((1, w, n_buf), lambda i: (i, 0, 0))
    return pl.pallas_call(
        functools.partial(_attn_sample_kernel, dec_seq=t, n_heads=n_heads),
        grid=(b,),
        in_specs=[new, new, new, cache, cache,
                  pl.BlockSpec((rows, n_buf), lambda i: (0, 0)),
                  pl.BlockSpec((rows, SUBLANES), lambda i: (0, 0))],
        out_specs=new,
        out_shape=jax.ShapeDtypeStruct((b, t, w), BF16),
        compiler_params=_cparams(("parallel",)),
        name="attn_sample",
    )(q, k_new, v_new, k_cache_t, v_cache_t, jnp.asarray(cc), jnp.asarray(cn))


def _softplus(x):
    return jnp.maximum(x, 0.0) + jnp.log1p(jnp.exp(-jnp.abs(x)))


def _silu(x):
    return x * jax.nn.sigmoid(x)


def _ssd_kernel(xbc_ref, dt_ref, dtt_ref, z_ref, cp_ref, h0_ref, cw_ref, cb_ref, dtb_ref, dtbt_ref,
                al_ref, alt_ref, dsk_ref, og_ref, y_ref, hl_ref, ct_ref, xpad_ref, h_ref, *,
                q, valid, ssm_w, n_pairs):
    c = pl.program_id(1)
    tail = CONV_WIDTH - 1

    @pl.when(c == 0)
    def _():
        h_ref[...] = h0_ref[0]
        xpad_ref[0:SUBLANES, :] = jnp.zeros((SUBLANES, xpad_ref.shape[1]), F32)
        xpad_ref[SUBLANES - tail:SUBLANES, :] = cp_ref[0]

    x = xbc_ref[...]
    xpad_ref[SUBLANES:SUBLANES + q, :] = x
    conv = cb_ref[...]
    for j in range(CONV_WIDTH):
        o = SUBLANES - tail + j
        conv = conv + xpad_ref[o:o + q, :] * cw_ref[j:j + 1, :]

    @pl.when(c == pl.num_programs(1) - 1)
    def _():
        ct_ref[0] = xpad_ref[valid:valid + SUBLANES, :]

    xpad_ref[0:SUBLANES, :] = x[q - SUBLANES:q, :]
    u = _silu(conv)
    xs = u[:, :ssm_w]
    gw = SSM_STATE
    bm = u[:, ssm_w:ssm_w + SSM_GROUPS * gw].astype(BF16)
    cm = u[:, ssm_w + SSM_GROUPS * gw:].astype(BF16)

    dt = _softplus(dt_ref[...] + dtb_ref[...])
    dtt = _softplus(dtt_ref[0] + dtbt_ref[...])
    if valid < q:
        dt = jnp.where(lax.broadcasted_iota(jnp.int32, dt.shape, 0) < valid, dt, 0.0)
        dtt = jnp.where(lax.broadcasted_iota(jnp.int32, dtt.shape, 1) < valid, dtt, 0.0)
    ri = lax.broadcasted_iota(jnp.int32, (q, q), 0)
    ci = lax.broadcasted_iota(jnp.int32, (q, q), 1)
    causal = ci <= ri
    acum = _dot_f32(causal.astype(F32), dt * -jnp.exp(al_ref[...]))
    acumt = _dot_f32(dtt * -jnp.exp(alt_ref[...]), (ri <= ci).astype(F32))

    left = lax.broadcasted_iota(jnp.int32, (q, LANES), 1) < SSM_HEAD_DIM
    top = lax.broadcasted_iota(jnp.int32, (LANES, LANES), 0) < SSM_HEAD_DIM
    rep = 2 * n_pairs // SSM_GROUPS
    gmat = [_dot_nt(cm[:, g * gw:(g + 1) * gw], bm[:, g * gw:(g + 1) * gw]) for g in range(SSM_GROUPS)]
    ys = []
    for pr in range(n_pairs):
        ha, hb = 2 * pr, 2 * pr + 1
        g = ha // rep
        bg = bm[:, g * gw:(g + 1) * gw]
        cg = cm[:, g * gw:(g + 1) * gw]
        xpair = xs[:, pr * LANES:(pr + 1) * LANES]
        xd = xpair * jnp.where(left, dt[:, ha:ha + 1], dt[:, hb:hb + 1])
        xdb = xd.astype(BF16)
        yd = []
        for hh in (ha, hb):
            seg = acum[:, hh:hh + 1] - acumt[hh:hh + 1, :]
            decay = jnp.exp(jnp.where(causal, seg, -jnp.inf))
            yd.append(_dot((gmat[g] * decay).astype(BF16), xdb))
        ac = jnp.where(left, acum[:, ha:ha + 1], acum[:, hb:hb + 1])
        hprev = h_ref[pr]
        y_off = _dot_nt(cg, hprev.astype(BF16)) * jnp.exp(ac)
        to_end = jnp.exp(ac[q - 1:q, :] - ac)
        upd = _dot_tn((xd * to_end).astype(BF16), bg)
        cdec = jnp.where(top, jnp.exp(acum[q - 1:q, ha:ha + 1]), jnp.exp(acum[q - 1:q, hb:hb + 1]))
        h_ref[pr] = hprev * cdec + upd
        ys.append(jnp.where(left, yd[0], yd[1]) + y_off
                  + xpair * dsk_ref[:, pr * LANES:(pr + 1) * LANES])
    y = jnp.concatenate(ys, axis=1) * _silu(z_ref[...])
    y_ref[...] = _rms(y, og_ref[...]).astype(y_ref.dtype)

    @pl.when(c == pl.num_programs(1) - 1)
    def _():
        hl_ref[0] = h_ref[...]


def _ssd(xbc, dt_raw, z, conv_prev, h0, conv_w, conv_b, dt_bias, a_log, d_skip, out_gain, b, l, valid_len):
    conv_ch = xbc.shape[-1]
    ssm_w = z.shape[-1]
    n_heads = ssm_w // SSM_HEAD_DIM
    n_pairs = n_heads // 2
    assert 2 * SSM_HEAD_DIM == LANES and SSM_STATE == LANES and n_heads <= SUBLANES
    assert (n_heads // SSM_GROUPS) % 2 == 0
    q = SSM_CHUNK if l % SSM_CHUNK == 0 else l
    assert l % q == 0 and q % SUBLANES == 0 and (valid_len == l or q == l)
    nc = l // q
    dtt = jnp.swapaxes(dt_raw[:b * l, :n_heads].reshape(b, l, n_heads), 1, 2)
    pad_h = lambda a: jnp.pad(a.reshape(1, n_heads), ((0, 0), (0, LANES - n_heads)))
    per_lane = lambda a: jnp.repeat(a, SSM_HEAD_DIM).reshape(1, ssm_w)
    h0p = h0.reshape(b, n_pairs, LANES, SSM_STATE).astype(F32)
    tail = CONV_WIDTH - 1
    tok = lambda w: pl.BlockSpec((q, w), lambda i, j: (i * nc + j, 0))
    fix2 = lambda r, w: pl.BlockSpec((r, w), lambda i, j: (0, 0))
    st = pl.BlockSpec((1, n_pairs, LANES, SSM_STATE), lambda i, j: (i, 0, 0, 0))
    kern = functools.partial(_ssd_kernel, q=q, valid=valid_len if q == l else q, ssm_w=ssm_w, n_pairs=n_pairs)
    y, h_last, conv_tail = pl.pallas_call(
        kern,
        grid=(b, nc),
        in_specs=[tok(conv_ch), tok(LANES), pl.BlockSpec((1, n_heads, q), lambda i, j: (i, 0, j)),
                  tok(ssm_w), pl.BlockSpec((1, tail, conv_ch), lambda i, j: (i, 0, 0)), st,
                  fix2(CONV_WIDTH, conv_ch), fix2(1, conv_ch), fix2(1, LANES), fix2(n_heads, 1),
                  fix2(1, LANES), fix2(n_heads, 1), fix2(1, ssm_w), fix2(1, ssm_w)],
        out_specs=[tok(ssm_w), st, pl.BlockSpec((1, SUBLANES, conv_ch), lambda i, j: (i, 0, 0))],
        out_shape=[jax.ShapeDtypeStruct((b * l, ssm_w), BF16),
                   jax.ShapeDtypeStruct((b, n_pairs, LANES, SSM_STATE), F32),
                   jax.ShapeDtypeStruct((b, SUBLANES, conv_ch), F32)],
        scratch_shapes=[pltpu.VMEM((q + SUBLANES, conv_ch), F32),
                        pltpu.VMEM((n_pairs, LANES, SSM_STATE), F32)],
        compiler_params=_cparams(("parallel", "arbitrary")),
        name="ssd",
    )(xbc, dt_raw, dtt, z, conv_prev.astype(F32), h0p, conv_w, conv_b.reshape(1, conv_ch),
      pad_h(dt_bias), dt_bias.reshape(n_heads, 1), pad_h(a_log), a_log.reshape(n_heads, 1),
      per_lane(d_skip), out_gain.reshape(1, ssm_w))
    return y, h_last.reshape(b, n_heads, SSM_HEAD_DIM, SSM_STATE), conv_tail[:, SUBLANES - tail:]


def _head_rms_store(dst_ref, t, gain, scale=None):
    for hd in range(t.shape[1] // MEM_HEAD_DIM):
        sl = slice(hd * MEM_HEAD_DIM, (hd + 1) * MEM_HEAD_DIM)
        r = _rms(t[:, sl], gain)
        if scale is not None:
            r = r * scale
        dst_ref[:, sl] = r.astype(dst_ref.dtype)


def _mem_kv_kernel(m_ref, g_ref, wk_ref, wv_ref, kg_ref, k_ref, v_ref):
    h = _rms(m_ref[...], g_ref[...]).astype(BF16)
    _head_rms_store(k_ref, _dot(h, wk_ref[...]), kg_ref[...])
    v_ref[...] = _dot(h, wv_ref[...])


def _mem_kv(mem, in_gain, w_k, w_v, k_gain):
    n, d = mem.shape
    mw = w_k.shape[1]
    tm = TOKEN_TILE
    assert n % tm == 0 and MEM_HEAD_DIM == LANES
    row = lambda i: (i, 0)
    fix = lambda i: (0, 0)
    return pl.pallas_call(
        _mem_kv_kernel,
        grid=(n // tm,),
        in_specs=[pl.BlockSpec((tm, d), row), pl.BlockSpec((1, d), fix), pl.BlockSpec((d, mw), fix),
                  pl.BlockSpec((d, mw), fix), pl.BlockSpec((1, MEM_HEAD_DIM), fix)],
        out_specs=[pl.BlockSpec((tm, mw), row), pl.BlockSpec((tm, mw), row)],
        out_shape=[jax.ShapeDtypeStruct((n, mw), F32), jax.ShapeDtypeStruct((n, mw), F32)],
        compiler_params=_cparams(("parallel",)),
        name="mem_kv",
    )(mem, in_gain.reshape(1, d), w_k.astype(BF16), w_v.astype(BF16), k_gain.reshape(1, MEM_HEAD_DIM))


def _out_proj_kernel(xp_ref, xs_ref, ap_ref, as_ref, yp_ref, ys_ref, wo_ref, g_ref, wq_ref, qg_ref,
                     x1_ref, q_ref, *, n_ptiles):
    i = pl.program_id(0)
    att_w = ap_ref.shape[1]
    x1 = (_two_source(i, n_ptiles, xp_ref, xs_ref)
          + _dot(_two_source(i, n_ptiles, ap_ref, as_ref), wo_ref[0:att_w, :])
          + _dot(_two_source(i, n_ptiles, yp_ref, ys_ref), wo_ref[att_w:, :]))
    x1_ref[...] = x1
    h = _rms(x1, g_ref[...]).astype(BF16)
    _head_rms_store(q_ref, _dot(h, wq_ref[...]), qg_ref[...], MEM_HEAD_DIM ** -0.5)


def _out_proj(x, att, yssm, w_out, norm_mem, w_mem_q, mem_q_gain):
    tm = TOKEN_TILE
    n_prompt, d = x[0].shape
    assert all(a[1].shape[0] == tm and a[0].shape[0] == n_prompt for a in (x, att, yssm))
    n = n_prompt + tm
    n_ptiles = n_prompt // tm
    att_w, ssm_w = att[0].shape[1], yssm[0].shape[1]
    mw = w_mem_q.shape[1]
    row = lambda i: (i, 0)
    fix = lambda i: (0, 0)
    first = lambda i: (jnp.minimum(i, n_ptiles - 1), 0)
    pair = lambda w: [pl.BlockSpec((tm, w), first), pl.BlockSpec((tm, w), fix)]
    return pl.pallas_call(
        functools.partial(_out_proj_kernel, n_ptiles=n_ptiles),
        grid=(n // tm,),
        in_specs=pair(d) + pair(att_w) + pair(ssm_w)
        + [pl.BlockSpec((att_w + ssm_w, d), fix), pl.BlockSpec((1, d), fix),
           pl.BlockSpec((d, mw), fix), pl.BlockSpec((1, MEM_HEAD_DIM), fix)],
        out_specs=[pl.BlockSpec((tm, d), row), pl.BlockSpec((tm, mw), row)],
        out_shape=[jax.ShapeDtypeStruct((n, d), F32), jax.ShapeDtypeStruct((n, mw), BF16)],
        compiler_params=_cparams(("parallel",)),
        name="out_proj",
    )(*x, *att, *yssm, w_out.astype(BF16), norm_mem.reshape(1, d), w_mem_q.astype(BF16),
      mem_q_gain.reshape(1, MEM_HEAD_DIM))


def _mem_attn_kernel(q_ref, k_ref, v_ref, o_ref):
    for hd in range(q_ref.shape[-1] // MEM_HEAD_DIM):
        sl = slice(hd * MEM_HEAD_DIM, (hd + 1) * MEM_HEAD_DIM)
        head = (lambda r: r[0, :, hd, :]) if len(k_ref.shape) == 4 else (lambda r: r[0, :, sl])
        s = _dot_nt(q_ref[:, sl], head(k_ref).astype(BF16))
        p = jnp.exp(s - jnp.max(s, axis=-1, keepdims=True))
        den = jnp.sum(p, axis=-1, keepdims=True)
        o_ref[:, sl] = (_dot(p.astype(BF16), head(v_ref).astype(BF16)) / den).astype(o_ref.dtype)


def _mem_attn(q, mem_k, mem_v, b, l, tq):
    w = q.shape[-1]
    n_mem = mem_k.shape[1]
    assert l % tq == 0
    qs = pl.BlockSpec((tq, w), lambda i, j: (i * (l // tq) + j, 0))
    ms = pl.BlockSpec((1,) + mem_k.shape[1:], lambda i, j: (i,) + (0,) * (mem_k.ndim - 1))
    return pl.pallas_call(
        _mem_attn_kernel,
        grid=(b, l // tq),
        in_specs=[qs, ms, ms],
        out_specs=qs,
        out_shape=jax.ShapeDtypeStruct((b * l, w), BF16),
        compiler_params=_cparams(("parallel", "parallel")),
        name="mem_attn",
    )(q, mem_k, mem_v)


def _post_kernel(x1_ref, op_ref, os_ref, wo_ref, g_ref, wrh_ref, wrl_ref, br_ref,
                 x2_ref, hf_ref, idx_ref, w_ref, *, n_ptiles):
    x2 = x1_ref[...] + _dot(_two_source(pl.program_id(0), n_ptiles, op_ref, os_ref), wo_ref[...])
    x2_ref[...] = x2
    hf = _rms(x2, g_ref[...])
    hf_ref[...] = hf
    hi = hf.astype(BF16)
    lo = (hf - hi.astype(F32)).astype(BF16)
    logits = (_dot(hi, wrh_ref[...]) + _dot(lo, wrh_ref[...]) + _dot(hi, wrl_ref[...])) + br_ref[...]
    lane = lax.broadcasted_iota(jnp.int32, logits.shape, 1)
    vals, idxs = [], []
    for _ in range(TOP_K):
        m = jnp.max(logits, axis=-1, keepdims=True)
        ix = jnp.min(jnp.where(logits == m, lane, LANES), axis=-1, keepdims=True)
        vals.append(m)
        idxs.append(ix)
        logits = jnp.where(lane == ix, -jnp.inf, logits)
    es = [jnp.exp(v - vals[0]) for v in vals]
    tot = es[0]
    for e in es[1:]:
        tot = tot + e
    wout = jnp.zeros(logits.shape, F32)
    iout = jnp.zeros(logits.shape, jnp.int32)
    for kk in range(TOP_K):
        wout = jnp.where(lane == kk, es[kk] / tot, wout)
        iout = jnp.where(lane == kk, idxs[kk], iout)
    w_ref[...] = wout
    idx_ref[...] = iout


def _post(x1, o, w_mem_o, norm_ffn, w_router, b_router):
    n, d = x1.shape
    mw = o[0].shape[1]
    n_exp = w_router.shape[1]
    assert n_exp <= LANES
    tm = TOKEN_TILE
    assert o[1].shape[0] == tm and o[0].shape[0] + tm == n
    n_ptiles = o[0].shape[0] // tm
    wr = jnp.pad(w_router, ((0, 0), (0, LANES - n_exp)))
    wrh = wr.astype(BF16)
    wrl = (wr - wrh.astype(F32)).astype(BF16)
    br = jnp.concatenate([b_router.astype(F32), jnp.full((LANES - n_exp,), NEG, F32)]).reshape(1, LANES)
    row = lambda i: (i, 0)
    fix = lambda i: (0, 0)
    return pl.pallas_call(
        functools.partial(_post_kernel, n_ptiles=n_ptiles),
        grid=(n // tm,),
        in_specs=[pl.BlockSpec((tm, d), row),
                  pl.BlockSpec((tm, mw), lambda i: (jnp.minimum(i, n_ptiles - 1), 0)),
                  pl.BlockSpec((tm, mw), fix), pl.BlockSpec((mw, d), fix),
                  pl.BlockSpec((1, d), fix), pl.BlockSpec((d, LANES), fix), pl.BlockSpec((d, LANES), fix),
                  pl.BlockSpec((1, LANES), fix)],
        out_specs=[pl.BlockSpec((tm, d), row), pl.BlockSpec((tm, d), row),
                   pl.BlockSpec((tm, LANES), row), pl.BlockSpec((tm, LANES), row)],
        out_shape=[jax.ShapeDtypeStruct((n, d), F32), jax.ShapeDtypeStruct((n, d), F32),
                   jax.ShapeDtypeStruct((n, LANES), jnp.int32), jax.ShapeDtypeStruct((n, LANES), F32)],
        compiler_params=_cparams(("parallel",)),
        name="post",
    )(x1, *o, w_mem_o.astype(BF16), norm_ffn.reshape(1, d), wrh, wrl, br)


GU_CHUNK = 2 * LANES


def _regroup_kernel(w_ref, p_ref, o_ref):
    for c in range(w_ref.shape[-1] // GU_CHUNK):
        sl = slice(c * GU_CHUNK, (c + 1) * GU_CHUNK)
        o_ref[0, :, sl] = _dot(w_ref[0, :, sl].astype(BF16), p_ref[...]).astype(BF16)


def _regroup_gate_up(w_gate_up):
    n_exp, d, ff2 = w_gate_up.shape
    wblk = 2 * GU_CHUNK
    assert ff2 % wblk == 0
    src = np.arange(GU_CHUNK)
    dst = np.where(src % 2 == 0, src // 2, LANES + src // 2)
    perm = np.zeros((GU_CHUNK, GU_CHUNK), np.float32)
    perm[src, dst] = 1.0
    blk = pl.BlockSpec((1, d, wblk), lambda e, j: (e, 0, j))
    return pl.pallas_call(
        _regroup_kernel,
        grid=(n_exp, ff2 // wblk),
        in_specs=[blk, pl.BlockSpec((GU_CHUNK, GU_CHUNK), lambda e, j: (0, 0))],
        out_specs=blk,
        out_shape=jax.ShapeDtypeStruct((n_exp, d, ff2), BF16),
        compiler_params=_cparams(("parallel", "parallel")),
        name="regroup_gate_up",
    )(w_gate_up, jnp.asarray(perm, BF16))


def _rank_kernel(idx_ref, rank_ref, cnt_ref, carry_ref):
    @pl.when(pl.program_id(0) == 0)
    def _():
        carry_ref[...] = jnp.zeros(carry_ref.shape, F32)

    idx = idx_ref[...]
    tm = idx.shape[0]
    lane = lax.broadcasted_iota(jnp.int32, idx.shape, 1)
    hot = [(lane == idx[:, kk:kk + 1]).astype(F32) for kk in range(TOP_K)]
    tot = hot[0]
    for h in hot[1:]:
        tot = tot + h
    earlier = (lax.broadcasted_iota(jnp.int32, (tm, tm), 1)
               < lax.broadcasted_iota(jnp.int32, (tm, tm), 0)).astype(BF16)
    base = carry_ref[...] + _dot(earlier, tot.astype(BF16))
    out = jnp.zeros(idx.shape, jnp.int32)
    for kk in range(TOP_K):
        r = jnp.sum(hot[kk] * base, axis=-1, keepdims=True)
        out = jnp.where(lane == kk, r.astype(jnp.int32), out)
        base = base + hot[kk]
    rank_ref[...] = out
    carry_ref[...] = carry_ref[...] + jnp.sum(tot, axis=0, keepdims=True)
    cnt_ref[...] = carry_ref[...]


def _rank(top_idx):
    n, w = top_idx.shape
    tm = TOKEN_TILE
    return pl.pallas_call(
        _rank_kernel,
        grid=(n // tm,),
        in_specs=[pl.BlockSpec((tm, w), lambda i: (i, 0))],
        out_specs=[pl.BlockSpec((tm, w), lambda i: (i, 0)), pl.BlockSpec((1, w), lambda i: (0, 0))],
        out_shape=[jax.ShapeDtypeStruct((n, w), jnp.int32), jax.ShapeDtypeStruct((1, w), F32)],
        scratch_shapes=[pltpu.VMEM((1, w), F32)],
        compiler_params=_cparams(("arbitrary",)),
        name="rank",
    )(top_idx)


def _dispatch_kernel(seg_start_ref, seg_len_ref, dest_ref, hf_ref, xb_out, zrow, sem, zsem):
    tc = hf_ref.shape[0]

    @pl.when(pl.program_id(0) == 0)
    def _():
        zrow[...] = jnp.zeros(zrow.shape, zrow.dtype)

        def zero_copy(row):
            return pltpu.make_async_copy(zrow.at[pl.ds(0, 1), :], xb_out.at[pl.ds(row, 1), :], zsem)

        def per_segment(op):
            def seg(s, carry):
                base = seg_start_ref[s]

                def row(r, c):
                    op(zero_copy(base + r))
                    return c

                return lax.fori_loop(0, seg_len_ref[s], row, carry)

            lax.fori_loop(0, seg_start_ref.shape[0], seg, 0)

        per_segment(lambda cp: cp.start())
        per_segment(lambda cp: cp.wait())

    for r in range(TOP_K * tc):
        pltpu.make_async_copy(hf_ref.at[pl.ds(r % tc, 1), :],
                              xb_out.at[pl.ds(dest_ref[0, 0, r], 1), :], sem).start(priority=r % 2)
    for _ in range(TOP_K):
        pltpu.make_async_copy(hf_ref, xb_out.at[pl.ds(0, tc), :], sem).wait()


def _experts_kernel(be_ref, nact_ref, x_ref, wgu_ref, bgu_ref, wd_ref, bd_ref, y_ref):
    i = pl.program_id(0)

    @pl.when(i < nact_ref[0])
    def _():
        gu = _dot(x_ref[...].astype(BF16), wgu_ref[0]) + bgu_ref[0]
        acts = []
        for c in range(gu.shape[1] // GU_CHUNK):
            g = jnp.minimum(gu[:, c * GU_CHUNK:c * GU_CHUNK + LANES], SWIGLU_LIMIT)
            u = jnp.clip(gu[:, c * GU_CHUNK + LANES:(c + 1) * GU_CHUNK], -SWIGLU_LIMIT, SWIGLU_LIMIT)
            acts.append(((u + 1.0) * (g * jax.nn.sigmoid(SWIGLU_ALPHA * g))).astype(BF16))
        y_ref[...] = _dot(jnp.concatenate(acts, axis=1), wd_ref[0]) + bd_ref[0]

    @pl.when(i >= nact_ref[0])
    def _():
        y_ref[...] = jnp.zeros(y_ref.shape, y_ref.dtype)


def _combine_kernel(dest_ref, x2_ref, w_ref, y_hbm, op_ref, os_ref, buf, sem, *, n_ptiles):
    tc = x2_ref.shape[0]

    for r in range(TOP_K * tc):
        pltpu.make_async_copy(y_hbm.at[pl.ds(dest_ref[0, 0, r], 1), :], buf.at[pl.ds(r, 1), :],
                              sem).start(priority=r % 2)
    pltpu.make_async_copy(y_hbm.at[pl.ds(0, TOP_K * tc), :], buf, sem).wait()
    acc = x2_ref[...]
    w = w_ref[...]
    for kk in range(TOP_K):
        acc = acc + w[:, kk:kk + 1] * buf[kk * tc:(kk + 1) * tc, :]

    @pl.when(pl.program_id(0) < n_ptiles)
    def _():
        op_ref[...] = acc

    @pl.when(pl.program_id(0) >= n_ptiles)
    def _():
        os_ref[...] = acc


def _moe(hf, x2, top_idx, top_w, w_gate_up, b_gate_up, w_down, b_down, n_prompt):
    n_tok, d = hf.shape
    n_exp, _, ff2 = w_gate_up.shape
    ff = ff2 // 2
    tm, tc = MOE_TILE, COMBINE_TILE
    assert n_tok % tc == 0
    n_assign = n_tok * TOP_K
    n_blocks = -(-(n_assign + n_exp * (tm - 1)) // tm)
    n_rows = n_blocks * tm

    rank, cnt = _rank(top_idx)
    counts = cnt[0, :n_exp].astype(jnp.int32)
    padded = (counts + tm - 1) // tm * tm
    pend = jnp.cumsum(padded)
    pstart = pend - padded
    choice = top_idx[:, :TOP_K]
    first = jnp.sum(jnp.where(choice[:, :, None] == jnp.arange(n_exp, dtype=jnp.int32), pstart, 0), axis=-1)
    dest = (first + rank[:, :TOP_K]).astype(jnp.int32)
    dest_tiles = dest.reshape(n_tok // tc, tc, TOP_K).transpose(0, 2, 1).reshape(n_tok // tc, 1, TOP_K * tc)
    block_e = jnp.minimum(jnp.sum(jnp.arange(n_blocks, dtype=jnp.int32)[None, :] * tm >= pend[:, None], axis=0),
                          n_exp - 1).astype(jnp.int32)
    n_active = (pend[-1:] // tm).astype(jnp.int32)
    seg_start = jnp.concatenate([pstart + counts, pend[-1:]]).astype(jnp.int32)
    seg_len = jnp.concatenate([padded - counts, n_rows - pend[-1:]]).astype(jnp.int32)

    xb = pl.pallas_call(
        _dispatch_kernel,
        grid_spec=pltpu.PrefetchScalarGridSpec(
            num_scalar_prefetch=2,
            grid=(n_tok // tc,),
            in_specs=[pl.BlockSpec((1, 1, TOP_K * tc), lambda i, ss, sl: (i, 0, 0), memory_space=pltpu.SMEM),
                      pl.BlockSpec((tc, d), lambda i, ss, sl: (i, 0))],
            out_specs=pl.BlockSpec(memory_space=pl.ANY),
            scratch_shapes=[pltpu.VMEM((SUBLANES, d), F32), pltpu.SemaphoreType.DMA(()),
                            pltpu.SemaphoreType.DMA(())]),
        out_shape=jax.ShapeDtypeStruct((n_rows, d), F32),
        compiler_params=_cparams(("arbitrary",)),
        name="dispatch",
    )(seg_start, seg_len, dest_tiles, hf)

    wgu = _regroup_gate_up(w_gate_up)
    bgu = b_gate_up.reshape(n_exp, ff2 // GU_CHUNK, LANES, 2).transpose(0, 1, 3, 2).reshape(n_exp, 1, ff2)
    by_e = lambda i, be, na: (be[i], 0, 0)
    yb = pl.pallas_call(
        _experts_kernel,
        grid_spec=pltpu.PrefetchScalarGridSpec(
            num_scalar_prefetch=2,
            grid=(n_blocks,),
            in_specs=[pl.BlockSpec((tm, d), lambda i, be, na: (jnp.minimum(i, na[0] - 1), 0)),
                      pl.BlockSpec((1, d, ff2), by_e), pl.BlockSpec((1, 1, ff2), by_e),
                      pl.BlockSpec((1, ff, d), by_e), pl.BlockSpec((1, 1, d), by_e)],
            out_specs=pl.BlockSpec((tm, d), lambda i, be, na: (i, 0))),
        out_shape=jax.ShapeDtypeStruct((n_rows, d), F32),
        compiler_params=_cparams(("arbitrary",)),
        name="experts",
    )(block_e, n_active, xb, wgu, bgu, w_down.astype(BF16), b_down.reshape(n_exp, 1, d))

    assert n_prompt % tc == 0 and 0 < n_prompt < n_tok
    n_ptiles = n_prompt // tc
    return pl.pallas_call(
        functools.partial(_combine_kernel, n_ptiles=n_ptiles),
        grid=(n_tok // tc,),
        in_specs=[pl.BlockSpec((1, 1, TOP_K * tc), lambda i: (i, 0, 0), memory_space=pltpu.SMEM),
                  pl.BlockSpec((tc, d), lambda i: (i, 0)),
                  pl.BlockSpec((tc, LANES), lambda i: (i, 0)),
                  pl.BlockSpec(memory_space=pl.ANY)],
        out_specs=[pl.BlockSpec((tc, d), lambda i: (jnp.minimum(i, n_ptiles - 1), 0)),
                   pl.BlockSpec((tc, d), lambda i: (jnp.maximum(i - n_ptiles, 0), 0))],
        out_shape=[jax.ShapeDtypeStruct((n_prompt, d), F32),
                   jax.ShapeDtypeStruct((n_tok - n_prompt, d), F32)],
        scratch_shapes=[pltpu.VMEM((TOP_K * tc, d), F32), pltpu.SemaphoreType.DMA(())],
        compiler_params=_cparams(("arbitrary",)),
        name="combine",
    )(dest_tiles, x2, top_w, yb)


def kernel(x_prompt, x_sample, cache_win_k, cache_win_v, state_conv, state_ssm, cache_mem_k, cache_mem_v,
           mem_prompt, norm_mix, w_in, q_gain, k_gain, conv_w, conv_b, dt_bias, a_log, d_skip, ssm_out_gain,
           w_out, norm_mem, mem_in_gain, w_mem_q, w_mem_k, w_mem_v, mem_q_gain, mem_k_gain, w_mem_o,
           norm_ffn, w_router, b_router, w_gate_up, b_gate_up, w_down, b_down):
    bp, lp, d = x_prompt.shape
    bs, ls, _ = x_sample.shape
    depth = norm_mix.shape[0]
    n_buf = cache_win_k.shape[2]
    past_len = PAST_LEN
    n_mem = mem_prompt.shape[1]
    npr, nsm = bp * lp, bs * ls
    att_w = d // 2
    n_heads = att_w // ATT_HEAD_DIM
    ssm_w = d - att_w
    conv_ch = ssm_w + 2 * SSM_GROUPS * SSM_STATE
    tail = CONV_WIDTH - 1
    keep = min(max(w for w, _ in DILATIONS), lp)
    ls_pad = SUBLANES

    assert keep == lp
    xp_out, xs_out = x_prompt.reshape(npr, d).astype(F32), x_sample.reshape(nsm, d).astype(F32)
    outs = [[] for _ in range(10)]
    for i in range(depth):
        x = (xp_out, xs_out)
        q, k, v, z, xbc, dt_raw, kt_p, vt_p = _projections(*x, norm_mix[i], w_in[i], q_gain[i], k_gain[i],
                                                           lp, ls, past_len)
        smp = lambda a: a[npr:].reshape(bs, ls, a.shape[-1])
        pad_s = lambda a: jnp.pad(smp(a), ((0, 0), (0, ls_pad - ls), (0, 0)))
        pad_rows = lambda a: pad_s(a).reshape(bs * ls_pad, a.shape[-1])
        unpad = lambda a: a.reshape(bs, ls_pad, a.shape[-1])[:, :ls].reshape(nsm, a.shape[-1])
        att_p = _attn_prompt(q, k, v, bp, lp)
        heads = lambda a, b, l: a.reshape(b, l, n_heads, ATT_HEAD_DIM)
        k_new, v_new = heads(k[npr:], bs, ls), heads(v[npr:], bs, ls)
        cache_t = lambda c: jnp.transpose(c, (0, 2, 3, 1)).reshape(bs, att_w, n_buf)
        att_s = _attn_sample(smp(q), smp(k), smp(v), cache_t(cache_win_k[i]), cache_t(cache_win_v[i]))
        ssm_par = (conv_w[i], conv_b[i], dt_bias[i], a_log[i], d_skip[i], ssm_out_gain[i])
        y_p, st_p, cs_p = _ssd(xbc, dt_raw, z, jnp.zeros((bp, tail, conv_ch), F32),
                               jnp.zeros((bp, ssm_w // SSM_HEAD_DIM, SSM_HEAD_DIM, SSM_STATE), F32), *ssm_par,
                               bp, lp, lp)
        y_s, st_s, cs_s = _ssd(pad_rows(xbc), pad_rows(dt_raw), pad_rows(z), state_conv[i], state_ssm[i],
                               *ssm_par, bs, ls_pad, ls)
        x1, qm = _out_proj(x, (att_p, att_s.reshape(nsm, att_w)), (y_p, unpad(y_s)),
                           w_out[i], norm_mem[i], w_mem_q[i], mem_q_gain[i])
        mk_p, mv_p = _mem_kv(mem_prompt.reshape(bp * n_mem, d).astype(F32), mem_in_gain[i], w_mem_k[i],
                             w_mem_v[i], mem_k_gain[i])
        mw = mk_p.shape[-1]
        o_p = _mem_attn(qm, mk_p.reshape(bp, n_mem, mw), mv_p.reshape(bp, n_mem, mw), bp, lp, TOKEN_TILE)
        o_s = _mem_attn(pad_rows(qm), cache_mem_k[i], cache_mem_v[i], bs, ls_pad, ls_pad)
        x2, hf, top_idx, top_w = _post(x1, (o_p, unpad(o_s)), w_mem_o[i], norm_ffn[i], w_router[i], b_router[i])
        xp_out, xs_out = _moe(hf, x2, top_idx, top_w, w_gate_up[i], b_gate_up[i], w_down[i], b_down[i], npr)

        untr = lambda a: jnp.transpose(a.reshape(bp, n_heads, ATT_HEAD_DIM, lp), (0, 3, 1, 2))
        new = (untr(kt_p), untr(vt_p),
               k_new, v_new,
               cs_p, cs_s,
               st_p, st_s,
               mk_p.reshape(bp, n_mem, mw // MEM_HEAD_DIM, MEM_HEAD_DIM),
               mv_p.reshape(bp, n_mem, mw // MEM_HEAD_DIM, MEM_HEAD_DIM))
        for lst, val in zip(outs, new):
            lst.append(val)
    y_p = xp_out.reshape(bp, lp, d).astype(x_prompt.dtype)
    y_s = xs_out.reshape(bs, ls, d).astype(x_sample.dtype)
    return (y_p, y_s) + tuple(jnp.stack(o) for o in outs)
```

```python
import functools
import math

import numpy as np
import jax
import jax.numpy as jnp
from jax import lax
from jax.experimental import pallas as pl
from jax.experimental.pallas import tpu as pltpu

F32 = jnp.float32
BF16 = jnp.bfloat16

ATT_HEAD_DIM = 64
DILATIONS = ((128, 1), (512, 4), (2048, 16))
ATT_BLOCK = 128
ROPE_DIM = ATT_HEAD_DIM // 4
ROPE_THETA = 500000.0
PAST_LEN = 8192
SSM_HEAD_DIM = 64
SSM_GROUPS = 2
SSM_STATE = 128
CONV_WIDTH = 4
SSM_CHUNK = 128
MEM_HEAD_DIM = 128
TOP_K = 4
SWIGLU_LIMIT = 7.0
SWIGLU_ALPHA = 1.702
NORM_EPS = 1e-6

LANES = 128
SUBLANES = 8
VMEM_LIMIT = 56 * 1024 * 1024

TOKEN_TILE = 512
MOE_TILE = 256
COMBINE_TILE = 256
ATTN_UNROLL = 8
LOG2E = math.log2(math.e)
NEG = -1e30


def _cparams(sem):
    return pltpu.CompilerParams(dimension_semantics=sem, vmem_limit_bytes=VMEM_LIMIT)


def _rms(x, gain):
    return x * lax.rsqrt(jnp.mean(x * x, axis=-1, keepdims=True) + NORM_EPS) * gain


def _dot(a, b):
    return jnp.dot(a, b, preferred_element_type=F32)


def _dot_nt(a, b):
    return lax.dot_general(a, b, (((1,), (1,)), ((), ())), preferred_element_type=F32)


def _dot_tn(a, b):
    return lax.dot_general(a, b, (((0,), (0,)), ((), ())), preferred_element_type=F32)


def _dot_f32(a, b):
    return jnp.dot(a, b, preferred_element_type=F32, precision=lax.Precision.HIGHEST)


def _two_source(i, n_first, first_ref, second_ref):
    return lax.cond(i < n_first, lambda: first_ref[...], lambda: second_ref[...])


def _proj_kernel(xp_ref, xs_ref, g_ref, w_ref, qg_ref, kg_ref, seg_ref, cos_ref, s1_ref, s2_ref,
                 q_ref, k_ref, v_ref, z_ref, xbc_ref, dt_ref, kt_ref, vt_ref, *,
                 att_w, ssm_w, conv_ch, n_ptiles):
    i = pl.program_id(0)
    h = _rms(_two_source(i, n_ptiles, xp_ref, xs_ref), g_ref[...]).astype(BF16)
    seg = seg_ref[...]
    cos, s1, s2 = cos_ref[...], s1_ref[...], s2_ref[...]

    def head_norm_rope(t, gain):
        sq = t * t
        hi = sq.astype(BF16)
        lo = (sq - hi.astype(F32)).astype(BF16)
        ms = (_dot(hi, seg) + _dot(lo, seg)) * (1.0 / ATT_HEAD_DIM)
        tn = t * lax.rsqrt(ms + NORM_EPS) * gain
        half = ROPE_DIM // 2
        return (tn * cos + pltpu.roll(tn, half, 1) * s1
                + pltpu.roll(tn, att_w - half, 1) * s2)

    q = head_norm_rope(_dot(h, w_ref[:, 0:att_w]), qg_ref[...])
    q_ref[...] = q * (ATT_HEAD_DIM ** -0.5 * LOG2E)
    k = head_norm_rope(_dot(h, w_ref[:, att_w:2 * att_w]), kg_ref[...])
    v = _dot(h, w_ref[:, 2 * att_w:3 * att_w])
    k_ref[...] = k
    v_ref[...] = v

    @pl.when(i < n_ptiles)
    def _():
        kt_ref[0] = k.T
        vt_ref[0] = v.T

    o = 3 * att_w
    z_ref[...] = _dot(h, w_ref[:, o:o + ssm_w])
    o += ssm_w
    xbc_ref[...] = _dot(h, w_ref[:, o:o + conv_ch])
    o += conv_ch
    dt_ref[...] = _dot(h, w_ref[:, o:o + LANES])


def _rope_tables(pos, n_heads):
    half = ROPE_DIM // 2
    inv_freq = jnp.power(ROPE_THETA, -jnp.arange(half, dtype=F32) / half)
    ang = pos.astype(F32)[:, None] * inv_freq[None, :]
    cos, sin = jnp.cos(ang), jnp.sin(ang)
    n = pos.shape[0]
    rest = ATT_HEAD_DIM - ROPE_DIM
    c = jnp.concatenate([cos, cos, jnp.ones((n, rest), F32)], axis=-1)
    s1 = jnp.concatenate([jnp.zeros((n, half), F32), sin, jnp.zeros((n, rest), F32)], axis=-1)
    s2 = jnp.concatenate([-sin, jnp.zeros((n, half + rest), F32)], axis=-1)
    return tuple(jnp.tile(t, (1, n_heads)) for t in (c, s1, s2))


def _projections(x_p, x_s, norm_mix, w_in, q_gain, k_gain, seq, dec_seq, past_len):
    n_prompt, d = x_p.shape
    n = n_prompt + x_s.shape[0]
    d_half = d // 2
    att_w, ssm_w = d_half, d - d_half
    n_heads = att_w // ATT_HEAD_DIM
    ssm_heads = ssm_w // SSM_HEAD_DIM
    conv_ch = ssm_w + 2 * SSM_GROUPS * SSM_STATE
    tm = TOKEN_TILE
    assert n % tm == 0 and n_prompt % tm == 0 and seq % tm == 0 and (n - n_prompt) == tm
    assert tm % dec_seq == 0
    c0 = 3 * att_w + ssm_w
    w = jnp.concatenate([w_in[:, :c0], w_in[:, c0 + ssm_heads:],
                         w_in[:, c0:c0 + ssm_heads],
                         jnp.zeros((d, LANES - ssm_heads), w_in.dtype)], axis=1).astype(BF16)
    wn = w.shape[1]
    pos = jnp.concatenate([jnp.arange(seq, dtype=jnp.int32),
                           past_len + jnp.arange(tm, dtype=jnp.int32) % dec_seq])
    cos, s1, s2 = _rope_tables(pos, n_heads)
    tiles_per_seq = seq // tm
    n_prompt_tiles = n_prompt // tm
    head_id = np.arange(att_w) // ATT_HEAD_DIM
    seg = jnp.asarray(head_id[:, None] == head_id[None, :], BF16)

    def tab_map(i):
        return (jnp.where(i < n_prompt_tiles, i % tiles_per_seq, tiles_per_seq), 0)

    row = lambda i: (i, 0)
    fix = lambda i: (0, 0)
    first = lambda i: (jnp.minimum(i, n_prompt_tiles - 1), 0)

    def t_map(i):
        j = jnp.minimum(i, n_prompt_tiles - 1)
        return (j // tiles_per_seq, 0, j % tiles_per_seq)

    tab = pl.BlockSpec((tm, att_w), tab_map)
    tr = pl.BlockSpec((1, att_w, tm), t_map)
    kern = functools.partial(_proj_kernel, att_w=att_w, ssm_w=ssm_w, conv_ch=conv_ch, n_ptiles=n_prompt_tiles)
    return pl.pallas_call(
        kern,
        grid=(n // tm,),
        in_specs=[pl.BlockSpec((tm, d), first), pl.BlockSpec((tm, d), fix), pl.BlockSpec((1, d), fix),
                  pl.BlockSpec((d, wn), fix), pl.BlockSpec((1, att_w), fix),
                  pl.BlockSpec((1, att_w), fix), pl.BlockSpec((att_w, att_w), fix),
                  tab, tab, tab],
        out_specs=[pl.BlockSpec((tm, att_w), row), pl.BlockSpec((tm, att_w), row),
                   pl.BlockSpec((tm, att_w), row), pl.BlockSpec((tm, ssm_w), row),
                   pl.BlockSpec((tm, conv_ch), row), pl.BlockSpec((tm, LANES), row), tr, tr],
        out_shape=[jax.ShapeDtypeStruct((n, att_w), F32), jax.ShapeDtypeStruct((n, att_w), F32),
                   jax.ShapeDtypeStruct((n, att_w), F32), jax.ShapeDtypeStruct((n, ssm_w), F32),
                   jax.ShapeDtypeStruct((n, conv_ch), F32), jax.ShapeDtypeStruct((n, LANES), F32),
                   jax.ShapeDtypeStruct((n_prompt // seq, att_w, seq), F32),
                   jax.ShapeDtypeStruct((n_prompt // seq, att_w, seq), F32)],
        compiler_params=_cparams(("arbitrary",)),
        name="proj",
    )(x_p, x_s, norm_mix.reshape(1, d), w, jnp.tile(q_gain, n_heads).reshape(1, att_w),
      jnp.tile(k_gain, n_heads).reshape(1, att_w), seg, cos, s1, s2)


def _attn_prompt_kernel(q_ref, k_ref, v_ref, o_ref, num_ref, m_ref, den_ref, *, seq):
    blk = ATT_BLOCK
    lane = lax.broadcasted_iota(jnp.int32, (blk, LANES), 1)
    head0 = lane < ATT_HEAD_DIM
    qi = lax.broadcasted_iota(jnp.int32, (blk, 2 * blk), 0) + blk
    ki = lax.broadcasted_iota(jnp.int32, (blk, 2 * blk), 1)
    dist = qi - ki
    band = (dist >= 0) & (dist <= blk)
    bias_rest = jnp.where(band, 0.0, NEG)
    bias_first = jnp.where(band & (ki >= blk), 0.0, NEG)

    for di, (window, dil) in enumerate(DILATIONS):
        assert window // dil == blk
        nb = seq // dil // blk

        def body(i, carry, dil=dil, nb=nb, di=di):
            r = i // nb
            j = i % nb
            if dil > 1:
                start = r + dil * blk * j
                prev = r + dil * blk * jnp.maximum(j - 1, 0)
                rows = pl.ds(start, blk, stride=dil)
                prows = pl.ds(prev, blk, stride=dil)
            else:
                rows = pl.ds(pl.multiple_of(blk * j, blk), blk)
                prows = pl.ds(pl.multiple_of(blk * jnp.maximum(j - 1, 0), blk), blk)
            qb = q_ref[rows, :].astype(BF16)
            k2 = jnp.concatenate([k_ref[prows, :], k_ref[rows, :]], axis=0).astype(BF16)
            v2 = jnp.concatenate([v_ref[prows, :], v_ref[rows, :]], axis=0).astype(BF16)
            bias = jnp.where(j > 0, bias_rest, bias_first)
            zero = jnp.zeros_like(qb)
            q2 = jnp.concatenate([jnp.where(head0, qb, zero), jnp.where(head0, zero, qb)], axis=0)
            s = _dot_nt(q2, k2) + jnp.concatenate([bias, bias], axis=0)
            m = jnp.max(s, axis=-1, keepdims=True)
            p = jnp.exp2(s - m)
            den = jnp.sum(p, axis=-1, keepdims=True)
            pv = _dot(p.astype(BF16), v2)
            num_ref[di, rows, :] = jnp.where(head0, pv[:blk], pv[blk:])
            m_ref[di, rows, :] = jnp.where(head0, m[:blk], m[blk:])
            den_ref[di, rows, :] = jnp.where(head0, den[:blk], den[blk:])
            return carry

        lax.fori_loop(0, dil * nb, body, 0, unroll=ATTN_UNROLL)

    m_all = jnp.maximum(jnp.maximum(m_ref[0], m_ref[1]), m_ref[2])
    num = jnp.zeros((seq, LANES), F32)
    den = jnp.zeros((seq, LANES), F32)
    for di in range(len(DILATIONS)):
        a = jnp.exp2(m_ref[di] - m_all)
        num = num + a * num_ref[di]
        den = den + a * den_ref[di]
    o_ref[...] = (num / den).astype(o_ref.dtype)


def _attn_prompt(q, k, v, batch, seq):
    att_w = q.shape[1]
    pairs = att_w // LANES
    nd = len(DILATIONS)
    blk = pl.BlockSpec((seq, LANES), lambda b, h: (b, h))
    return pl.pallas_call(
        functools.partial(_attn_prompt_kernel, seq=seq),
        grid=(batch, pairs),
        in_specs=[blk, blk, blk],
        out_specs=blk,
        out_shape=jax.ShapeDtypeStruct((batch * seq, att_w), BF16),
        scratch_shapes=[pltpu.VMEM((nd, seq, LANES), F32), pltpu.VMEM((nd, seq, LANES), F32),
                        pltpu.VMEM((nd, seq, LANES), F32)],
        compiler_params=_cparams(("parallel", "parallel")),
        name="attn_prompt",
    )(q, k, v)


def _attn_sample_kernel(q_ref, kn_ref, vn_ref, kc_ref, vc_ref, cc_ref, cn_ref, o_ref, *,
                        dec_seq, n_heads):
    w = q_ref.shape[-1]
    rows = dec_seq * n_heads
    q = q_ref[0]
    qm = jnp.concatenate([jnp.broadcast_to(q[t:t + 1], (n_heads, w)) for t in range(dec_seq)], axis=0)
    lane_head = lax.broadcasted_iota(jnp.int32, (rows, w), 1) // ATT_HEAD_DIM
    row_head = lax.broadcasted_iota(jnp.int32, (rows, w), 0) % n_heads
    own = lane_head == row_head
    qm = jnp.where(own, qm, 0.0).astype(BF16)
    zpad = jnp.zeros((SUBLANES - dec_seq, w), F32)
    kn = jnp.concatenate([kn_ref[0], zpad], axis=0).astype(BF16)
    vn = jnp.concatenate([vn_ref[0], zpad], axis=0).astype(BF16)
    cc, cn = cc_ref[...], cn_ref[...]
    s_c = jnp.where(cc > 0, _dot(qm, kc_ref[0].astype(BF16)), -jnp.inf)
    s_n = jnp.where(cn > 0, _dot_nt(qm, kn), -jnp.inf)
    m = jnp.maximum(jnp.max(s_c, axis=-1, keepdims=True), jnp.max(s_n, axis=-1, keepdims=True))
    p_c = cc * jnp.exp2(s_c - m)
    p_n = cn * jnp.exp2(s_n - m)
    den = jnp.sum(p_c, axis=-1, keepdims=True) + jnp.sum(p_n, axis=-1, keepdims=True)
    o = _dot_nt(p_c.astype(BF16), vc_ref[0].astype(BF16)) + _dot(p_n.astype(BF16), vn)
    o = jnp.where(own, o / den, 0.0)
    o_ref[0] = jnp.sum(o.reshape(dec_seq, n_heads, w), axis=1).astype(o_ref.dtype)


def _attn_sample(q, k_new, v_new, k_cache_t, v_cache_t):
    b, t, w = q.shape
    n_buf = k_cache_t.shape[2]
    n_heads = w // ATT_HEAD_DIM
    assert n_heads == SUBLANES and t <= SUBLANES
    assert n_buf >= max(win for win, _ in DILATIONS)

    def count(dist):
        return sum(((dist >= 0) & (dist % dil == 0) & (dist <= win)).astype(np.float32)
                   for win, dil in DILATIONS)

    tq = np.repeat(np.arange(t), n_heads)[:, None]
    cc = count(n_buf + tq - np.arange(n_buf)[None, :])
    jn = np.arange(SUBLANES)[None, :]
    cn = np.where(jn < t, count(tq - jn), 0.0).astype(np.float32)
    rows = t * n_heads
    new = pl.BlockSpec((1, t, w), lambda i: (i, 0, 0))
    cache = pl.BlockSpec((1, w, n_buf), lambda i: (i, 0, 0))
    return pl.pallas_call(
        functools.partial(_attn_sample_kernel, dec_seq=t, n_heads=n_heads),
        grid=(b,),
        in_specs=[new, new, new, cache, cache,
                  pl.BlockSpec((rows, n_buf), lambda i: (0, 0)),
                  pl.BlockSpec((rows, SUBLANES), lambda i: (0, 0))],
        out_specs=new,
        out_shape=jax.ShapeDtypeStruct((b, t, w), BF16),
        compiler_params=_cparams(("parallel",)),
        name="attn_sample",
    )(q, k_new, v_new, k_cache_t, v_cache_t, jnp.asarray(cc), jnp.asarray(cn))


def _softplus(x):
    return jnp.maximum(x, 0.0) + jnp.log1p(jnp.exp(-jnp.abs(x)))


def _silu(x):
    return x * jax.nn.sigmoid(x)


def _ssd_kernel(xbc_ref, dt_ref, dtt_ref, z_ref, cp_ref, h0_ref, cw_ref, cb_ref, dtb_ref, dtbt_ref,
                al_ref, alt_ref, dsk_ref, og_ref, y_ref, hl_ref, ct_ref, xpad_ref, h_ref, *,
                q, valid, ssm_w, n_pairs):
    c = pl.program_id(1)
    tail = CONV_WIDTH - 1

    @pl.when(c == 0)
    def _():
        h_ref[...] = h0_ref[0]
        xpad_ref[0:SUBLANES, :] = jnp.zeros((SUBLANES, xpad_ref.shape[1]), F32)
        xpad_ref[SUBLANES - tail:SUBLANES, :] = cp_ref[0]

    x = xbc_ref[...]
    xpad_ref[SUBLANES:SUBLANES + q, :] = x
    conv = cb_ref[...]
    for j in range(CONV_WIDTH):
        o = SUBLANES - tail + j
        conv = conv + xpad_ref[o:o + q, :] * cw_ref[j:j + 1, :]

    @pl.when(c == pl.num_programs(1) - 1)
    def _():
        ct_ref[0] = xpad_ref[valid:valid + SUBLANES, :]

    xpad_ref[0:SUBLANES, :] = x[q - SUBLANES:q, :]
    u = _silu(conv)
    xs = u[:, :ssm_w]
    gw = SSM_STATE
    bm = u[:, ssm_w:ssm_w + SSM_GROUPS * gw].astype(BF16)
    cm = u[:, ssm_w + SSM_GROUPS * gw:].astype(BF16)

    dt = _softplus(dt_ref[...] + dtb_ref[...])
    dtt = _softplus(dtt_ref[0] + dtbt_ref[...])
    if valid < q:
        dt = jnp.where(lax.broadcasted_iota(jnp.int32, dt.shape, 0) < valid, dt, 0.0)
        dtt = jnp.where(lax.broadcasted_iota(jnp.int32, dtt.shape, 1) < valid, dtt, 0.0)
    ri = lax.broadcasted_iota(jnp.int32, (q, q), 0)
    ci = lax.broadcasted_iota(jnp.int32, (q, q), 1)
    causal = ci <= ri
    acum = _dot_f32(causal.astype(F32), dt * -jnp.exp(al_ref[...]))
    acumt = _dot_f32(dtt * -jnp.exp(alt_ref[...]), (ri <= ci).astype(F32))

    left = lax.broadcasted_iota(jnp.int32, (q, LANES), 1) < SSM_HEAD_DIM
    top = lax.broadcasted_iota(jnp.int32, (LANES, LANES), 0) < SSM_HEAD_DIM
    rep = 2 * n_pairs // SSM_GROUPS
    gmat = [_dot_nt(cm[:, g * gw:(g + 1) * gw], bm[:, g * gw:(g + 1) * gw]) for g in range(SSM_GROUPS)]
    ys = []
    for pr in range(n_pairs):
        ha, hb = 2 * pr, 2 * pr + 1
        g = ha // rep
        bg = bm[:, g * gw:(g + 1) * gw]
        cg = cm[:, g * gw:(g + 1) * gw]
        xpair = xs[:, pr * LANES:(pr + 1) * LANES]
        xd = xpair * jnp.where(left, dt[:, ha:ha + 1], dt[:, hb:hb + 1])
        xdb = xd.astype(BF16)
        yd = []
        for hh in (ha, hb):
            seg = acum[:, hh:hh + 1] - acumt[hh:hh + 1, :]
            decay = jnp.exp(jnp.where(causal, seg, -jnp.inf))
            yd.append(_dot((gmat[g] * decay).astype(BF16), xdb))
        ac = jnp.where(left, acum[:, ha:ha + 1], acum[:, hb:hb + 1])
        hprev = h_ref[pr]
        y_off = _dot_nt(cg, hprev.astype(BF16)) * jnp.exp(ac)
        to_end = jnp.exp(ac[q - 1:q, :] - ac)
        upd = _dot_tn((xd * to_end).astype(BF16), bg)
        cdec = jnp.where(top, jnp.exp(acum[q - 1:q, ha:ha + 1]), jnp.exp(acum[q - 1:q, hb:hb + 1]))
        h_ref[pr] = hprev * cdec + upd
        ys.append(jnp.where(left, yd[0], yd[1]) + y_off
                  + xpair * dsk_ref[:, pr * LANES:(pr + 1) * LANES])
    y = jnp.concatenate(ys, axis=1) * _silu(z_ref[...])
    y_ref[...] = _rms(y, og_ref[...]).astype(y_ref.dtype)

    @pl.when(c == pl.num_programs(1) - 1)
    def _():
        hl_ref[0] = h_ref[...]


def _ssd(xbc, dt_raw, z, conv_prev, h0, conv_w, conv_b, dt_bias, a_log, d_skip, out_gain, b, l, valid_len):
    conv_ch = xbc.shape[-1]
    ssm_w = z.shape[-1]
    n_heads = ssm_w // SSM_HEAD_DIM
    n_pairs = n_heads // 2
    assert 2 * SSM_HEAD_DIM == LANES and SSM_STATE == LANES and n_heads <= SUBLANES
    assert (n_heads // SSM_GROUPS) % 2 == 0
    q = SSM_CHUNK if l % SSM_CHUNK == 0 else l
    assert l % q == 0 and q % SUBLANES == 0 and (valid_len == l or q == l)
    nc = l // q
    dtt = jnp.swapaxes(dt_raw[:b * l, :n_heads].reshape(b, l, n_heads), 1, 2)
    pad_h = lambda a: jnp.pad(a.reshape(1, n_heads), ((0, 0), (0, LANES - n_heads)))
    per_lane = lambda a: jnp.repeat(a, SSM_HEAD_DIM).reshape(1, ssm_w)
    h0p = h0.reshape(b, n_pairs, LANES, SSM_STATE).astype(F32)
    tail = CONV_WIDTH - 1
    tok = lambda w: pl.BlockSpec((q, w), lambda i, j: (i * nc + j, 0))
    fix2 = lambda r, w: pl.BlockSpec((r, w), lambda i, j: (0, 0))
    st = pl.BlockSpec((1, n_pairs, LANES, SSM_STATE), lambda i, j: (i, 0, 0, 0))
    kern = functools.partial(_ssd_kernel, q=q, valid=valid_len if q == l else q, ssm_w=ssm_w, n_pairs=n_pairs)
    y, h_last, conv_tail = pl.pallas_call(
        kern,
        grid=(b, nc),
        in_specs=[tok(conv_ch), tok(LANES), pl.BlockSpec((1, n_heads, q), lambda i, j: (i, 0, j)),
                  tok(ssm_w), pl.BlockSpec((1, tail, conv_ch), lambda i, j: (i, 0, 0)), st,
                  fix2(CONV_WIDTH, conv_ch), fix2(1, conv_ch), fix2(1, LANES), fix2(n_heads, 1),
                  fix2(1, LANES), fix2(n_heads, 1), fix2(1, ssm_w), fix2(1, ssm_w)],
        out_specs=[tok(ssm_w), st, pl.BlockSpec((1, SUBLANES, conv_ch), lambda i, j: (i, 0, 0))],
        out_shape=[jax.ShapeDtypeStruct((b * l, ssm_w), BF16),
                   jax.ShapeDtypeStruct((b, n_pairs, LANES, SSM_STATE), F32),
                   jax.ShapeDtypeStruct((b, SUBLANES, conv_ch), F32)],
        scratch_shapes=[pltpu.VMEM((q + SUBLANES, conv_ch), F32),
                        pltpu.VMEM((n_pairs, LANES, SSM_STATE), F32)],
        compiler_params=_cparams(("parallel", "arbitrary")),
        name="ssd",
    )(xbc, dt_raw, dtt, z, conv_prev.astype(F32), h0p, conv_w, conv_b.reshape(1, conv_ch),
      pad_h(dt_bias), dt_bias.reshape(n_heads, 1), pad_h(a_log), a_log.reshape(n_heads, 1),
      per_lane(d_skip), out_gain.reshape(1, ssm_w))
    return y, h_last.reshape(b, n_heads, SSM_HEAD_DIM, SSM_STATE), conv_tail[:, SUBLANES - tail:]


def _head_rms_store(dst_ref, t, gain, scale=None):
    for hd in range(t.shape[1] // MEM_HEAD_DIM):
        sl = slice(hd * MEM_HEAD_DIM, (hd + 1) * MEM_HEAD_DIM)
        r = _rms(t[:, sl], gain)
        if scale is not None:
            r = r * scale
        dst_ref[:, sl] = r.astype(dst_ref.dtype)


def _mem_kv_kernel(m_ref, g_ref, wk_ref, wv_ref, kg_ref, k_ref, v_ref):
    h = _rms(m_ref[...], g_ref[...]).astype(BF16)
    _head_rms_store(k_ref, _dot(h, wk_ref[...]), kg_ref[...])
    v_ref[...] = _dot(h, wv_ref[...])


def _mem_kv(mem, in_gain, w_k, w_v, k_gain):
    n, d = mem.shape
    mw = w_k.shape[1]
    tm = TOKEN_TILE
    assert n % tm == 0 and MEM_HEAD_DIM == LANES
    row = lambda i: (i, 0)
    fix = lambda i: (0, 0)
    return pl.pallas_call(
        _mem_kv_kernel,
        grid=(n // tm,),
        in_specs=[pl.BlockSpec((tm, d), row), pl.BlockSpec((1, d), fix), pl.BlockSpec((d, mw), fix),
                  pl.BlockSpec((d, mw), fix), pl.BlockSpec((1, MEM_HEAD_DIM), fix)],
        out_specs=[pl.BlockSpec((tm, mw), row), pl.BlockSpec((tm, mw), row)],
        out_shape=[jax.ShapeDtypeStruct((n, mw), F32), jax.ShapeDtypeStruct((n, mw), F32)],
        compiler_params=_cparams(("parallel",)),
        name="mem_kv",
    )(mem, in_gain.reshape(1, d), w_k.astype(BF16), w_v.astype(BF16), k_gain.reshape(1, MEM_HEAD_DIM))


def _out_proj_kernel(xp_ref, xs_ref, ap_ref, as_ref, yp_ref, ys_ref, wo_ref, g_ref, wq_ref, qg_ref,
                     x1_ref, q_ref, *, n_ptiles):
    i = pl.program_id(0)
    att_w = ap_ref.shape[1]
    x1 = (_two_source(i, n_ptiles, xp_ref, xs_ref)
          + _dot(_two_source(i, n_ptiles, ap_ref, as_ref), wo_ref[0:att_w, :])
          + _dot(_two_source(i, n_ptiles, yp_ref, ys_ref), wo_ref[att_w:, :]))
    x1_ref[...] = x1
    h = _rms(x1, g_ref[...]).astype(BF16)
    _head_rms_store(q_ref, _dot(h, wq_ref[...]), qg_ref[...], MEM_HEAD_DIM ** -0.5)


def _out_proj(x, att, yssm, w_out, norm_mem, w_mem_q, mem_q_gain):
    tm = TOKEN_TILE
    n_prompt, d = x[0].shape
    assert all(a[1].shape[0] == tm and a[0].shape[0] == n_prompt for a in (x, att, yssm))
    n = n_prompt + tm
    n_ptiles = n_prompt // tm
    att_w, ssm_w = att[0].shape[1], yssm[0].shape[1]
    mw = w_mem_q.shape[1]
    row = lambda i: (i, 0)
    fix = lambda i: (0, 0)
    first = lambda i: (jnp.minimum(i, n_ptiles - 1), 0)
    pair = lambda w: [pl.BlockSpec((tm, w), first), pl.BlockSpec((tm, w), fix)]
    return pl.pallas_call(
        functools.partial(_out_proj_kernel, n_ptiles=n_ptiles),
        grid=(n // tm,),
        in_specs=pair(d) + pair(att_w) + pair(ssm_w)
        + [pl.BlockSpec((att_w + ssm_w, d), fix), pl.BlockSpec((1, d), fix),
           pl.BlockSpec((d, mw), fix), pl.BlockSpec((1, MEM_HEAD_DIM), fix)],
        out_specs=[pl.BlockSpec((tm, d), row), pl.BlockSpec((tm, mw), row)],
        out_shape=[jax.ShapeDtypeStruct((n, d), F32), jax.ShapeDtypeStruct((n, mw), BF16)],
        compiler_params=_cparams(("parallel",)),
        name="out_proj",
    )(*x, *att, *yssm, w_out.astype(BF16), norm_mem.reshape(1, d), w_mem_q.astype(BF16),
      mem_q_gain.reshape(1, MEM_HEAD_DIM))


def _mem_attn_kernel(q_ref, k_ref, v_ref, o_ref):
    for hd in range(q_ref.shape[-1] // MEM_HEAD_DIM):
        sl = slice(hd * MEM_HEAD_DIM, (hd + 1) * MEM_HEAD_DIM)
        head = (lambda r: r[0, :, hd, :]) if len(k_ref.shape) == 4 else (lambda r: r[0, :, sl])
        s = _dot_nt(q_ref[:, sl], head(k_ref).astype(BF16))
        p = jnp.exp(s - jnp.max(s, axis=-1, keepdims=True))
        den = jnp.sum(p, axis=-1, keepdims=True)
        o_ref[:, sl] = (_dot(p.astype(BF16), head(v_ref).astype(BF16)) / den).astype(o_ref.dtype)


def _mem_attn(q, mem_k, mem_v, b, l, tq):
    w = q.shape[-1]
    n_mem = mem_k.shape[1]
    assert l % tq == 0
    qs = pl.BlockSpec((tq, w), lambda i, j: (i * (l // tq) + j, 0))
    ms = pl.BlockSpec((1,) + mem_k.shape[1:], lambda i, j: (i,) + (0,) * (mem_k.ndim - 1))
    return pl.pallas_call(
        _mem_attn_kernel,
        grid=(b, l // tq),
        in_specs=[qs, ms, ms],
        out_specs=qs,
        out_shape=jax.ShapeDtypeStruct((b * l, w), BF16),
        compiler_params=_cparams(("parallel", "parallel")),
        name="mem_attn",
    )(q, mem_k, mem_v)


def _store_row_tiles(dst_ref, x):
    rows = x.shape[0]
    for j in range(x.shape[1] // LANES):
        dst_ref[pl.ds(j, rows, stride=SUBLANES), :] = x[:, j * LANES:(j + 1) * LANES]


def _load_row_tiles(src_ref, first, rows):
    return jnp.concatenate([src_ref[pl.ds(first * SUBLANES + j, rows, stride=SUBLANES), :]
                            for j in range(SUBLANES)], axis=1)


def _post_kernel(x1_ref, op_ref, os_ref, wo_ref, g_ref, wrh_ref, wrl_ref, br_ref,
                 x2_ref, hf_ref, idx_ref, w_ref, *, n_ptiles):
    x2 = x1_ref[...] + _dot(_two_source(pl.program_id(0), n_ptiles, op_ref, os_ref), wo_ref[...])
    x2_ref[...] = x2
    hf = _rms(x2, g_ref[...])
    _store_row_tiles(hf_ref, hf)
    hi = hf.astype(BF16)
    lo = (hf - hi.astype(F32)).astype(BF16)
    logits = (_dot(hi, wrh_ref[...]) + _dot(lo, wrh_ref[...]) + _dot(hi, wrl_ref[...])) + br_ref[...]
    lane = lax.broadcasted_iota(jnp.int32, logits.shape, 1)
    vals, idxs = [], []
    for _ in range(TOP_K):
        m = jnp.max(logits, axis=-1, keepdims=True)
        ix = jnp.min(jnp.where(logits == m, lane, LANES), axis=-1, keepdims=True)
        vals.append(m)
        idxs.append(ix)
        logits = jnp.where(lane == ix, -jnp.inf, logits)
    es = [jnp.exp(v - vals[0]) for v in vals]
    tot = es[0]
    for e in es[1:]:
        tot = tot + e
    wout = jnp.zeros(logits.shape, F32)
    iout = jnp.zeros(logits.shape, jnp.int32)
    for kk in range(TOP_K):
        wout = jnp.where(lane == kk, es[kk] / tot, wout)
        iout = jnp.where(lane == kk, idxs[kk], iout)
    w_ref[...] = wout
    idx_ref[...] = iout


def _post(x1, o, w_mem_o, norm_ffn, w_router, b_router):
    n, d = x1.shape
    mw = o[0].shape[1]
    n_exp = w_router.shape[1]
    assert n_exp <= LANES
    tm = TOKEN_TILE
    assert o[1].shape[0] == tm and o[0].shape[0] + tm == n
    n_ptiles = o[0].shape[0] // tm
    wr = jnp.pad(w_router, ((0, 0), (0, LANES - n_exp)))
    wrh = wr.astype(BF16)
    wrl = (wr - wrh.astype(F32)).astype(BF16)
    br = jnp.concatenate([b_router.astype(F32), jnp.full((LANES - n_exp,), NEG, F32)]).reshape(1, LANES)
    row = lambda i: (i, 0)
    fix = lambda i: (0, 0)
    return pl.pallas_call(
        functools.partial(_post_kernel, n_ptiles=n_ptiles),
        grid=(n // tm,),
        in_specs=[pl.BlockSpec((tm, d), row),
                  pl.BlockSpec((tm, mw), lambda i: (jnp.minimum(i, n_ptiles - 1), 0)),
                  pl.BlockSpec((tm, mw), fix), pl.BlockSpec((mw, d), fix),
                  pl.BlockSpec((1, d), fix), pl.BlockSpec((d, LANES), fix), pl.BlockSpec((d, LANES), fix),
                  pl.BlockSpec((1, LANES), fix)],
        out_specs=[pl.BlockSpec((tm, d), row), pl.BlockSpec((tm * SUBLANES, LANES), row),
                   pl.BlockSpec((tm, LANES), row), pl.BlockSpec((tm, LANES), row)],
        out_shape=[jax.ShapeDtypeStruct((n, d), F32), jax.ShapeDtypeStruct((n * SUBLANES, LANES), F32),
                   jax.ShapeDtypeStruct((n, LANES), jnp.int32), jax.ShapeDtypeStruct((n, LANES), F32)],
        compiler_params=_cparams(("parallel",)),
        name="post",
    )(x1, *o, w_mem_o.astype(BF16), norm_ffn.reshape(1, d), wrh, wrl, br)


GU_CHUNK = 2 * LANES


def _regroup_kernel(w_ref, p_ref, o_ref):
    for c in range(w_ref.shape[-1] // GU_CHUNK):
        sl = slice(c * GU_CHUNK, (c + 1) * GU_CHUNK)
        o_ref[0, :, sl] = _dot(w_ref[0, :, sl].astype(BF16), p_ref[...]).astype(BF16)


def _regroup_gate_up(w_gate_up):
    n_exp, d, ff2 = w_gate_up.shape
    wblk = 2 * GU_CHUNK
    assert ff2 % wblk == 0
    src = np.arange(GU_CHUNK)
    dst = np.where(src % 2 == 0, src // 2, LANES + src // 2)
    perm = np.zeros((GU_CHUNK, GU_CHUNK), np.float32)
    perm[src, dst] = 1.0
    blk = pl.BlockSpec((1, d, wblk), lambda e, j: (e, 0, j))
    return pl.pallas_call(
        _regroup_kernel,
        grid=(n_exp, ff2 // wblk),
        in_specs=[blk, pl.BlockSpec((GU_CHUNK, GU_CHUNK), lambda e, j: (0, 0))],
        out_specs=blk,
        out_shape=jax.ShapeDtypeStruct((n_exp, d, ff2), BF16),
        compiler_params=_cparams(("parallel", "parallel")),
        name="regroup_gate_up",
    )(w_gate_up, jnp.asarray(perm, BF16))


def _rank_kernel(idx_ref, rank_ref, cnt_ref, carry_ref):
    @pl.when(pl.program_id(0) == 0)
    def _():
        carry_ref[...] = jnp.zeros(carry_ref.shape, F32)

    idx = idx_ref[...]
    tm = idx.shape[0]
    lane = lax.broadcasted_iota(jnp.int32, idx.shape, 1)
    hot = [(lane == idx[:, kk:kk + 1]).astype(F32) for kk in range(TOP_K)]
    tot = hot[0]
    for h in hot[1:]:
        tot = tot + h
    earlier = (lax.broadcasted_iota(jnp.int32, (tm, tm), 1)
               < lax.broadcasted_iota(jnp.int32, (tm, tm), 0)).astype(BF16)
    base = carry_ref[...] + _dot(earlier, tot.astype(BF16))
    out = jnp.zeros(idx.shape, jnp.int32)
    for kk in range(TOP_K):
        r = jnp.sum(hot[kk] * base, axis=-1, keepdims=True)
        out = jnp.where(lane == kk, r.astype(jnp.int32), out)
        base = base + hot[kk]
    rank_ref[...] = out
    carry_ref[...] = carry_ref[...] + jnp.sum(tot, axis=0, keepdims=True)
    cnt_ref[...] = carry_ref[...]


def _rank(top_idx):
    n, w = top_idx.shape
    tm = TOKEN_TILE
    return pl.pallas_call(
        _rank_kernel,
        grid=(n // tm,),
        in_specs=[pl.BlockSpec((tm, w), lambda i: (i, 0))],
        out_specs=[pl.BlockSpec((tm, w), lambda i: (i, 0)), pl.BlockSpec((1, w), lambda i: (0, 0))],
        out_shape=[jax.ShapeDtypeStruct((n, w), jnp.int32), jax.ShapeDtypeStruct((1, w), F32)],
        scratch_shapes=[pltpu.VMEM((1, w), F32)],
        compiler_params=_cparams(("arbitrary",)),
        name="rank",
    )(top_idx)


def _dispatch_kernel(seg_start_ref, seg_len_ref, dest_ref, hf_ref, xb_out, zrow, sem, zsem):
    tc = hf_ref.shape[0] // SUBLANES

    def slot(ref, row8):
        return ref.at[pl.ds(pl.multiple_of(row8, SUBLANES), SUBLANES), :]

    @pl.when(pl.program_id(0) == 0)
    def _():
        zrow[...] = jnp.zeros(zrow.shape, zrow.dtype)

        def zero_copy(row):
            return pltpu.make_async_copy(zrow, slot(xb_out, row * SUBLANES), zsem)

        def per_segment(op):
            def seg(s, carry):
                base = seg_start_ref[s]

                def row(r, c):
                    op(zero_copy(base + r))
                    return c

                return lax.fori_loop(0, seg_len_ref[s], row, carry)

            lax.fori_loop(0, seg_start_ref.shape[0], seg, 0)

        per_segment(lambda cp: cp.start())
        per_segment(lambda cp: cp.wait())

    for r in range(TOP_K * tc):
        pltpu.make_async_copy(slot(hf_ref, (r % tc) * SUBLANES), slot(xb_out, dest_ref[0, 0, r]),
                              sem).start(priority=r % 2)
    for _ in range(TOP_K):
        pltpu.make_async_copy(hf_ref, xb_out.at[pl.ds(0, tc * SUBLANES), :], sem).wait()


def _experts_kernel(be_ref, nact_ref, x_ref, wgu_ref, bgu_ref, wd_ref, bd_ref, y_ref):
    i = pl.program_id(0)

    tm = x_ref.shape[0] // SUBLANES

    @pl.when(i < nact_ref[0])
    def _():
        gu = _dot(_load_row_tiles(x_ref, 0, tm).astype(BF16), wgu_ref[0]) + bgu_ref[0]
        acts = []
        for c in range(gu.shape[1] // GU_CHUNK):
            g = jnp.minimum(gu[:, c * GU_CHUNK:c * GU_CHUNK + LANES], SWIGLU_LIMIT)
            u = jnp.clip(gu[:, c * GU_CHUNK + LANES:(c + 1) * GU_CHUNK], -SWIGLU_LIMIT, SWIGLU_LIMIT)
            acts.append(((u + 1.0) * (g * jax.nn.sigmoid(SWIGLU_ALPHA * g))).astype(BF16))
        _store_row_tiles(y_ref, _dot(jnp.concatenate(acts, axis=1), wd_ref[0]) + bd_ref[0])

    @pl.when(i >= nact_ref[0])
    def _():
        y_ref[...] = jnp.zeros(y_ref.shape, y_ref.dtype)


def _combine_kernel(dest_ref, x2_ref, w_ref, y_hbm, op_ref, os_ref, buf, sem, *, n_ptiles):
    tc = x2_ref.shape[0]

    for r in range(TOP_K * tc):
        src = y_hbm.at[pl.ds(pl.multiple_of(dest_ref[0, 0, r], SUBLANES), SUBLANES), :]
        pltpu.make_async_copy(src, buf.at[pl.ds(r * SUBLANES, SUBLANES), :], sem).start(priority=r % 2)
    pltpu.make_async_copy(y_hbm.at[pl.ds(0, TOP_K * tc * SUBLANES), :], buf, sem).wait()
    acc = x2_ref[...]
    w = w_ref[...]
    for kk in range(TOP_K):
        acc = acc + w[:, kk:kk + 1] * _load_row_tiles(buf, kk * tc, tc)

    @pl.when(pl.program_id(0) < n_ptiles)
    def _():
        op_ref[...] = acc

    @pl.when(pl.program_id(0) >= n_ptiles)
    def _():
        os_ref[...] = acc


def _moe(hf, x2, top_idx, top_w, w_gate_up, b_gate_up, w_down, b_down, n_prompt):
    n_tok, d = x2.shape
    n_exp, _, ff2 = w_gate_up.shape
    ff = ff2 // 2
    tm, tc = MOE_TILE, COMBINE_TILE
    row8 = SUBLANES
    assert n_tok % tc == 0 and d == row8 * LANES and hf.shape == (n_tok * row8, LANES)
    n_assign = n_tok * TOP_K
    n_blocks = -(-(n_assign + n_exp * (tm - 1)) // tm)
    n_rows = n_blocks * tm

    rank, cnt = _rank(top_idx)
    counts = cnt[0, :n_exp].astype(jnp.int32)
    padded = (counts + tm - 1) // tm * tm
    pend = jnp.cumsum(padded)
    pstart = pend - padded
    choice = top_idx[:, :TOP_K]
    first = jnp.sum(jnp.where(choice[:, :, None] == jnp.arange(n_exp, dtype=jnp.int32), pstart, 0), axis=-1)
    dest = ((first + rank[:, :TOP_K]) * row8).astype(jnp.int32)
    dest_tiles = dest.reshape(n_tok // tc, tc, TOP_K).transpose(0, 2, 1).reshape(n_tok // tc, 1, TOP_K * tc)
    block_e = jnp.minimum(jnp.sum(jnp.arange(n_blocks, dtype=jnp.int32)[None, :] * tm >= pend[:, None], axis=0),
                          n_exp - 1).astype(jnp.int32)
    n_active = (pend[-1:] // tm).astype(jnp.int32)
    seg_start = jnp.concatenate([pstart + counts, pend[-1:]]).astype(jnp.int32)
    seg_len = jnp.concatenate([padded - counts, n_rows - pend[-1:]]).astype(jnp.int32)

    xb = pl.pallas_call(
        _dispatch_kernel,
        grid_spec=pltpu.PrefetchScalarGridSpec(
            num_scalar_prefetch=2,
            grid=(n_tok // tc,),
            in_specs=[pl.BlockSpec((1, 1, TOP_K * tc), lambda i, ss, sl: (i, 0, 0), memory_space=pltpu.SMEM),
                      pl.BlockSpec((tc * row8, LANES), lambda i, ss, sl: (i, 0))],
            out_specs=pl.BlockSpec(memory_space=pl.ANY),
            scratch_shapes=[pltpu.VMEM((row8, LANES), F32), pltpu.SemaphoreType.DMA(()),
                            pltpu.SemaphoreType.DMA(())]),
        out_shape=jax.ShapeDtypeStruct((n_rows * row8, LANES), F32),
        compiler_params=_cparams(("arbitrary",)),
        name="dispatch",
    )(seg_start, seg_len, dest_tiles, hf)

    wgu = _regroup_gate_up(w_gate_up)
    bgu = b_gate_up.reshape(n_exp, ff2 // GU_CHUNK, LANES, 2).transpose(0, 1, 3, 2).reshape(n_exp, 1, ff2)
    by_e = lambda i, be, na: (be[i], 0, 0)
    yb = pl.pallas_call(
        _experts_kernel,
        grid_spec=pltpu.PrefetchScalarGridSpec(
            num_scalar_prefetch=2,
            grid=(n_blocks,),
            in_specs=[pl.BlockSpec((tm * row8, LANES), lambda i, be, na: (jnp.minimum(i, na[0] - 1), 0)),
                      pl.BlockSpec((1, d, ff2), by_e), pl.BlockSpec((1, 1, ff2), by_e),
                      pl.BlockSpec((1, ff, d), by_e), pl.BlockSpec((1, 1, d), by_e)],
            out_specs=pl.BlockSpec((tm * row8, LANES), lambda i, be, na: (i, 0))),
        out_shape=jax.ShapeDtypeStruct((n_rows * row8, LANES), F32),
        compiler_params=_cparams(("arbitrary",)),
        name="experts",
    )(block_e, n_active, xb, wgu, bgu, w_down.astype(BF16), b_down.reshape(n_exp, 1, d))

    assert n_prompt % tc == 0 and 0 < n_prompt < n_tok
    n_ptiles = n_prompt // tc
    return pl.pallas_call(
        functools.partial(_combine_kernel, n_ptiles=n_ptiles),
        grid=(n_tok // tc,),
        in_specs=[pl.BlockSpec((1, 1, TOP_K * tc), lambda i: (i, 0, 0), memory_space=pltpu.SMEM),
                  pl.BlockSpec((tc, d), lambda i: (i, 0)),
                  pl.BlockSpec((tc, LANES), lambda i: (i, 0)),
                  pl.BlockSpec(memory_space=pl.ANY)],
        out_specs=[pl.BlockSpec((tc, d), lambda i: (jnp.minimum(i, n_ptiles - 1), 0)),
                   pl.BlockSpec((tc, d), lambda i: (jnp.maximum(i - n_ptiles, 0), 0))],
        out_shape=[jax.ShapeDtypeStruct((n_prompt, d), F32),
                   jax.ShapeDtypeStruct((n_tok - n_prompt, d), F32)],
        scratch_shapes=[pltpu.VMEM((TOP_K * tc * row8, LANES), F32), pltpu.SemaphoreType.DMA(())],
        compiler_params=_cparams(("arbitrary",)),
        name="combine",
    )(dest_tiles, x2, top_w, yb)


def kernel(x_prompt, x_sample, cache_win_k, cache_win_v, state_conv, state_ssm, cache_mem_k, cache_mem_v,
           mem_prompt, norm_mix, w_in, q_gain, k_gain, conv_w, conv_b, dt_bias, a_log, d_skip, ssm_out_gain,
           w_out, norm_mem, mem_in_gain, w_mem_q, w_mem_k, w_mem_v, mem_q_gain, mem_k_gain, w_mem_o,
           norm_ffn, w_router, b_router, w_gate_up, b_gate_up, w_down, b_down):
    bp, lp, d = x_prompt.shape
    bs, ls, _ = x_sample.shape
    depth = norm_mix.shape[0]
    n_buf = cache_win_k.shape[2]
    past_len = PAST_LEN
    n_mem = mem_prompt.shape[1]
    npr, nsm = bp * lp, bs * ls
    att_w = d // 2
    n_heads = att_w // ATT_HEAD_DIM
    ssm_w = d - att_w
    conv_ch = ssm_w + 2 * SSM_GROUPS * SSM_STATE
    tail = CONV_WIDTH - 1
    keep = min(max(w for w, _ in DILATIONS), lp)
    ls_pad = SUBLANES

    assert keep == lp
    xp_out, xs_out = x_prompt.reshape(npr, d).astype(F32), x_sample.reshape(nsm, d).astype(F32)
    outs = [[] for _ in range(10)]
    for i in range(depth):
        x = (xp_out, xs_out)
        q, k, v, z, xbc, dt_raw, kt_p, vt_p = _projections(*x, norm_mix[i], w_in[i], q_gain[i], k_gain[i],
                                                           lp, ls, past_len)
        smp = lambda a: a[npr:].reshape(bs, ls, a.shape[-1])
        pad_s = lambda a: jnp.pad(smp(a), ((0, 0), (0, ls_pad - ls), (0, 0)))
        pad_rows = lambda a: pad_s(a).reshape(bs * ls_pad, a.shape[-1])
        unpad = lambda a: a.reshape(bs, ls_pad, a.shape[-1])[:, :ls].reshape(nsm, a.shape[-1])
        att_p = _attn_prompt(q, k, v, bp, lp)
        heads = lambda a, b, l: a.reshape(b, l, n_heads, ATT_HEAD_DIM)
        k_new, v_new = heads(k[npr:], bs, ls), heads(v[npr:], bs, ls)
        cache_t = lambda c: jnp.transpose(c, (0, 2, 3, 1)).reshape(bs, att_w, n_buf)
        att_s = _attn_sample(smp(q), smp(k), smp(v), cache_t(cache_win_k[i]), cache_t(cache_win_v[i]))
        ssm_par = (conv_w[i], conv_b[i], dt_bias[i], a_log[i], d_skip[i], ssm_out_gain[i])
        y_p, st_p, cs_p = _ssd(xbc, dt_raw, z, jnp.zeros((bp, tail, conv_ch), F32),
                               jnp.zeros((bp, ssm_w // SSM_HEAD_DIM, SSM_HEAD_DIM, SSM_STATE), F32), *ssm_par,
                               bp, lp, lp)
        y_s, st_s, cs_s = _ssd(pad_rows(xbc), pad_rows(dt_raw), pad_rows(z), state_conv[i], state_ssm[i],
                               *ssm_par, bs, ls_pad, ls)
        x1, qm = _out_proj(x, (att_p, att_s.reshape(nsm, att_w)), (y_p, unpad(y_s)),
                           w_out[i], norm_mem[i], w_mem_q[i], mem_q_gain[i])
        mk_p, mv_p = _mem_kv(mem_prompt.reshape(bp * n_mem, d).astype(F32), mem_in_gain[i], w_mem_k[i],
                             w_mem_v[i], mem_k_gain[i])
        mw = mk_p.shape[-1]
        o_p = _mem_attn(qm, mk_p.reshape(bp, n_mem, mw), mv_p.reshape(bp, n_mem, mw), bp, lp, TOKEN_TILE)
        o_s = _mem_attn(pad_rows(qm), cache_mem_k[i], cache_mem_v[i], bs, ls_pad, ls_pad)
        x2, hf, top_idx, top_w = _post(x1, (o_p, unpad(o_s)), w_mem_o[i], norm_ffn[i], w_router[i], b_router[i])
        xp_out, xs_out = _moe(hf, x2, top_idx, top_w, w_gate_up[i], b_gate_up[i], w_down[i], b_down[i], npr)

        untr = lambda a: jnp.transpose(a.reshape(bp, n_heads, ATT_HEAD_DIM, lp), (0, 3, 1, 2))
        new = (untr(kt_p), untr(vt_p),
               k_new, v_new,
               cs_p, cs_s,
               st_p, st_s,
               mk_p.reshape(bp, n_mem, mw // MEM_HEAD_DIM, MEM_HEAD_DIM),
               mv_p.reshape(bp, n_mem, mw // MEM_HEAD_DIM, MEM_HEAD_DIM))
        for lst, val in zip(outs, new):
            lst.append(val)
    y_p = xp_out.reshape(bp, lp, d).astype(x_prompt.dtype)
    y_s = xs_out.reshape(bs, ls, d).astype(x_sample.dtype)
    return (y_p, y_s) + tuple(jnp.stack(o) for o in outs)
```

```python
import functools
import math

import numpy as np
import jax
import jax.numpy as jnp
from jax import lax
from jax.experimental import pallas as pl
from jax.experimental.pallas import tpu as pltpu

F32 = jnp.float32
BF16 = jnp.bfloat16

ATT_HEAD_DIM = 64
DILATIONS = ((128, 1), (512, 4), (2048, 16))
ATT_BLOCK = 128
ROPE_DIM = ATT_HEAD_DIM // 4
ROPE_THETA = 500000.0
PAST_LEN = 8192
SSM_HEAD_DIM = 64
SSM_GROUPS = 2
SSM_STATE = 128
CONV_WIDTH = 4
SSM_CHUNK = 128
MEM_HEAD_DIM = 128
TOP_K = 4
SWIGLU_LIMIT = 7.0
SWIGLU_ALPHA = 1.702
NORM_EPS = 1e-6

LANES = 128
SUBLANES = 8
VMEM_LIMIT = 56 * 1024 * 1024

TOKEN_TILE = 512
MOE_TILE = 256
COMBINE_TILE = 256
ATTN_UNROLL = 8
LOG2E = math.log2(math.e)
NEG = -1e30


def _cparams(sem):
    return pltpu.CompilerParams(dimension_semantics=sem, vmem_limit_bytes=VMEM_LIMIT)


def _rms(x, gain):
    return x * lax.rsqrt(jnp.mean(x * x, axis=-1, keepdims=True) + NORM_EPS) * gain


def _dot(a, b):
    return jnp.dot(a, b, preferred_element_type=F32)


def _dot_nt(a, b):
    return lax.dot_general(a, b, (((1,), (1,)), ((), ())), preferred_element_type=F32)


def _dot_tn(a, b):
    return lax.dot_general(a, b, (((0,), (0,)), ((), ())), preferred_element_type=F32)


def _dot_f32(a, b):
    return jnp.dot(a, b, preferred_element_type=F32, precision=lax.Precision.HIGHEST)


def _two_source(i, n_first, first_ref, second_ref):
    return lax.cond(i < n_first, lambda: first_ref[...], lambda: second_ref[...])


def _proj_kernel(xp_ref, xs_ref, g_ref, w_ref, qg_ref, kg_ref, seg_ref, cos_ref, s1_ref, s2_ref,
                 q_ref, k_ref, v_ref, z_ref, xbc_ref, dt_ref, kt_ref, vt_ref, *,
                 att_w, ssm_w, conv_ch, n_ptiles):
    i = pl.program_id(0)
    h = _rms(_two_source(i, n_ptiles, xp_ref, xs_ref), g_ref[...]).astype(BF16)
    seg = seg_ref[...]
    cos, s1, s2 = cos_ref[...], s1_ref[...], s2_ref[...]

    def head_norm_rope(t, gain):
        ms = _dot((t * t).astype(BF16), seg) * (1.0 / ATT_HEAD_DIM)
        tn = t * lax.rsqrt(ms + NORM_EPS) * gain
        half = ROPE_DIM // 2
        return (tn * cos + pltpu.roll(tn, half, 1) * s1
                + pltpu.roll(tn, att_w - half, 1) * s2)

    q = head_norm_rope(_dot(h, w_ref[:, 0:att_w]), qg_ref[...])
    q_ref[...] = q * (ATT_HEAD_DIM ** -0.5 * LOG2E)
    k = head_norm_rope(_dot(h, w_ref[:, att_w:2 * att_w]), kg_ref[...])
    v = _dot(h, w_ref[:, 2 * att_w:3 * att_w])
    k_ref[...] = k
    v_ref[...] = v

    @pl.when(i < n_ptiles)
    def _():
        kt_ref[0] = k.T
        vt_ref[0] = v.T

    o = 3 * att_w
    z_ref[...] = _dot(h, w_ref[:, o:o + ssm_w])
    o += ssm_w
    xbc_ref[...] = _dot(h, w_ref[:, o:o + conv_ch])
    o += conv_ch
    dt_ref[...] = _dot(h, w_ref[:, o:o + LANES])


def _rope_tables(pos, n_heads):
    half = ROPE_DIM // 2
    inv_freq = jnp.power(ROPE_THETA, -jnp.arange(half, dtype=F32) / half)
    ang = pos.astype(F32)[:, None] * inv_freq[None, :]
    cos, sin = jnp.cos(ang), jnp.sin(ang)
    n = pos.shape[0]
    rest = ATT_HEAD_DIM - ROPE_DIM
    c = jnp.concatenate([cos, cos, jnp.ones((n, rest), F32)], axis=-1)
    s1 = jnp.concatenate([jnp.zeros((n, half), F32), sin, jnp.zeros((n, rest), F32)], axis=-1)
    s2 = jnp.concatenate([-sin, jnp.zeros((n, half + rest), F32)], axis=-1)
    return tuple(jnp.tile(t, (1, n_heads)) for t in (c, s1, s2))


def _projections(x_p, x_s, norm_mix, w_in, q_gain, k_gain, seq, dec_seq, past_len):
    n_prompt, d = x_p.shape
    n = n_prompt + x_s.shape[0]
    d_half = d // 2
    att_w, ssm_w = d_half, d - d_half
    n_heads = att_w // ATT_HEAD_DIM
    ssm_heads = ssm_w // SSM_HEAD_DIM
    conv_ch = ssm_w + 2 * SSM_GROUPS * SSM_STATE
    tm = TOKEN_TILE
    assert n % tm == 0 and n_prompt % tm == 0 and seq % tm == 0 and (n - n_prompt) == tm
    assert tm % dec_seq == 0
    c0 = 3 * att_w + ssm_w
    w = jnp.concatenate([w_in[:, :c0], w_in[:, c0 + ssm_heads:],
                         w_in[:, c0:c0 + ssm_heads],
                         jnp.zeros((d, LANES - ssm_heads), w_in.dtype)], axis=1).astype(BF16)
    wn = w.shape[1]
    pos = jnp.concatenate([jnp.arange(seq, dtype=jnp.int32),
                           past_len + jnp.arange(tm, dtype=jnp.int32) % dec_seq])
    cos, s1, s2 = _rope_tables(pos, n_heads)
    tiles_per_seq = seq // tm
    n_prompt_tiles = n_prompt // tm
    head_id = np.arange(att_w) // ATT_HEAD_DIM
    seg = jnp.asarray(head_id[:, None] == head_id[None, :], BF16)

    def tab_map(i):
        return (jnp.where(i < n_prompt_tiles, i % tiles_per_seq, tiles_per_seq), 0)

    row = lambda i: (i, 0)
    fix = lambda i: (0, 0)
    first = lambda i: (jnp.minimum(i, n_prompt_tiles - 1), 0)

    def t_map(i):
        j = jnp.minimum(i, n_prompt_tiles - 1)
        return (j // tiles_per_seq, 0, j % tiles_per_seq)

    tab = pl.BlockSpec((tm, att_w), tab_map)
    tr = pl.BlockSpec((1, att_w, tm), t_map)
    kern = functools.partial(_proj_kernel, att_w=att_w, ssm_w=ssm_w, conv_ch=conv_ch, n_ptiles=n_prompt_tiles)
    return pl.pallas_call(
        kern,
        grid=(n // tm,),
        in_specs=[pl.BlockSpec((tm, d), first), pl.BlockSpec((tm, d), fix), pl.BlockSpec((1, d), fix),
                  pl.BlockSpec((d, wn), fix), pl.BlockSpec((1, att_w), fix),
                  pl.BlockSpec((1, att_w), fix), pl.BlockSpec((att_w, att_w), fix),
                  tab, tab, tab],
        out_specs=[pl.BlockSpec((tm, att_w), row), pl.BlockSpec((tm, att_w), row),
                   pl.BlockSpec((tm, att_w), row), pl.BlockSpec((tm, ssm_w), row),
                   pl.BlockSpec((tm, conv_ch), row), pl.BlockSpec((tm, LANES), row), tr, tr],
        out_shape=[jax.ShapeDtypeStruct((n, att_w), F32), jax.ShapeDtypeStruct((n, att_w), F32),
                   jax.ShapeDtypeStruct((n, att_w), F32), jax.ShapeDtypeStruct((n, ssm_w), F32),
                   jax.ShapeDtypeStruct((n, conv_ch), F32), jax.ShapeDtypeStruct((n, LANES), F32),
                   jax.ShapeDtypeStruct((n_prompt // seq, att_w, seq), F32),
                   jax.ShapeDtypeStruct((n_prompt // seq, att_w, seq), F32)],
        compiler_params=_cparams(("arbitrary",)),
        name="proj",
    )(x_p, x_s, norm_mix.reshape(1, d), w, jnp.tile(q_gain, n_heads).reshape(1, att_w),
      jnp.tile(k_gain, n_heads).reshape(1, att_w), seg, cos, s1, s2)


def _attn_prompt_kernel(q_ref, k_ref, v_ref, o_ref, num_ref, m_ref, den_ref, *, seq):
    blk = ATT_BLOCK
    lane = lax.broadcasted_iota(jnp.int32, (blk, LANES), 1)
    head0 = lane < ATT_HEAD_DIM
    qi = lax.broadcasted_iota(jnp.int32, (blk, 2 * blk), 0) + blk
    ki = lax.broadcasted_iota(jnp.int32, (blk, 2 * blk), 1)
    dist = qi - ki
    band = (dist >= 0) & (dist <= blk)
    bias_rest = jnp.where(band, 0.0, NEG)
    bias_first = jnp.where(band & (ki >= blk), 0.0, NEG)

    for di, (window, dil) in enumerate(DILATIONS):
        assert window // dil == blk
        nb = seq // dil // blk

        def body(i, carry, dil=dil, nb=nb, di=di):
            r = i // nb
            j = i % nb
            if dil > 1:
                start = r + dil * blk * j
                prev = r + dil * blk * jnp.maximum(j - 1, 0)
                rows = pl.ds(start, blk, stride=dil)
                prows = pl.ds(prev, blk, stride=dil)
            else:
                rows = pl.ds(pl.multiple_of(blk * j, blk), blk)
                prows = pl.ds(pl.multiple_of(blk * jnp.maximum(j - 1, 0), blk), blk)
            qb = q_ref[rows, :].astype(BF16)
            k2 = jnp.concatenate([k_ref[prows, :], k_ref[rows, :]], axis=0).astype(BF16)
            v2 = jnp.concatenate([v_ref[prows, :], v_ref[rows, :]], axis=0).astype(BF16)
            bias = jnp.where(j > 0, bias_rest, bias_first)
            zero = jnp.zeros_like(qb)
            q2 = jnp.concatenate([jnp.where(head0, qb, zero), jnp.where(head0, zero, qb)], axis=0)
            s = _dot_nt(q2, k2) + jnp.concatenate([bias, bias], axis=0)
            m = jnp.max(s, axis=-1, keepdims=True)
            p = jnp.exp2(s - m)
            den = jnp.sum(p, axis=-1, keepdims=True)
            pv = _dot(p.astype(BF16), v2)
            num_ref[di, rows, :] = jnp.where(head0, pv[:blk], pv[blk:])
            m_ref[di, rows, :] = jnp.where(head0, m[:blk], m[blk:])
            den_ref[di, rows, :] = jnp.where(head0, den[:blk], den[blk:])
            return carry

        lax.fori_loop(0, dil * nb, body, 0, unroll=ATTN_UNROLL)

    m_all = jnp.maximum(jnp.maximum(m_ref[0], m_ref[1]), m_ref[2])
    num = jnp.zeros((seq, LANES), F32)
    den = jnp.zeros((seq, LANES), F32)
    for di in range(len(DILATIONS)):
        a = jnp.exp2(m_ref[di] - m_all)
        num = num + a * num_ref[di]
        den = den + a * den_ref[di]
    o_ref[...] = (num / den).astype(o_ref.dtype)


def _attn_prompt(q, k, v, batch, seq):
    att_w = q.shape[1]
    pairs = att_w // LANES
    nd = len(DILATIONS)
    blk = pl.BlockSpec((seq, LANES), lambda b, h: (b, h))
    return pl.pallas_call(
        functools.partial(_attn_prompt_kernel, seq=seq),
        grid=(batch, pairs),
        in_specs=[blk, blk, blk],
        out_specs=blk,
        out_shape=jax.ShapeDtypeStruct((batch * seq, att_w), BF16),
        scratch_shapes=[pltpu.VMEM((nd, seq, LANES), F32), pltpu.VMEM((nd, seq, LANES), F32),
                        pltpu.VMEM((nd, seq, LANES), F32)],
        compiler_params=_cparams(("parallel", "parallel")),
        name="attn_prompt",
    )(q, k, v)


def _attn_sample_kernel(q_ref, kn_ref, vn_ref, kc_ref, vc_ref, cc_ref, cn_ref, o_ref, *,
                        dec_seq, n_heads):
    w = q_ref.shape[-1]
    rows = dec_seq * n_heads
    q = q_ref[0]
    qm = jnp.concatenate([jnp.broadcast_to(q[t:t + 1], (n_heads, w)) for t in range(dec_seq)], axis=0)
    lane_head = lax.broadcasted_iota(jnp.int32, (rows, w), 1) // ATT_HEAD_DIM
    row_head = lax.broadcasted_iota(jnp.int32, (rows, w), 0) % n_heads
    own = lane_head == row_head
    qm = jnp.where(own, qm, 0.0).astype(BF16)
    zpad = jnp.zeros((SUBLANES - dec_seq, w), F32)
    kn = jnp.concatenate([kn_ref[0], zpad], axis=0).astype(BF16)
    vn = jnp.concatenate([vn_ref[0], zpad], axis=0).astype(BF16)
    cc, cn = cc_ref[...], cn_ref[...]
    s_c = jnp.where(cc > 0, _dot(qm, kc_ref[0].astype(BF16)), -jnp.inf)
    s_n = jnp.where(cn > 0, _dot_nt(qm, kn), -jnp.inf)
    m = jnp.maximum(jnp.max(s_c, axis=-1, keepdims=True), jnp.max(s_n, axis=-1, keepdims=True))
    p_c = cc * jnp.exp2(s_c - m)
    p_n = cn * jnp.exp2(s_n - m)
    den = jnp.sum(p_c, axis=-1, keepdims=True) + jnp.sum(p_n, axis=-1, keepdims=True)
    o = _dot_nt(p_c.astype(BF16), vc_ref[0].astype(BF16)) + _dot(p_n.astype(BF16), vn)
    o = jnp.where(own, o / den, 0.0)
    o_ref[0] = jnp.sum(o.reshape(dec_seq, n_heads, w), axis=1).astype(o_ref.dtype)


def _attn_sample(q, k_new, v_new, k_cache_t, v_cache_t):
    b, t, w = q.shape
    n_buf = k_cache_t.shape[2]
    n_heads = w // ATT_HEAD_DIM
    assert n_heads == SUBLANES and t <= SUBLANES
    assert n_buf >= max(win for win, _ in DILATIONS)

    def count(dist):
        return sum(((dist >= 0) & (dist % dil == 0) & (dist <= win)).astype(np.float32)
                   for win, dil in DILATIONS)

    tq = np.repeat(np.arange(t), n_heads)[:, None]
    cc = count(n_buf + tq - np.arange(n_buf)[None, :])
    jn = np.arange(SUBLANES)[None, :]
    cn = np.where(jn < t, count(tq - jn), 0.0).astype(np.float32)
    rows = t * n_heads
    new = pl.BlockSpec((1, t, w), lambda i: (i, 0, 0))
    cache = pl.BlockSpec((1, w, n_buf), lambda i: (i, 0, 0))
    return pl.pallas_call(
        functools.partial(_attn_sample_kernel, dec_seq=t, n_heads=n_heads),
        grid=(b,),
        in_specs=[new, new, new, cache, cache,
                  pl.BlockSpec((rows, n_buf), lambda i: (0, 0)),
                  pl.BlockSpec((rows, SUBLANES), lambda i: (0, 0))],
        out_specs=new,
        out_shape=jax.ShapeDtypeStruct((b, t, w), BF16),
        compiler_params=_cparams(("parallel",)),
        name="attn_sample",
    )(q, k_new, v_new, k_cache_t, v_cache_t, jnp.asarray(cc), jnp.asarray(cn))


def _softplus(x):
    return jnp.maximum(x, 0.0) + jnp.log1p(jnp.exp(-jnp.abs(x)))


def _silu(x):
    return x * jax.nn.sigmoid(x)


def _ssd_kernel(xbc_ref, dt_ref, dtt_ref, z_ref, cp_ref, h0_ref, cw_ref, cb_ref, dtb_ref, dtbt_ref,
                al_ref, alt_ref, dsk_ref, og_ref, y_ref, hl_ref, ct_ref, xpad_ref, h_ref, *,
                q, valid, ssm_w, n_pairs):
    c = pl.program_id(1)
    tail = CONV_WIDTH - 1

    @pl.when(c == 0)
    def _():
        h_ref[...] = h0_ref[0]
        xpad_ref[0:SUBLANES, :] = jnp.zeros((SUBLANES, xpad_ref.shape[1]), F32)
        xpad_ref[SUBLANES - tail:SUBLANES, :] = cp_ref[0]

    x = xbc_ref[...]
    xpad_ref[SUBLANES:SUBLANES + q, :] = x
    conv = cb_ref[...]
    for j in range(CONV_WIDTH):
        o = SUBLANES - tail + j
        conv = conv + xpad_ref[o:o + q, :] * cw_ref[j:j + 1, :]

    @pl.when(c == pl.num_programs(1) - 1)
    def _():
        ct_ref[0] = xpad_ref[valid:valid + SUBLANES, :]

    xpad_ref[0:SUBLANES, :] = x[q - SUBLANES:q, :]
    u = _silu(conv)
    xs = u[:, :ssm_w]
    gw = SSM_STATE
    bm = u[:, ssm_w:ssm_w + SSM_GROUPS * gw].astype(BF16)
    cm = u[:, ssm_w + SSM_GROUPS * gw:].astype(BF16)

    dt = _softplus(dt_ref[...] + dtb_ref[...])
    dtt = _softplus(dtt_ref[0] + dtbt_ref[...])
    if valid < q:
        dt = jnp.where(lax.broadcasted_iota(jnp.int32, dt.shape, 0) < valid, dt, 0.0)
        dtt = jnp.where(lax.broadcasted_iota(jnp.int32, dtt.shape, 1) < valid, dtt, 0.0)
    ri = lax.broadcasted_iota(jnp.int32, (q, q), 0)
    ci = lax.broadcasted_iota(jnp.int32, (q, q), 1)
    causal = ci <= ri
    acum = _dot_f32(causal.astype(F32), dt * -jnp.exp(al_ref[...]))
    acumt = _dot_f32(dtt * -jnp.exp(alt_ref[...]), (ri <= ci).astype(F32))

    left = lax.broadcasted_iota(jnp.int32, (q, LANES), 1) < SSM_HEAD_DIM
    top = lax.broadcasted_iota(jnp.int32, (LANES, LANES), 0) < SSM_HEAD_DIM
    rep = 2 * n_pairs // SSM_GROUPS
    gmat = [_dot_nt(cm[:, g * gw:(g + 1) * gw], bm[:, g * gw:(g + 1) * gw]) for g in range(SSM_GROUPS)]
    ys = []
    for pr in range(n_pairs):
        ha, hb = 2 * pr, 2 * pr + 1
        g = ha // rep
        bg = bm[:, g * gw:(g + 1) * gw]
        cg = cm[:, g * gw:(g + 1) * gw]
        xpair = xs[:, pr * LANES:(pr + 1) * LANES]
        xd = xpair * jnp.where(left, dt[:, ha:ha + 1], dt[:, hb:hb + 1])
        xdb = xd.astype(BF16)
        yd = []
        for hh in (ha, hb):
            seg = acum[:, hh:hh + 1] - acumt[hh:hh + 1, :]
            decay = jnp.exp(jnp.where(causal, seg, -jnp.inf))
            yd.append(_dot((gmat[g] * decay).astype(BF16), xdb))
        ac = jnp.where(left, acum[:, ha:ha + 1], acum[:, hb:hb + 1])
        hprev = h_ref[pr]
        y_off = _dot_nt(cg, hprev.astype(BF16)) * jnp.exp(ac)
        to_end = jnp.exp(ac[q - 1:q, :] - ac)
        upd = _dot_tn((xd * to_end).astype(BF16), bg)
        cdec = jnp.where(top, jnp.exp(acum[q - 1:q, ha:ha + 1]), jnp.exp(acum[q - 1:q, hb:hb + 1]))
        h_ref[pr] = hprev * cdec + upd
        ys.append(jnp.where(left, yd[0], yd[1]) + y_off
                  + xpair * dsk_ref[:, pr * LANES:(pr + 1) * LANES])
    y = jnp.concatenate(ys, axis=1) * _silu(z_ref[...])
    y_ref[...] = _rms(y, og_ref[...]).astype(y_ref.dtype)

    @pl.when(c == pl.num_programs(1) - 1)
    def _():
        hl_ref[0] = h_ref[...]


def _ssd(xbc, dt_raw, z, conv_prev, h0, conv_w, conv_b, dt_bias, a_log, d_skip, out_gain, b, l, valid_len):
    conv_ch = xbc.shape[-1]
    ssm_w = z.shape[-1]
    n_heads = ssm_w // SSM_HEAD_DIM
    n_pairs = n_heads // 2
    assert 2 * SSM_HEAD_DIM == LANES and SSM_STATE == LANES and n_heads <= SUBLANES
    assert (n_heads // SSM_GROUPS) % 2 == 0
    q = SSM_CHUNK if l % SSM_CHUNK == 0 else l
    assert l % q == 0 and q % SUBLANES == 0 and (valid_len == l or q == l)
    nc = l // q
    dtt = jnp.swapaxes(dt_raw[:b * l, :n_heads].reshape(b, l, n_heads), 1, 2)
    pad_h = lambda a: jnp.pad(a.reshape(1, n_heads), ((0, 0), (0, LANES - n_heads)))
    per_lane = lambda a: jnp.repeat(a, SSM_HEAD_DIM).reshape(1, ssm_w)
    h0p = h0.reshape(b, n_pairs, LANES, SSM_STATE).astype(F32)
    tail = CONV_WIDTH - 1
    tok = lambda w: pl.BlockSpec((q, w), lambda i, j: (i * nc + j, 0))
    fix2 = lambda r, w: pl.BlockSpec((r, w), lambda i, j: (0, 0))
    st = pl.BlockSpec((1, n_pairs, LANES, SSM_STATE), lambda i, j: (i, 0, 0, 0))
    kern = functools.partial(_ssd_kernel, q=q, valid=valid_len if q == l else q, ssm_w=ssm_w, n_pairs=n_pairs)
    y, h_last, conv_tail = pl.pallas_call(
        kern,
        grid=(b, nc),
        in_specs=[tok(conv_ch), tok(LANES), pl.BlockSpec((1, n_heads, q), lambda i, j: (i, 0, j)),
                  tok(ssm_w), pl.BlockSpec((1, tail, conv_ch), lambda i, j: (i, 0, 0)), st,
                  fix2(CONV_WIDTH, conv_ch), fix2(1, conv_ch), fix2(1, LANES), fix2(n_heads, 1),
                  fix2(1, LANES), fix2(n_heads, 1), fix2(1, ssm_w), fix2(1, ssm_w)],
        out_specs=[tok(ssm_w), st, pl.BlockSpec((1, SUBLANES, conv_ch), lambda i, j: (i, 0, 0))],
        out_shape=[jax.ShapeDtypeStruct((b * l, ssm_w), BF16),
                   jax.ShapeDtypeStruct((b, n_pairs, LANES, SSM_STATE), F32),
                   jax.ShapeDtypeStruct((b, SUBLANES, conv_ch), F32)],
        scratch_shapes=[pltpu.VMEM((q + SUBLANES, conv_ch), F32),
                        pltpu.VMEM((n_pairs, LANES, SSM_STATE), F32)],
        compiler_params=_cparams(("parallel", "arbitrary")),
        name="ssd",
    )(xbc, dt_raw, dtt, z, conv_prev.astype(F32), h0p, conv_w, conv_b.reshape(1, conv_ch),
      pad_h(dt_bias), dt_bias.reshape(n_heads, 1), pad_h(a_log), a_log.reshape(n_heads, 1),
      per_lane(d_skip), out_gain.reshape(1, ssm_w))
    return y, h_last.reshape(b, n_heads, SSM_HEAD_DIM, SSM_STATE), conv_tail[:, SUBLANES - tail:]


def _head_rms_store(dst_ref, t, gain, scale=None):
    for hd in range(t.shape[1] // MEM_HEAD_DIM):
        sl = slice(hd * MEM_HEAD_DIM, (hd + 1) * MEM_HEAD_DIM)
        r = _rms(t[:, sl], gain)
        if scale is not None:
            r = r * scale
        dst_ref[:, sl] = r.astype(dst_ref.dtype)


def _mem_kv_kernel(m_ref, g_ref, wk_ref, wv_ref, kg_ref, k_ref, v_ref):
    h = _rms(m_ref[...], g_ref[...]).astype(BF16)
    _head_rms_store(k_ref, _dot(h, wk_ref[...]), kg_ref[...])
    v_ref[...] = _dot(h, wv_ref[...])


def _mem_kv(mem, in_gain, w_k, w_v, k_gain):
    n, d = mem.shape
    mw = w_k.shape[1]
    tm = TOKEN_TILE
    assert n % tm == 0 and MEM_HEAD_DIM == LANES
    row = lambda i: (i, 0)
    fix = lambda i: (0, 0)
    return pl.pallas_call(
        _mem_kv_kernel,
        grid=(n // tm,),
        in_specs=[pl.BlockSpec((tm, d), row), pl.BlockSpec((1, d), fix), pl.BlockSpec((d, mw), fix),
                  pl.BlockSpec((d, mw), fix), pl.BlockSpec((1, MEM_HEAD_DIM), fix)],
        out_specs=[pl.BlockSpec((tm, mw), row), pl.BlockSpec((tm, mw), row)],
        out_shape=[jax.ShapeDtypeStruct((n, mw), F32), jax.ShapeDtypeStruct((n, mw), F32)],
        compiler_params=_cparams(("parallel",)),
        name="mem_kv",
    )(mem, in_gain.reshape(1, d), w_k.astype(BF16), w_v.astype(BF16), k_gain.reshape(1, MEM_HEAD_DIM))


def _out_proj_kernel(xp_ref, xs_ref, ap_ref, as_ref, yp_ref, ys_ref, wo_ref, g_ref, wq_ref, qg_ref,
                     x1_ref, q_ref, *, n_ptiles):
    i = pl.program_id(0)
    att_w = ap_ref.shape[1]
    x1 = (_two_source(i, n_ptiles, xp_ref, xs_ref)
          + _dot(_two_source(i, n_ptiles, ap_ref, as_ref), wo_ref[0:att_w, :])
          + _dot(_two_source(i, n_ptiles, yp_ref, ys_ref), wo_ref[att_w:, :]))
    x1_ref[...] = x1
    h = _rms(x1, g_ref[...]).astype(BF16)
    _head_rms_store(q_ref, _dot(h, wq_ref[...]), qg_ref[...], MEM_HEAD_DIM ** -0.5)


def _out_proj(x, att, yssm, w_out, norm_mem, w_mem_q, mem_q_gain):
    tm = TOKEN_TILE
    n_prompt, d = x[0].shape
    assert all(a[1].shape[0] == tm and a[0].shape[0] == n_prompt for a in (x, att, yssm))
    n = n_prompt + tm
    n_ptiles = n_prompt // tm
    att_w, ssm_w = att[0].shape[1], yssm[0].shape[1]
    mw = w_mem_q.shape[1]
    row = lambda i: (i, 0)
    fix = lambda i: (0, 0)
    first = lambda i: (jnp.minimum(i, n_ptiles - 1), 0)
    pair = lambda w: [pl.BlockSpec((tm, w), first), pl.BlockSpec((tm, w), fix)]
    return pl.pallas_call(
        functools.partial(_out_proj_kernel, n_ptiles=n_ptiles),
        grid=(n // tm,),
        in_specs=pair(d) + pair(att_w) + pair(ssm_w)
        + [pl.BlockSpec((att_w + ssm_w, d), fix), pl.BlockSpec((1, d), fix),
           pl.BlockSpec((d, mw), fix), pl.BlockSpec((1, MEM_HEAD_DIM), fix)],
        out_specs=[pl.BlockSpec((tm, d), row), pl.BlockSpec((tm, mw), row)],
        out_shape=[jax.ShapeDtypeStruct((n, d), F32), jax.ShapeDtypeStruct((n, mw), BF16)],
        compiler_params=_cparams(("parallel",)),
        name="out_proj",
    )(*x, *att, *yssm, w_out.astype(BF16), norm_mem.reshape(1, d), w_mem_q.astype(BF16),
      mem_q_gain.reshape(1, MEM_HEAD_DIM))


def _mem_attn_kernel(q_ref, k_ref, v_ref, o_ref):
    for hd in range(q_ref.shape[-1] // MEM_HEAD_DIM):
        sl = slice(hd * MEM_HEAD_DIM, (hd + 1) * MEM_HEAD_DIM)
        head = (lambda r: r[0, :, hd, :]) if len(k_ref.shape) == 4 else (lambda r: r[0, :, sl])
        s = _dot_nt(q_ref[:, sl], head(k_ref).astype(BF16))
        p = jnp.exp(s - jnp.max(s, axis=-1, keepdims=True))
        den = jnp.sum(p, axis=-1, keepdims=True)
        o_ref[:, sl] = (_dot(p.astype(BF16), head(v_ref).astype(BF16)) / den).astype(o_ref.dtype)


def _mem_attn(q, mem_k, mem_v, b, l, tq):
    w = q.shape[-1]
    n_mem = mem_k.shape[1]
    assert l % tq == 0
    qs = pl.BlockSpec((tq, w), lambda i, j: (i * (l // tq) + j, 0))
    ms = pl.BlockSpec((1,) + mem_k.shape[1:], lambda i, j: (i,) + (0,) * (mem_k.ndim - 1))
    return pl.pallas_call(
        _mem_attn_kernel,
        grid=(b, l // tq),
        in_specs=[qs, ms, ms],
        out_specs=qs,
        out_shape=jax.ShapeDtypeStruct((b * l, w), BF16),
        compiler_params=_cparams(("parallel", "parallel")),
        name="mem_attn",
    )(q, mem_k, mem_v)


def _store_row_tiles(dst_ref, x):
    rows = x.shape[0]
    for j in range(x.shape[1] // LANES):
        dst_ref[pl.ds(j, rows, stride=SUBLANES), :] = x[:, j * LANES:(j + 1) * LANES]


def _load_row_tiles(src_ref, first, rows):
    return jnp.concatenate([src_ref[pl.ds(first * SUBLANES + j, rows, stride=SUBLANES), :]
                            for j in range(SUBLANES)], axis=1)


def _post_kernel(x1_ref, op_ref, os_ref, wo_ref, g_ref, wrh_ref, wrl_ref, br_ref,
                 x2_ref, hf_ref, idx_ref, w_ref, *, n_ptiles):
    x2 = x1_ref[...] + _dot(_two_source(pl.program_id(0), n_ptiles, op_ref, os_ref), wo_ref[...])
    x2_ref[...] = x2
    hf = _rms(x2, g_ref[...])
    _store_row_tiles(hf_ref, hf)
    hi = hf.astype(BF16)
    lo = (hf - hi.astype(F32)).astype(BF16)
    logits = (_dot(hi, wrh_ref[...]) + _dot(lo, wrh_ref[...]) + _dot(hi, wrl_ref[...])) + br_ref[...]
    lane = lax.broadcasted_iota(jnp.int32, logits.shape, 1)
    vals, idxs = [], []
    for _ in range(TOP_K):
        m = jnp.max(logits, axis=-1, keepdims=True)
        ix = jnp.min(jnp.where(logits == m, lane, LANES), axis=-1, keepdims=True)
        vals.append(m)
        idxs.append(ix)
        logits = jnp.where(lane == ix, -jnp.inf, logits)
    es = [jnp.exp(v - vals[0]) for v in vals]
    tot = es[0]
    for e in es[1:]:
        tot = tot + e
    wout = jnp.zeros(logits.shape, F32)
    iout = jnp.zeros(logits.shape, jnp.int32)
    for kk in range(TOP_K):
        wout = jnp.where(lane == kk, es[kk] / tot, wout)
        iout = jnp.where(lane == kk, idxs[kk], iout)
    w_ref[...] = wout
    idx_ref[...] = iout


def _post(x1, o, w_mem_o, norm_ffn, w_router, b_router):
    n, d = x1.shape
    mw = o[0].shape[1]
    n_exp = w_router.shape[1]
    assert n_exp <= LANES
    tm = TOKEN_TILE
    assert o[1].shape[0] == tm and o[0].shape[0] + tm == n
    n_ptiles = o[0].shape[0] // tm
    wr = jnp.pad(w_router, ((0, 0), (0, LANES - n_exp)))
    wrh = wr.astype(BF16)
    wrl = (wr - wrh.astype(F32)).astype(BF16)
    br = jnp.concatenate([b_router.astype(F32), jnp.full((LANES - n_exp,), NEG, F32)]).reshape(1, LANES)
    row = lambda i: (i, 0)
    fix = lambda i: (0, 0)
    return pl.pallas_call(
        functools.partial(_post_kernel, n_ptiles=n_ptiles),
        grid=(n // tm,),
        in_specs=[pl.BlockSpec((tm, d), row),
                  pl.BlockSpec((tm, mw), lambda i: (jnp.minimum(i, n_ptiles - 1), 0)),
                  pl.BlockSpec((tm, mw), fix), pl.BlockSpec((mw, d), fix),
                  pl.BlockSpec((1, d), fix), pl.BlockSpec((d, LANES), fix), pl.BlockSpec((d, LANES), fix),
                  pl.BlockSpec((1, LANES), fix)],
        out_specs=[pl.BlockSpec((tm, d), row), pl.BlockSpec((tm * SUBLANES, LANES), row),
                   pl.BlockSpec((tm, LANES), row), pl.BlockSpec((tm, LANES), row)],
        out_shape=[jax.ShapeDtypeStruct((n, d), F32), jax.ShapeDtypeStruct((n * SUBLANES, LANES), F32),
                   jax.ShapeDtypeStruct((n, LANES), jnp.int32), jax.ShapeDtypeStruct((n, LANES), F32)],
        compiler_params=_cparams(("parallel",)),
        name="post",
    )(x1, *o, w_mem_o.astype(BF16), norm_ffn.reshape(1, d), wrh, wrl, br)


GU_CHUNK = 2 * LANES


def _regroup_kernel(w_ref, p_ref, o_ref):
    for c in range(w_ref.shape[-1] // GU_CHUNK):
        sl = slice(c * GU_CHUNK, (c + 1) * GU_CHUNK)
        o_ref[0, :, sl] = _dot(w_ref[0, :, sl].astype(BF16), p_ref[...]).astype(BF16)


def _regroup_gate_up(w_gate_up):
    n_exp, d, ff2 = w_gate_up.shape
    wblk = 2 * GU_CHUNK
    assert ff2 % wblk == 0
    src = np.arange(GU_CHUNK)
    dst = np.where(src % 2 == 0, src // 2, LANES + src // 2)
    perm = np.zeros((GU_CHUNK, GU_CHUNK), np.float32)
    perm[src, dst] = 1.0
    blk = pl.BlockSpec((1, d, wblk), lambda e, j: (e, 0, j))
    return pl.pallas_call(
        _regroup_kernel,
        grid=(n_exp, ff2 // wblk),
        in_specs=[blk, pl.BlockSpec((GU_CHUNK, GU_CHUNK), lambda e, j: (0, 0))],
        out_specs=blk,
        out_shape=jax.ShapeDtypeStruct((n_exp, d, ff2), BF16),
        compiler_params=_cparams(("parallel", "parallel")),
        name="regroup_gate_up",
    )(w_gate_up, jnp.asarray(perm, BF16))


def _rank_kernel(idx_ref, rank_ref, cnt_ref, carry_ref):
    @pl.when(pl.program_id(0) == 0)
    def _():
        carry_ref[...] = jnp.zeros(carry_ref.shape, F32)

    idx = idx_ref[...]
    tm = idx.shape[0]
    lane = lax.broadcasted_iota(jnp.int32, idx.shape, 1)
    hot = [(lane == idx[:, kk:kk + 1]).astype(F32) for kk in range(TOP_K)]
    tot = hot[0]
    for h in hot[1:]:
        tot = tot + h
    earlier = (lax.broadcasted_iota(jnp.int32, (tm, tm), 1)
               < lax.broadcasted_iota(jnp.int32, (tm, tm), 0)).astype(BF16)
    base = carry_ref[...] + _dot(earlier, tot.astype(BF16))
    out = jnp.zeros(idx.shape, jnp.int32)
    for kk in range(TOP_K):
        r = jnp.sum(hot[kk] * base, axis=-1, keepdims=True)
        out = jnp.where(lane == kk, r.astype(jnp.int32), out)
        base = base + hot[kk]
    rank_ref[...] = out
    carry_ref[...] = carry_ref[...] + jnp.sum(tot, axis=0, keepdims=True)
    cnt_ref[...] = carry_ref[...]


def _rank(top_idx):
    n, w = top_idx.shape
    tm = TOKEN_TILE
    return pl.pallas_call(
        _rank_kernel,
        grid=(n // tm,),
        in_specs=[pl.BlockSpec((tm, w), lambda i: (i, 0))],
        out_specs=[pl.BlockSpec((tm, w), lambda i: (i, 0)), pl.BlockSpec((1, w), lambda i: (0, 0))],
        out_shape=[jax.ShapeDtypeStruct((n, w), jnp.int32), jax.ShapeDtypeStruct((1, w), F32)],
        scratch_shapes=[pltpu.VMEM((1, w), F32)],
        compiler_params=_cparams(("arbitrary",)),
        name="rank",
    )(top_idx)


def _dispatch_kernel(seg_start_ref, seg_len_ref, dest_ref, hf_ref, xb_out, zrow, sem, zsem):
    tc = hf_ref.shape[0] // SUBLANES

    def slot(ref, row8):
        return ref.at[pl.ds(pl.multiple_of(row8, SUBLANES), SUBLANES), :]

    @pl.when(pl.program_id(0) == 0)
    def _():
        zrow[...] = jnp.zeros(zrow.shape, zrow.dtype)

        def zero_copy(row):
            return pltpu.make_async_copy(zrow, slot(xb_out, row * SUBLANES), zsem)

        def per_segment(op):
            def seg(s, carry):
                base = seg_start_ref[s]

                def row(r, c):
                    op(zero_copy(base + r))
                    return c

                return lax.fori_loop(0, seg_len_ref[s], row, carry)

            lax.fori_loop(0, seg_start_ref.shape[0], seg, 0)

        per_segment(lambda cp: cp.start())
        per_segment(lambda cp: cp.wait())

    for r in range(TOP_K * tc):
        pltpu.make_async_copy(slot(hf_ref, (r % tc) * SUBLANES), slot(xb_out, dest_ref[0, 0, r]),
                              sem).start(priority=r % 2)
    for _ in range(TOP_K):
        pltpu.make_async_copy(hf_ref, xb_out.at[pl.ds(0, tc * SUBLANES), :], sem).wait()


def _experts_kernel(be_ref, nact_ref, x_ref, wgu_ref, bgu_ref, wd_ref, bd_ref, y_ref):
    i = pl.program_id(0)

    tm = x_ref.shape[0] // SUBLANES

    @pl.when(i < nact_ref[0])
    def _():
        gu = _dot(_load_row_tiles(x_ref, 0, tm).astype(BF16), wgu_ref[0]) + bgu_ref[0]
        acts = []
        for c in range(gu.shape[1] // GU_CHUNK):
            g = jnp.minimum(gu[:, c * GU_CHUNK:c * GU_CHUNK + LANES], SWIGLU_LIMIT)
            u = jnp.clip(gu[:, c * GU_CHUNK + LANES:(c + 1) * GU_CHUNK], -SWIGLU_LIMIT, SWIGLU_LIMIT)
            acts.append(((u + 1.0) * (g * jax.nn.sigmoid(SWIGLU_ALPHA * g))).astype(BF16))
        _store_row_tiles(y_ref, _dot(jnp.concatenate(acts, axis=1), wd_ref[0]) + bd_ref[0])

    @pl.when(i >= nact_ref[0])
    def _():
        y_ref[...] = jnp.zeros(y_ref.shape, y_ref.dtype)


def _combine_kernel(dest_ref, x2_ref, w_ref, y_hbm, op_ref, os_ref, buf, sems, *, n_tiles, n_ptiles):
    i = pl.program_id(0)
    tc = x2_ref.shape[0]

    def start_gather(slot):
        for r in range(TOP_K * tc):
            src = y_hbm.at[pl.ds(pl.multiple_of(dest_ref[0, 0, r], SUBLANES), SUBLANES), :]
            pltpu.make_async_copy(src, buf.at[slot, pl.ds(r * SUBLANES, SUBLANES), :],
                                  sems.at[slot]).start(priority=r % 2)

    def finish(slot):
        rows = buf.at[slot]
        pltpu.make_async_copy(y_hbm.at[pl.ds(0, TOP_K * tc * SUBLANES), :], rows, sems.at[slot]).wait()
        acc = x2_ref[...]
        w = w_ref[...]
        for kk in range(TOP_K):
            acc = acc + w[:, kk:kk + 1] * _load_row_tiles(rows, kk * tc, tc)

        @pl.when(i - 1 < n_ptiles)
        def _():
            op_ref[...] = acc

        @pl.when(i - 1 >= n_ptiles)
        def _():
            os_ref[...] = acc

    for slot in range(2):
        pl.when((i < n_tiles) & (i % 2 == slot))(functools.partial(start_gather, slot))
    for slot in range(2):
        pl.when((i > 0) & ((i + 1) % 2 == slot))(functools.partial(finish, slot))


def _moe(hf, x2, top_idx, top_w, w_gate_up, b_gate_up, w_down, b_down, n_prompt):
    n_tok, d = x2.shape
    n_exp, _, ff2 = w_gate_up.shape
    ff = ff2 // 2
    tm, tc = MOE_TILE, COMBINE_TILE
    row8 = SUBLANES
    assert n_tok % tc == 0 and d == row8 * LANES and hf.shape == (n_tok * row8, LANES)
    n_assign = n_tok * TOP_K
    n_blocks = -(-(n_assign + n_exp * (tm - 1)) // tm)
    n_rows = n_blocks * tm

    rank, cnt = _rank(top_idx)
    counts = cnt[0, :n_exp].astype(jnp.int32)
    padded = (counts + tm - 1) // tm * tm
    pend = jnp.cumsum(padded)
    pstart = pend - padded
    choice = top_idx[:, :TOP_K]
    first = jnp.sum(jnp.where(choice[:, :, None] == jnp.arange(n_exp, dtype=jnp.int32), pstart, 0), axis=-1)
    dest = ((first + rank[:, :TOP_K]) * row8).astype(jnp.int32)
    dest_tiles = dest.reshape(n_tok // tc, tc, TOP_K).transpose(0, 2, 1).reshape(n_tok // tc, 1, TOP_K * tc)
    block_e = jnp.minimum(jnp.sum(jnp.arange(n_blocks, dtype=jnp.int32)[None, :] * tm >= pend[:, None], axis=0),
                          n_exp - 1).astype(jnp.int32)
    n_active = (pend[-1:] // tm).astype(jnp.int32)
    seg_start = jnp.concatenate([pstart + counts, pend[-1:]]).astype(jnp.int32)
    seg_len = jnp.concatenate([padded - counts, n_rows - pend[-1:]]).astype(jnp.int32)

    xb = pl.pallas_call(
        _dispatch_kernel,
        grid_spec=pltpu.PrefetchScalarGridSpec(
            num_scalar_prefetch=2,
            grid=(n_tok // tc,),
            in_specs=[pl.BlockSpec((1, 1, TOP_K * tc), lambda i, ss, sl: (i, 0, 0), memory_space=pltpu.SMEM),
                      pl.BlockSpec((tc * row8, LANES), lambda i, ss, sl: (i, 0))],
            out_specs=pl.BlockSpec(memory_space=pl.ANY),
            scratch_shapes=[pltpu.VMEM((row8, LANES), F32), pltpu.SemaphoreType.DMA(()),
                            pltpu.SemaphoreType.DMA(())]),
        out_shape=jax.ShapeDtypeStruct((n_rows * row8, LANES), F32),
        compiler_params=_cparams(("arbitrary",)),
        name="dispatch",
    )(seg_start, seg_len, dest_tiles, hf)

    wgu = _regroup_gate_up(w_gate_up)
    bgu = b_gate_up.reshape(n_exp, ff2 // GU_CHUNK, LANES, 2).transpose(0, 1, 3, 2).reshape(n_exp, 1, ff2)
    by_e = lambda i, be, na: (be[i], 0, 0)
    yb = pl.pallas_call(
        _experts_kernel,
        grid_spec=pltpu.PrefetchScalarGridSpec(
            num_scalar_prefetch=2,
            grid=(n_blocks,),
            in_specs=[pl.BlockSpec((tm * row8, LANES), lambda i, be, na: (jnp.minimum(i, na[0] - 1), 0)),
                      pl.BlockSpec((1, d, ff2), by_e), pl.BlockSpec((1, 1, ff2), by_e),
                      pl.BlockSpec((1, ff, d), by_e), pl.BlockSpec((1, 1, d), by_e)],
            out_specs=pl.BlockSpec((tm * row8, LANES), lambda i, be, na: (i, 0))),
        out_shape=jax.ShapeDtypeStruct((n_rows * row8, LANES), F32),
        compiler_params=_cparams(("arbitrary",)),
        name="experts",
    )(block_e, n_active, xb, wgu, bgu, w_down.astype(BF16), b_down.reshape(n_exp, 1, d))

    assert n_prompt % tc == 0 and 0 < n_prompt < n_tok
    n_ptiles = n_prompt // tc
    n_tiles = n_tok // tc
    prev = lambda i: jnp.maximum(i - 1, 0)
    return pl.pallas_call(
        functools.partial(_combine_kernel, n_tiles=n_tiles, n_ptiles=n_ptiles),
        grid=(n_tiles + 1,),
        in_specs=[pl.BlockSpec((1, 1, TOP_K * tc), lambda i: (jnp.minimum(i, n_tiles - 1), 0, 0),
                               memory_space=pltpu.SMEM),
                  pl.BlockSpec((tc, d), lambda i: (prev(i), 0)),
                  pl.BlockSpec((tc, LANES), lambda i: (prev(i), 0)),
                  pl.BlockSpec(memory_space=pl.ANY)],
        out_specs=[pl.BlockSpec((tc, d), lambda i: (jnp.minimum(prev(i), n_ptiles - 1), 0)),
                   pl.BlockSpec((tc, d), lambda i: (jnp.maximum(prev(i) - n_ptiles, 0), 0))],
        out_shape=[jax.ShapeDtypeStruct((n_prompt, d), F32),
                   jax.ShapeDtypeStruct((n_tok - n_prompt, d), F32)],
        scratch_shapes=[pltpu.VMEM((2, TOP_K * tc * row8, LANES), F32), pltpu.SemaphoreType.DMA((2,))],
        compiler_params=_cparams(("arbitrary",)),
        name="combine",
    )(dest_tiles, x2, top_w, yb)


def kernel(x_prompt, x_sample, cache_win_k, cache_win_v, state_conv, state_ssm, cache_mem_k, cache_mem_v,
           mem_prompt, norm_mix, w_in, q_gain, k_gain, conv_w, conv_b, dt_bias, a_log, d_skip, ssm_out_gain,
           w_out, norm_mem, mem_in_gain, w_mem_q, w_mem_k, w_mem_v, mem_q_gain, mem_k_gain, w_mem_o,
           norm_ffn, w_router, b_router, w_gate_up, b_gate_up, w_down, b_down):
    bp, lp, d = x_prompt.shape
    bs, ls, _ = x_sample.shape
    depth = norm_mix.shape[0]
    n_buf = cache_win_k.shape[2]
    past_len = PAST_LEN
    n_mem = mem_prompt.shape[1]
    npr, nsm = bp * lp, bs * ls
    att_w = d // 2
    n_heads = att_w // ATT_HEAD_DIM
    ssm_w = d - att_w
    conv_ch = ssm_w + 2 * SSM_GROUPS * SSM_STATE
    tail = CONV_WIDTH - 1
    keep = min(max(w for w, _ in DILATIONS), lp)
    ls_pad = SUBLANES

    assert keep == lp
    xp_out, xs_out = x_prompt.reshape(npr, d).astype(F32), x_sample.reshape(nsm, d).astype(F32)
    outs = [[] for _ in range(10)]
    for i in range(depth):
        x = (xp_out, xs_out)
        q, k, v, z, xbc, dt_raw, kt_p, vt_p = _projections(*x, norm_mix[i], w_in[i], q_gain[i], k_gain[i],
                                                           lp, ls, past_len)
        smp = lambda a: a[npr:].reshape(bs, ls, a.shape[-1])
        pad_s = lambda a: jnp.pad(smp(a), ((0, 0), (0, ls_pad - ls), (0, 0)))
        pad_rows = lambda a: pad_s(a).reshape(bs * ls_pad, a.shape[-1])
        unpad = lambda a: a.reshape(bs, ls_pad, a.shape[-1])[:, :ls].reshape(nsm, a.shape[-1])
        att_p = _attn_prompt(q, k, v, bp, lp)
        heads = lambda a, b, l: a.reshape(b, l, n_heads, ATT_HEAD_DIM)
        k_new, v_new = heads(k[npr:], bs, ls), heads(v[npr:], bs, ls)
        cache_t = lambda c: jnp.transpose(c, (0, 2, 3, 1)).reshape(bs, att_w, n_buf)
        att_s = _attn_sample(smp(q), smp(k), smp(v), cache_t(cache_win_k[i]), cache_t(cache_win_v[i]))
        ssm_par = (conv_w[i], conv_b[i], dt_bias[i], a_log[i], d_skip[i], ssm_out_gain[i])
        y_p, st_p, cs_p = _ssd(xbc, dt_raw, z, jnp.zeros((bp, tail, conv_ch), F32),
                               jnp.zeros((bp, ssm_w // SSM_HEAD_DIM, SSM_HEAD_DIM, SSM_STATE), F32), *ssm_par,
                               bp, lp, lp)
        y_s, st_s, cs_s = _ssd(pad_rows(xbc), pad_rows(dt_raw), pad_rows(z), state_conv[i], state_ssm[i],
                               *ssm_par, bs, ls_pad, ls)
        x1, qm = _out_proj(x, (att_p, att_s.reshape(nsm, att_w)), (y_p, unpad(y_s)),
                           w_out[i], norm_mem[i], w_mem_q[i], mem_q_gain[i])
        mk_p, mv_p = _mem_kv(mem_prompt.reshape(bp * n_mem, d).astype(F32), mem_in_gain[i], w_mem_k[i],
                             w_mem_v[i], mem_k_gain[i])
        mw = mk_p.shape[-1]
        o_p = _mem_attn(qm, mk_p.reshape(bp, n_mem, mw), mv_p.reshape(bp, n_mem, mw), bp, lp, TOKEN_TILE)
        o_s = _mem_attn(pad_rows(qm), cache_mem_k[i], cache_mem_v[i], bs, ls_pad, ls_pad)
        x2, hf, top_idx, top_w = _post(x1, (o_p, unpad(o_s)), w_mem_o[i], norm_ffn[i], w_router[i], b_router[i])
        xp_out, xs_out = _moe(hf, x2, top_idx, top_w, w_gate_up[i], b_gate_up[i], w_down[i], b_down[i], npr)

        untr = lambda a: jnp.transpose(a.reshape(bp, n_heads, ATT_HEAD_DIM, lp), (0, 3, 1, 2))
        new = (untr(kt_p), untr(vt_p),
               k_new, v_new,
               cs_p, cs_s,
               st_p, st_s,
               mk_p.reshape(bp, n_mem, mw // MEM_HEAD_DIM, MEM_HEAD_DIM),
               mv_p.reshape(bp, n_mem, mw // MEM_HEAD_DIM, MEM_HEAD_DIM))
        for lst, val in zip(outs, new):
            lst.append(val)
    y_p = xp_out.reshape(bp, lp, d).astype(x_prompt.dtype)
    y_s = xs_out.reshape(bs, ls, d).astype(x_sample.dtype)
    return (y_p, y_s) + tuple(jnp.stack(o) for o in outs)
```

```python
import functools
import math

import numpy as np
import jax
import jax.numpy as jnp
from jax import lax
from jax.experimental import pallas as pl
from jax.experimental.pallas import tpu as pltpu

F32 = jnp.float32
BF16 = jnp.bfloat16

ATT_HEAD_DIM = 64
DILATIONS = ((128, 1), (512, 4), (2048, 16))
ATT_BLOCK = 128
ROPE_DIM = ATT_HEAD_DIM // 4
ROPE_THETA = 500000.0
PAST_LEN = 8192
SSM_HEAD_DIM = 64
SSM_GROUPS = 2
SSM_STATE = 128
CONV_WIDTH = 4
SSM_CHUNK = 128
MEM_HEAD_DIM = 128
TOP_K = 4
SWIGLU_LIMIT = 7.0
SWIGLU_ALPHA = 1.702
NORM_EPS = 1e-6

LANES = 128
SUBLANES = 8
VMEM_LIMIT = 56 * 1024 * 1024

TOKEN_TILE = 512
MOE_TILE = 256
COMBINE_TILE = 256
DISPATCH_TILE = 512
ATTN_UNROLL = 16
LOG2E = math.log2(math.e)
NEG = -1e30


def _cparams(sem):
    return pltpu.CompilerParams(dimension_semantics=sem, vmem_limit_bytes=VMEM_LIMIT)


def _rms(x, gain):
    return x * lax.rsqrt(jnp.mean(x * x, axis=-1, keepdims=True) + NORM_EPS) * gain


def _dot(a, b):
    return jnp.dot(a, b, preferred_element_type=F32)


def _dot_nt(a, b):
    return lax.dot_general(a, b, (((1,), (1,)), ((), ())), preferred_element_type=F32)


def _dot_tn(a, b):
    return lax.dot_general(a, b, (((0,), (0,)), ((), ())), preferred_element_type=F32)


def _dot_f32(a, b):
    return jnp.dot(a, b, preferred_element_type=F32, precision=lax.Precision.HIGHEST)


def _two_source(i, n_first, first_ref, second_ref):
    return lax.cond(i < n_first, lambda: first_ref[...], lambda: second_ref[...])


def _proj_kernel(xp_ref, xs_ref, g_ref, w_ref, qg_ref, kg_ref, seg_ref, cos_ref, s1_ref, s2_ref,
                 q_ref, k_ref, v_ref, z_ref, xbc_ref, dt_ref, kt_ref, vt_ref, *,
                 att_w, ssm_w, conv_ch, n_ptiles):
    i = pl.program_id(0)
    h = _rms(_two_source(i, n_ptiles, xp_ref, xs_ref), g_ref[...]).astype(BF16)
    seg = seg_ref[...]
    cos, s1, s2 = cos_ref[...], s1_ref[...], s2_ref[...]

    def head_norm_rope(t, gain):
        ms = _dot((t * t).astype(BF16), seg) * (1.0 / ATT_HEAD_DIM)
        tn = t * lax.rsqrt(ms + NORM_EPS) * gain
        half = ROPE_DIM // 2
        return (tn * cos + pltpu.roll(tn, half, 1) * s1
                + pltpu.roll(tn, att_w - half, 1) * s2)

    q = head_norm_rope(_dot(h, w_ref[:, 0:att_w]), qg_ref[...])
    q_ref[...] = q * (ATT_HEAD_DIM ** -0.5 * LOG2E)
    k = head_norm_rope(_dot(h, w_ref[:, att_w:2 * att_w]), kg_ref[...])
    v = _dot(h, w_ref[:, 2 * att_w:3 * att_w])
    k_ref[...] = k
    v_ref[...] = v

    @pl.when(i < n_ptiles)
    def _():
        kt_ref[0] = k.T
        vt_ref[0] = v.T

    o = 3 * att_w
    z_ref[...] = _dot(h, w_ref[:, o:o + ssm_w])
    o += ssm_w
    xbc_ref[...] = _dot(h, w_ref[:, o:o + conv_ch])
    o += conv_ch
    dt_ref[...] = _dot(h, w_ref[:, o:o + LANES])


def _rope_tables(pos, n_heads):
    half = ROPE_DIM // 2
    inv_freq = jnp.power(ROPE_THETA, -jnp.arange(half, dtype=F32) / half)
    ang = pos.astype(F32)[:, None] * inv_freq[None, :]
    cos, sin = jnp.cos(ang), jnp.sin(ang)
    n = pos.shape[0]
    rest = ATT_HEAD_DIM - ROPE_DIM
    c = jnp.concatenate([cos, cos, jnp.ones((n, rest), F32)], axis=-1)
    s1 = jnp.concatenate([jnp.zeros((n, half), F32), sin, jnp.zeros((n, rest), F32)], axis=-1)
    s2 = jnp.concatenate([-sin, jnp.zeros((n, half + rest), F32)], axis=-1)
    return tuple(jnp.tile(t, (1, n_heads)) for t in (c, s1, s2))


def _projections(x_p, x_s, norm_mix, w_in, q_gain, k_gain, seq, dec_seq, past_len):
    n_prompt, d = x_p.shape
    n = n_prompt + x_s.shape[0]
    d_half = d // 2
    att_w, ssm_w = d_half, d - d_half
    n_heads = att_w // ATT_HEAD_DIM
    ssm_heads = ssm_w // SSM_HEAD_DIM
    conv_ch = ssm_w + 2 * SSM_GROUPS * SSM_STATE
    tm = TOKEN_TILE
    assert n % tm == 0 and n_prompt % tm == 0 and seq % tm == 0 and (n - n_prompt) == tm
    assert tm % dec_seq == 0
    c0 = 3 * att_w + ssm_w
    w = jnp.concatenate([w_in[:, :c0], w_in[:, c0 + ssm_heads:],
                         w_in[:, c0:c0 + ssm_heads],
                         jnp.zeros((d, LANES - ssm_heads), w_in.dtype)], axis=1).astype(BF16)
    wn = w.shape[1]
    pos = jnp.concatenate([jnp.arange(seq, dtype=jnp.int32),
                           past_len + jnp.arange(tm, dtype=jnp.int32) % dec_seq])
    cos, s1, s2 = _rope_tables(pos, n_heads)
    tiles_per_seq = seq // tm
    n_prompt_tiles = n_prompt // tm
    head_id = np.arange(att_w) // ATT_HEAD_DIM
    seg = jnp.asarray(head_id[:, None] == head_id[None, :], BF16)

    def tab_map(i):
        return (jnp.where(i < n_prompt_tiles, i % tiles_per_seq, tiles_per_seq), 0)

    row = lambda i: (i, 0)
    fix = lambda i: (0, 0)
    first = lambda i: (jnp.minimum(i, n_prompt_tiles - 1), 0)

    def t_map(i):
        j = jnp.minimum(i, n_prompt_tiles - 1)
        return (j // tiles_per_seq, 0, j % tiles_per_seq)

    tab = pl.BlockSpec((tm, att_w), tab_map)
    tr = pl.BlockSpec((1, att_w, tm), t_map)
    kern = functools.partial(_proj_kernel, att_w=att_w, ssm_w=ssm_w, conv_ch=conv_ch, n_ptiles=n_prompt_tiles)
    return pl.pallas_call(
        kern,
        grid=(n // tm,),
        in_specs=[pl.BlockSpec((tm, d), first), pl.BlockSpec((tm, d), fix), pl.BlockSpec((1, d), fix),
                  pl.BlockSpec((d, wn), fix), pl.BlockSpec((1, att_w), fix),
                  pl.BlockSpec((1, att_w), fix), pl.BlockSpec((att_w, att_w), fix),
                  tab, tab, tab],
        out_specs=[pl.BlockSpec((tm, att_w), row), pl.BlockSpec((tm, att_w), row),
                   pl.BlockSpec((tm, att_w), row), pl.BlockSpec((tm, ssm_w), row),
                   pl.BlockSpec((tm, conv_ch), row), pl.BlockSpec((tm, LANES), row), tr, tr],
        out_shape=[jax.ShapeDtypeStruct((n, att_w), F32), jax.ShapeDtypeStruct((n, att_w), F32),
                   jax.ShapeDtypeStruct((n, att_w), F32), jax.ShapeDtypeStruct((n, ssm_w), F32),
                   jax.ShapeDtypeStruct((n, conv_ch), F32), jax.ShapeDtypeStruct((n, LANES), F32),
                   jax.ShapeDtypeStruct((n_prompt // seq, att_w, seq), F32),
                   jax.ShapeDtypeStruct((n_prompt // seq, att_w, seq), F32)],
        compiler_params=_cparams(("arbitrary",)),
        name="proj",
    )(x_p, x_s, norm_mix.reshape(1, d), w, jnp.tile(q_gain, n_heads).reshape(1, att_w),
      jnp.tile(k_gain, n_heads).reshape(1, att_w), seg, cos, s1, s2)


def _attn_prompt_kernel(q_ref, k_ref, v_ref, o_ref, num_ref, m_ref, den_ref, *, seq):
    blk = ATT_BLOCK
    lane = lax.broadcasted_iota(jnp.int32, (blk, LANES), 1)
    head0 = lane < ATT_HEAD_DIM
    qi = lax.broadcasted_iota(jnp.int32, (blk, 2 * blk), 0) + blk
    ki = lax.broadcasted_iota(jnp.int32, (blk, 2 * blk), 1)
    dist = qi - ki
    band = (dist >= 0) & (dist <= blk)
    bias_rest = jnp.where(band, 0.0, NEG)
    bias_first = jnp.where(band & (ki >= blk), 0.0, NEG)

    for di, (window, dil) in enumerate(DILATIONS):
        assert window // dil == blk
        nb = seq // dil // blk

        def body(i, carry, dil=dil, nb=nb, di=di):
            r = i // nb
            j = i % nb
            if dil > 1:
                start = r + dil * blk * j
                prev = r + dil * blk * jnp.maximum(j - 1, 0)
                rows = pl.ds(start, blk, stride=dil)
                prows = pl.ds(prev, blk, stride=dil)
            else:
                rows = pl.ds(pl.multiple_of(blk * j, blk), blk)
                prows = pl.ds(pl.multiple_of(blk * jnp.maximum(j - 1, 0), blk), blk)
            qb = q_ref[rows, :].astype(BF16)
            k2 = jnp.concatenate([k_ref[prows, :], k_ref[rows, :]], axis=0).astype(BF16)
            v2 = jnp.concatenate([v_ref[prows, :], v_ref[rows, :]], axis=0).astype(BF16)
            bias = jnp.where(j > 0, bias_rest, bias_first)
            zero = jnp.zeros_like(qb)
            q2 = jnp.concatenate([jnp.where(head0, qb, zero), jnp.where(head0, zero, qb)], axis=0)
            s = _dot_nt(q2, k2) + jnp.concatenate([bias, bias], axis=0)
            m = jnp.max(s, axis=-1, keepdims=True)
            p = jnp.exp2(s - m)
            den = jnp.sum(p, axis=-1, keepdims=True)
            pv = _dot(p.astype(BF16), v2)
            num_ref[di, rows, :] = jnp.where(head0, pv[:blk], pv[blk:])
            m_ref[di, rows, :] = jnp.where(head0, m[:blk], m[blk:])
            den_ref[di, rows, :] = jnp.where(head0, den[:blk], den[blk:])
            return carry

        lax.fori_loop(0, dil * nb, body, 0, unroll=ATTN_UNROLL)

    m_all = jnp.maximum(jnp.maximum(m_ref[0], m_ref[1]), m_ref[2])
    num = jnp.zeros((seq, LANES), F32)
    den = jnp.zeros((seq, LANES), F32)
    for di in range(len(DILATIONS)):
        a = jnp.exp2(m_ref[di] - m_all)
        num = num + a * num_ref[di]
        den = den + a * den_ref[di]
    o_ref[...] = (num / den).astype(o_ref.dtype)


def _attn_prompt(q, k, v, batch, seq):
    att_w = q.shape[1]
    pairs = att_w // LANES
    nd = len(DILATIONS)
    blk = pl.BlockSpec((seq, LANES), lambda b, h: (b, h))
    return pl.pallas_call(
        functools.partial(_attn_prompt_kernel, seq=seq),
        grid=(batch, pairs),
        in_specs=[blk, blk, blk],
        out_specs=blk,
        out_shape=jax.ShapeDtypeStruct((batch * seq, att_w), BF16),
        scratch_shapes=[pltpu.VMEM((nd, seq, LANES), F32), pltpu.VMEM((nd, seq, LANES), F32),
                        pltpu.VMEM((nd, seq, LANES), F32)],
        compiler_params=_cparams(("parallel", "parallel")),
        name="attn_prompt",
    )(q, k, v)


def _attn_sample_kernel(q_ref, kn_ref, vn_ref, kc_ref, vc_ref, cc_ref, cn_ref, o_ref, *,
                        dec_seq, n_heads):
    w = q_ref.shape[-1]
    rows = dec_seq * n_heads
    q = q_ref[0]
    qm = jnp.concatenate([jnp.broadcast_to(q[t:t + 1], (n_heads, w)) for t in range(dec_seq)], axis=0)
    lane_head = lax.broadcasted_iota(jnp.int32, (rows, w), 1) // ATT_HEAD_DIM
    row_head = lax.broadcasted_iota(jnp.int32, (rows, w), 0) % n_heads
    own = lane_head == row_head
    qm = jnp.where(own, qm, 0.0).astype(BF16)
    zpad = jnp.zeros((SUBLANES - dec_seq, w), F32)
    kn = jnp.concatenate([kn_ref[0], zpad], axis=0).astype(BF16)
    vn = jnp.concatenate([vn_ref[0], zpad], axis=0).astype(BF16)
    cc, cn = cc_ref[...], cn_ref[...]
    s_c = jnp.where(cc > 0, _dot(qm, kc_ref[0].astype(BF16)), -jnp.inf)
    s_n = jnp.where(cn > 0, _dot_nt(qm, kn), -jnp.inf)
    m = jnp.maximum(jnp.max(s_c, axis=-1, keepdims=True), jnp.max(s_n, axis=-1, keepdims=True))
    p_c = cc * jnp.exp2(s_c - m)
    p_n = cn * jnp.exp2(s_n - m)
    den = jnp.sum(p_c, axis=-1, keepdims=True) + jnp.sum(p_n, axis=-1, keepdims=True)
    o = _dot_nt(p_c.astype(BF16), vc_ref[0].astype(BF16)) + _dot(p_n.astype(BF16), vn)
    o = jnp.where(own, o / den, 0.0)
    o_ref[0] = jnp.sum(o.reshape(dec_seq, n_heads, w), axis=1).astype(o_ref.dtype)


def _attn_sample(q, k_new, v_new, k_cache_t, v_cache_t):
    b, t, w = q.shape
    n_buf = k_cache_t.shape[2]
    n_heads = w // ATT_HEAD_DIM
    assert n_heads == SUBLANES and t <= SUBLANES
    assert n_buf >= max(win for win, _ in DILATIONS)

    def count(dist):
        return sum(((dist >= 0) & (dist % dil == 0) & (dist <= win)).astype(np.float32)
                   for win, dil in DILATIONS)

    tq = np.repeat(np.arange(t), n_heads)[:, None]
    cc = count(n_buf + tq - np.arange(n_buf)[None, :])
    jn = np.arange(SUBLANES)[None, :]
    cn = np.where(jn < t, count(tq - jn), 0.0).astype(np.float32)
    rows = t * n_heads
    new = pl.BlockSpec((1, t, w), lambda i: (i, 0, 0))
    cache = pl.BlockSpec((1, w, n_buf), lambda i: (i, 0, 0))
    return pl.pallas_call(
        functools.partial(_attn_sample_kernel, dec_seq=t, n_heads=n_heads),
        grid=(b,),
        in_specs=[new, new, new, cache, cache,
                  pl.BlockSpec((rows, n_buf), lambda i: (0, 0)),
                  pl.BlockSpec((rows, SUBLANES), lambda i: (0, 0))],
        out_specs=new,
        out_shape=jax.ShapeDtypeStruct((b, t, w), BF16),
        compiler_params=_cparams(("parallel",)),
        name="attn_sample",
    )(q, k_new, v_new, k_cache_t, v_cache_t, jnp.asarray(cc), jnp.asarray(cn))


def _softplus(x):
    return jnp.maximum(x, 0.0) + jnp.log1p(jnp.exp(-jnp.abs(x)))


def _silu(x):
    return x * jax.nn.sigmoid(x)


def _ssd_kernel(xbc_ref, dt_ref, dtt_ref, z_ref, cp_ref, h0_ref, cw_ref, cb_ref, dtb_ref, dtbt_ref,
                al_ref, alt_ref, dsk_ref, og_ref, y_ref, hl_ref, ct_ref, xpad_ref, h_ref, *,
                q, valid, ssm_w, n_pairs):
    c = pl.program_id(1)
    tail = CONV_WIDTH - 1

    @pl.when(c == 0)
    def _():
        h_ref[...] = h0_ref[0]
        xpad_ref[0:SUBLANES, :] = jnp.zeros((SUBLANES, xpad_ref.shape[1]), F32)
        xpad_ref[SUBLANES - tail:SUBLANES, :] = cp_ref[0]

    x = xbc_ref[...]
    xpad_ref[SUBLANES:SUBLANES + q, :] = x
    conv = cb_ref[...]
    for j in range(CONV_WIDTH):
        o = SUBLANES - tail + j
        conv = conv + xpad_ref[o:o + q, :] * cw_ref[j:j + 1, :]

    @pl.when(c == pl.num_programs(1) - 1)
    def _():
        ct_ref[0] = xpad_ref[valid:valid + SUBLANES, :]

    xpad_ref[0:SUBLANES, :] = x[q - SUBLANES:q, :]
    u = _silu(conv)
    xs = u[:, :ssm_w]
    gw = SSM_STATE
    bm = u[:, ssm_w:ssm_w + SSM_GROUPS * gw].astype(BF16)
    cm = u[:, ssm_w + SSM_GROUPS * gw:].astype(BF16)

    dt = _softplus(dt_ref[...] + dtb_ref[...])
    dtt = _softplus(dtt_ref[0] + dtbt_ref[...])
    if valid < q:
        dt = jnp.where(lax.broadcasted_iota(jnp.int32, dt.shape, 0) < valid, dt, 0.0)
        dtt = jnp.where(lax.broadcasted_iota(jnp.int32, dtt.shape, 1) < valid, dtt, 0.0)
    ri = lax.broadcasted_iota(jnp.int32, (q, q), 0)
    ci = lax.broadcasted_iota(jnp.int32, (q, q), 1)
    causal = ci <= ri
    acum = _dot_f32(causal.astype(F32), dt * -jnp.exp(al_ref[...]))
    acumt = _dot_f32(dtt * -jnp.exp(alt_ref[...]), (ri <= ci).astype(F32))

    left = lax.broadcasted_iota(jnp.int32, (q, LANES), 1) < SSM_HEAD_DIM
    top = lax.broadcasted_iota(jnp.int32, (LANES, LANES), 0) < SSM_HEAD_DIM
    rep = 2 * n_pairs // SSM_GROUPS
    gmat = [_dot_nt(cm[:, g * gw:(g + 1) * gw], bm[:, g * gw:(g + 1) * gw]) for g in range(SSM_GROUPS)]
    ys = []
    for pr in range(n_pairs):
        ha, hb = 2 * pr, 2 * pr + 1
        g = ha // rep
        bg = bm[:, g * gw:(g + 1) * gw]
        cg = cm[:, g * gw:(g + 1) * gw]
        xpair = xs[:, pr * LANES:(pr + 1) * LANES]
        xd = xpair * jnp.where(left, dt[:, ha:ha + 1], dt[:, hb:hb + 1])
        xdb = xd.astype(BF16)
        yd = []
        for hh in (ha, hb):
            seg = acum[:, hh:hh + 1] - acumt[hh:hh + 1, :]
            decay = jnp.exp(jnp.where(causal, seg, -jnp.inf))
            yd.append(_dot((gmat[g] * decay).astype(BF16), xdb))
        ac = jnp.where(left, acum[:, ha:ha + 1], acum[:, hb:hb + 1])
        hprev = h_ref[pr]
        y_off = _dot_nt(cg, hprev.astype(BF16)) * jnp.exp(ac)
        to_end = jnp.exp(ac[q - 1:q, :] - ac)
        upd = _dot_tn((xd * to_end).astype(BF16), bg)
        cdec = jnp.where(top, jnp.exp(acum[q - 1:q, ha:ha + 1]), jnp.exp(acum[q - 1:q, hb:hb + 1]))
        h_ref[pr] = hprev * cdec + upd
        ys.append(jnp.where(left, yd[0], yd[1]) + y_off
                  + xpair * dsk_ref[:, pr * LANES:(pr + 1) * LANES])
    y = jnp.concatenate(ys, axis=1) * _silu(z_ref[...])
    y_ref[...] = _rms(y, og_ref[...]).astype(y_ref.dtype)

    @pl.when(c == pl.num_programs(1) - 1)
    def _():
        hl_ref[0] = h_ref[...]


def _ssd(xbc, dt_raw, z, conv_prev, h0, conv_w, conv_b, dt_bias, a_log, d_skip, out_gain, b, l, valid_len):
    conv_ch = xbc.shape[-1]
    ssm_w = z.shape[-1]
    n_heads = ssm_w // SSM_HEAD_DIM
    n_pairs = n_heads // 2
    assert 2 * SSM_HEAD_DIM == LANES and SSM_STATE == LANES and n_heads <= SUBLANES
    assert (n_heads // SSM_GROUPS) % 2 == 0
    q = SSM_CHUNK if l % SSM_CHUNK == 0 else l
    assert l % q == 0 and q % SUBLANES == 0 and (valid_len == l or q == l)
    nc = l // q
    dtt = jnp.swapaxes(dt_raw[:b * l, :n_heads].reshape(b, l, n_heads), 1, 2)
    pad_h = lambda a: jnp.pad(a.reshape(1, n_heads), ((0, 0), (0, LANES - n_heads)))
    per_lane = lambda a: jnp.repeat(a, SSM_HEAD_DIM).reshape(1, ssm_w)
    h0p = h0.reshape(b, n_pairs, LANES, SSM_STATE).astype(F32)
    tail = CONV_WIDTH - 1
    tok = lambda w: pl.BlockSpec((q, w), lambda i, j: (i * nc + j, 0))
    fix2 = lambda r, w: pl.BlockSpec((r, w), lambda i, j: (0, 0))
    st = pl.BlockSpec((1, n_pairs, LANES, SSM_STATE), lambda i, j: (i, 0, 0, 0))
    kern = functools.partial(_ssd_kernel, q=q, valid=valid_len if q == l else q, ssm_w=ssm_w, n_pairs=n_pairs)
    y, h_last, conv_tail = pl.pallas_call(
        kern,
        grid=(b, nc),
        in_specs=[tok(conv_ch), tok(LANES), pl.BlockSpec((1, n_heads, q), lambda i, j: (i, 0, j)),
                  tok(ssm_w), pl.BlockSpec((1, tail, conv_ch), lambda i, j: (i, 0, 0)), st,
                  fix2(CONV_WIDTH, conv_ch), fix2(1, conv_ch), fix2(1, LANES), fix2(n_heads, 1),
                  fix2(1, LANES), fix2(n_heads, 1), fix2(1, ssm_w), fix2(1, ssm_w)],
        out_specs=[tok(ssm_w), st, pl.BlockSpec((1, SUBLANES, conv_ch), lambda i, j: (i, 0, 0))],
        out_shape=[jax.ShapeDtypeStruct((b * l, ssm_w), BF16),
                   jax.ShapeDtypeStruct((b, n_pairs, LANES, SSM_STATE), F32),
                   jax.ShapeDtypeStruct((b, SUBLANES, conv_ch), F32)],
        scratch_shapes=[pltpu.VMEM((q + SUBLANES, conv_ch), F32),
                        pltpu.VMEM((n_pairs, LANES, SSM_STATE), F32)],
        compiler_params=_cparams(("parallel", "arbitrary")),
        name="ssd",
    )(xbc, dt_raw, dtt, z, conv_prev.astype(F32), h0p, conv_w, conv_b.reshape(1, conv_ch),
      pad_h(dt_bias), dt_bias.reshape(n_heads, 1), pad_h(a_log), a_log.reshape(n_heads, 1),
      per_lane(d_skip), out_gain.reshape(1, ssm_w))
    return y, h_last.reshape(b, n_heads, SSM_HEAD_DIM, SSM_STATE), conv_tail[:, SUBLANES - tail:]


def _head_rms_store(dst_ref, t, gain, scale=None):
    for hd in range(t.shape[1] // MEM_HEAD_DIM):
        sl = slice(hd * MEM_HEAD_DIM, (hd + 1) * MEM_HEAD_DIM)
        r = _rms(t[:, sl], gain)
        if scale is not None:
            r = r * scale
        dst_ref[:, sl] = r.astype(dst_ref.dtype)


def _mem_kv_kernel(m_ref, g_ref, wk_ref, wv_ref, kg_ref, k_ref, v_ref):
    h = _rms(m_ref[...], g_ref[...]).astype(BF16)
    _head_rms_store(k_ref, _dot(h, wk_ref[...]), kg_ref[...])
    v_ref[...] = _dot(h, wv_ref[...])


def _mem_kv(mem, in_gain, w_k, w_v, k_gain):
    n, d = mem.shape
    mw = w_k.shape[1]
    tm = TOKEN_TILE
    assert n % tm == 0 and MEM_HEAD_DIM == LANES
    row = lambda i: (i, 0)
    fix = lambda i: (0, 0)
    return pl.pallas_call(
        _mem_kv_kernel,
        grid=(n // tm,),
        in_specs=[pl.BlockSpec((tm, d), row), pl.BlockSpec((1, d), fix), pl.BlockSpec((d, mw), fix),
                  pl.BlockSpec((d, mw), fix), pl.BlockSpec((1, MEM_HEAD_DIM), fix)],
        out_specs=[pl.BlockSpec((tm, mw), row), pl.BlockSpec((tm, mw), row)],
        out_shape=[jax.ShapeDtypeStruct((n, mw), F32), jax.ShapeDtypeStruct((n, mw), F32)],
        compiler_params=_cparams(("parallel",)),
        name="mem_kv",
    )(mem, in_gain.reshape(1, d), w_k.astype(BF16), w_v.astype(BF16), k_gain.reshape(1, MEM_HEAD_DIM))


def _out_proj_kernel(xp_ref, xs_ref, ap_ref, as_ref, yp_ref, ys_ref, wo_ref, g_ref, wq_ref, qg_ref,
                     x1_ref, q_ref, *, n_ptiles):
    i = pl.program_id(0)
    att_w = ap_ref.shape[1]
    x1 = (_two_source(i, n_ptiles, xp_ref, xs_ref)
          + _dot(_two_source(i, n_ptiles, ap_ref, as_ref), wo_ref[0:att_w, :])
          + _dot(_two_source(i, n_ptiles, yp_ref, ys_ref), wo_ref[att_w:, :]))
    x1_ref[...] = x1
    h = _rms(x1, g_ref[...]).astype(BF16)
    _head_rms_store(q_ref, _dot(h, wq_ref[...]), qg_ref[...], MEM_HEAD_DIM ** -0.5)


def _out_proj(x, att, yssm, w_out, norm_mem, w_mem_q, mem_q_gain):
    tm = TOKEN_TILE
    n_prompt, d = x[0].shape
    assert all(a[1].shape[0] == tm and a[0].shape[0] == n_prompt for a in (x, att, yssm))
    n = n_prompt + tm
    n_ptiles = n_prompt // tm
    att_w, ssm_w = att[0].shape[1], yssm[0].shape[1]
    mw = w_mem_q.shape[1]
    row = lambda i: (i, 0)
    fix = lambda i: (0, 0)
    first = lambda i: (jnp.minimum(i, n_ptiles - 1), 0)
    pair = lambda w: [pl.BlockSpec((tm, w), first), pl.BlockSpec((tm, w), fix)]
    return pl.pallas_call(
        functools.partial(_out_proj_kernel, n_ptiles=n_ptiles),
        grid=(n // tm,),
        in_specs=pair(d) + pair(att_w) + pair(ssm_w)
        + [pl.BlockSpec((att_w + ssm_w, d), fix), pl.BlockSpec((1, d), fix),
           pl.BlockSpec((d, mw), fix), pl.BlockSpec((1, MEM_HEAD_DIM), fix)],
        out_specs=[pl.BlockSpec((tm, d), row), pl.BlockSpec((tm, mw), row)],
        out_shape=[jax.ShapeDtypeStruct((n, d), F32), jax.ShapeDtypeStruct((n, mw), BF16)],
        compiler_params=_cparams(("parallel",)),
        name="out_proj",
    )(*x, *att, *yssm, w_out.astype(BF16), norm_mem.reshape(1, d), w_mem_q.astype(BF16),
      mem_q_gain.reshape(1, MEM_HEAD_DIM))


def _mem_attn_kernel(q_ref, k_ref, v_ref, o_ref):
    for hd in range(q_ref.shape[-1] // MEM_HEAD_DIM):
        sl = slice(hd * MEM_HEAD_DIM, (hd + 1) * MEM_HEAD_DIM)
        head = (lambda r: r[0, :, hd, :]) if len(k_ref.shape) == 4 else (lambda r: r[0, :, sl])
        s = _dot_nt(q_ref[:, sl], head(k_ref).astype(BF16))
        p = jnp.exp(s - jnp.max(s, axis=-1, keepdims=True))
        den = jnp.sum(p, axis=-1, keepdims=True)
        o_ref[:, sl] = (_dot(p.astype(BF16), head(v_ref).astype(BF16)) / den).astype(o_ref.dtype)


def _mem_attn(q, mem_k, mem_v, b, l, tq):
    w = q.shape[-1]
    n_mem = mem_k.shape[1]
    assert l % tq == 0
    qs = pl.BlockSpec((tq, w), lambda i, j: (i * (l // tq) + j, 0))
    ms = pl.BlockSpec((1,) + mem_k.shape[1:], lambda i, j: (i,) + (0,) * (mem_k.ndim - 1))
    return pl.pallas_call(
        _mem_attn_kernel,
        grid=(b, l // tq),
        in_specs=[qs, ms, ms],
        out_specs=qs,
        out_shape=jax.ShapeDtypeStruct((b * l, w), BF16),
        compiler_params=_cparams(("parallel", "parallel")),
        name="mem_attn",
    )(q, mem_k, mem_v)


def _store_row_tiles(dst_ref, x):
    rows = x.shape[0]
    for j in range(x.shape[1] // LANES):
        dst_ref[pl.ds(j, rows, stride=SUBLANES), :] = x[:, j * LANES:(j + 1) * LANES]


def _load_row_tiles(src_ref, first, rows):
    return jnp.concatenate([src_ref[pl.ds(first * SUBLANES + j, rows, stride=SUBLANES), :]
                            for j in range(SUBLANES)], axis=1)


def _post_kernel(x1_ref, op_ref, os_ref, wo_ref, g_ref, wrh_ref, wrl_ref, br_ref,
                 x2_ref, hf_ref, idx_ref, w_ref, *, n_ptiles):
    x2 = x1_ref[...] + _dot(_two_source(pl.program_id(0), n_ptiles, op_ref, os_ref), wo_ref[...])
    x2_ref[...] = x2
    hf = _rms(x2, g_ref[...])
    _store_row_tiles(hf_ref, hf)
    hi = hf.astype(BF16)
    lo = (hf - hi.astype(F32)).astype(BF16)
    logits = (_dot(hi, wrh_ref[...]) + _dot(lo, wrh_ref[...]) + _dot(hi, wrl_ref[...])) + br_ref[...]
    lane = lax.broadcasted_iota(jnp.int32, logits.shape, 1)
    vals, idxs = [], []
    for _ in range(TOP_K):
        m = jnp.max(logits, axis=-1, keepdims=True)
        ix = jnp.min(jnp.where(logits == m, lane, LANES), axis=-1, keepdims=True)
        vals.append(m)
        idxs.append(ix)
        logits = jnp.where(lane == ix, -jnp.inf, logits)
    es = [jnp.exp(v - vals[0]) for v in vals]
    tot = es[0]
    for e in es[1:]:
        tot = tot + e
    wout = jnp.zeros(logits.shape, F32)
    iout = jnp.zeros(logits.shape, jnp.int32)
    for kk in range(TOP_K):
        wout = jnp.where(lane == kk, es[kk] / tot, wout)
        iout = jnp.where(lane == kk, idxs[kk], iout)
    w_ref[...] = wout
    idx_ref[...] = iout


def _post(x1, o, w_mem_o, norm_ffn, w_router, b_router):
    n, d = x1.shape
    mw = o[0].shape[1]
    n_exp = w_router.shape[1]
    assert n_exp <= LANES
    tm = TOKEN_TILE
    assert o[1].shape[0] == tm and o[0].shape[0] + tm == n
    n_ptiles = o[0].shape[0] // tm
    wr = jnp.pad(w_router, ((0, 0), (0, LANES - n_exp)))
    wrh = wr.astype(BF16)
    wrl = (wr - wrh.astype(F32)).astype(BF16)
    br = jnp.concatenate([b_router.astype(F32), jnp.full((LANES - n_exp,), NEG, F32)]).reshape(1, LANES)
    row = lambda i: (i, 0)
    fix = lambda i: (0, 0)
    return pl.pallas_call(
        functools.partial(_post_kernel, n_ptiles=n_ptiles),
        grid=(n // tm,),
        in_specs=[pl.BlockSpec((tm, d), row),
                  pl.BlockSpec((tm, mw), lambda i: (jnp.minimum(i, n_ptiles - 1), 0)),
                  pl.BlockSpec((tm, mw), fix), pl.BlockSpec((mw, d), fix),
                  pl.BlockSpec((1, d), fix), pl.BlockSpec((d, LANES), fix), pl.BlockSpec((d, LANES), fix),
                  pl.BlockSpec((1, LANES), fix)],
        out_specs=[pl.BlockSpec((tm, d), row), pl.BlockSpec((tm * SUBLANES, LANES), row),
                   pl.BlockSpec((tm, LANES), row), pl.BlockSpec((tm, LANES), row)],
        out_shape=[jax.ShapeDtypeStruct((n, d), F32), jax.ShapeDtypeStruct((n * SUBLANES, LANES), F32),
                   jax.ShapeDtypeStruct((n, LANES), jnp.int32), jax.ShapeDtypeStruct((n, LANES), F32)],
        compiler_params=_cparams(("parallel",)),
        name="post",
    )(x1, *o, w_mem_o.astype(BF16), norm_ffn.reshape(1, d), wrh, wrl, br)


GU_CHUNK = 2 * LANES


def _regroup_kernel(w_ref, p_ref, o_ref):
    for c in range(w_ref.shape[-1] // GU_CHUNK):
        sl = slice(c * GU_CHUNK, (c + 1) * GU_CHUNK)
        o_ref[0, :, sl] = _dot(w_ref[0, :, sl].astype(BF16), p_ref[...]).astype(BF16)


def _regroup_gate_up(w_gate_up):
    n_exp, d, ff2 = w_gate_up.shape
    wblk = 2 * GU_CHUNK
    assert ff2 % wblk == 0
    src = np.arange(GU_CHUNK)
    dst = np.where(src % 2 == 0, src // 2, LANES + src // 2)
    perm = np.zeros((GU_CHUNK, GU_CHUNK), np.float32)
    perm[src, dst] = 1.0
    blk = pl.BlockSpec((1, d, wblk), lambda e, j: (e, 0, j))
    return pl.pallas_call(
        _regroup_kernel,
        grid=(n_exp, ff2 // wblk),
        in_specs=[blk, pl.BlockSpec((GU_CHUNK, GU_CHUNK), lambda e, j: (0, 0))],
        out_specs=blk,
        out_shape=jax.ShapeDtypeStruct((n_exp, d, ff2), BF16),
        compiler_params=_cparams(("parallel", "parallel")),
        name="regroup_gate_up",
    )(w_gate_up, jnp.asarray(perm, BF16))


def _rank_kernel(idx_ref, rank_ref, cnt_ref, carry_ref):
    @pl.when(pl.program_id(0) == 0)
    def _():
        carry_ref[...] = jnp.zeros(carry_ref.shape, F32)

    idx = idx_ref[...]
    tm = idx.shape[0]
    lane = lax.broadcasted_iota(jnp.int32, idx.shape, 1)
    hot = [(lane == idx[:, kk:kk + 1]).astype(F32) for kk in range(TOP_K)]
    tot = hot[0]
    for h in hot[1:]:
        tot = tot + h
    earlier = (lax.broadcasted_iota(jnp.int32, (tm, tm), 1)
               < lax.broadcasted_iota(jnp.int32, (tm, tm), 0)).astype(BF16)
    base = carry_ref[...] + _dot(earlier, tot.astype(BF16))
    out = jnp.zeros(idx.shape, jnp.int32)
    for kk in range(TOP_K):
        r = jnp.sum(hot[kk] * base, axis=-1, keepdims=True)
        out = jnp.where(lane == kk, r.astype(jnp.int32), out)
        base = base + hot[kk]
    rank_ref[...] = out
    carry_ref[...] = carry_ref[...] + jnp.sum(tot, axis=0, keepdims=True)
    cnt_ref[...] = carry_ref[...]


def _rank(top_idx):
    n, w = top_idx.shape
    tm = TOKEN_TILE
    return pl.pallas_call(
        _rank_kernel,
        grid=(n // tm,),
        in_specs=[pl.BlockSpec((tm, w), lambda i: (i, 0))],
        out_specs=[pl.BlockSpec((tm, w), lambda i: (i, 0)), pl.BlockSpec((1, w), lambda i: (0, 0))],
        out_shape=[jax.ShapeDtypeStruct((n, w), jnp.int32), jax.ShapeDtypeStruct((1, w), F32)],
        scratch_shapes=[pltpu.VMEM((1, w), F32)],
        compiler_params=_cparams(("arbitrary",)),
        name="rank",
    )(top_idx)


def _dispatch_kernel(seg_start_ref, seg_len_ref, dest_ref, hf_ref, xb_out, zrow, sem, zsem):
    tc = hf_ref.shape[0] // SUBLANES

    def slot(ref, row8):
        return ref.at[pl.ds(pl.multiple_of(row8, SUBLANES), SUBLANES), :]

    @pl.when(pl.program_id(0) == 0)
    def _():
        zrow[...] = jnp.zeros(zrow.shape, zrow.dtype)

        def zero_copy(row):
            return pltpu.make_async_copy(zrow, slot(xb_out, row * SUBLANES), zsem)

        def per_segment(op):
            def seg(s, carry):
                base = seg_start_ref[s]

                def row(r, c):
                    op(zero_copy(base + r))
                    return c

                return lax.fori_loop(0, seg_len_ref[s], row, carry)

            lax.fori_loop(0, seg_start_ref.shape[0], seg, 0)

        per_segment(lambda cp: cp.start())
        per_segment(lambda cp: cp.wait())

    for r in range(TOP_K * tc):
        pltpu.make_async_copy(slot(hf_ref, (r % tc) * SUBLANES), slot(xb_out, dest_ref[0, 0, r]),
                              sem).start(priority=r % 2)
    for _ in range(TOP_K):
        pltpu.make_async_copy(hf_ref, xb_out.at[pl.ds(0, tc * SUBLANES), :], sem).wait()


def _experts_kernel(be_ref, nact_ref, x_ref, wgu_ref, bgu_ref, wd_ref, bd_ref, y_ref):
    i = pl.program_id(0)

    tm = x_ref.shape[0] // SUBLANES

    @pl.when(i < nact_ref[0])
    def _():
        gu = _dot(_load_row_tiles(x_ref, 0, tm).astype(BF16), wgu_ref[0]) + bgu_ref[0]
        acts = []
        for c in range(gu.shape[1] // GU_CHUNK):
            g = jnp.minimum(gu[:, c * GU_CHUNK:c * GU_CHUNK + LANES], SWIGLU_LIMIT)
            u = jnp.clip(gu[:, c * GU_CHUNK + LANES:(c + 1) * GU_CHUNK], -SWIGLU_LIMIT, SWIGLU_LIMIT)
            acts.append(((u + 1.0) * (g * jax.nn.sigmoid(SWIGLU_ALPHA * g))).astype(BF16))
        _store_row_tiles(y_ref, _dot(jnp.concatenate(acts, axis=1), wd_ref[0]) + bd_ref[0])

    @pl.when(i >= nact_ref[0])
    def _():
        y_ref[...] = jnp.zeros(y_ref.shape, y_ref.dtype)


def _combine_kernel(dest_ref, x2_ref, w_ref, y_hbm, op_ref, os_ref, buf, sems, *, n_tiles, n_ptiles):
    i = pl.program_id(0)
    tc = x2_ref.shape[0]

    def start_gather(slot):
        for r in range(TOP_K * tc):
            src = y_hbm.at[pl.ds(pl.multiple_of(dest_ref[0, 0, r], SUBLANES), SUBLANES), :]
            pltpu.make_async_copy(src, buf.at[slot, pl.ds(r * SUBLANES, SUBLANES), :],
                                  sems.at[slot]).start(priority=r % 2)

    def finish(slot):
        rows = buf.at[slot]
        pltpu.make_async_copy(y_hbm.at[pl.ds(0, TOP_K * tc * SUBLANES), :], rows, sems.at[slot]).wait()
        acc = x2_ref[...]
        w = w_ref[...]
        for kk in range(TOP_K):
            acc = acc + w[:, kk:kk + 1] * _load_row_tiles(rows, kk * tc, tc)

        @pl.when(i - 1 < n_ptiles)
        def _():
            op_ref[...] = acc

        @pl.when(i - 1 >= n_ptiles)
        def _():
            os_ref[...] = acc

    for slot in range(2):
        pl.when((i < n_tiles) & (i % 2 == slot))(functools.partial(start_gather, slot))
    for slot in range(2):
        pl.when((i > 0) & ((i + 1) % 2 == slot))(functools.partial(finish, slot))


def _moe(hf, x2, top_idx, top_w, w_gate_up, b_gate_up, w_down, b_down, n_prompt):
    n_tok, d = x2.shape
    n_exp, _, ff2 = w_gate_up.shape
    ff = ff2 // 2
    tm, tc = MOE_TILE, COMBINE_TILE
    row8 = SUBLANES
    assert n_tok % tc == 0 and d == row8 * LANES and hf.shape == (n_tok * row8, LANES)
    n_assign = n_tok * TOP_K
    n_blocks = -(-(n_assign + n_exp * (tm - 1)) // tm)
    n_rows = n_blocks * tm

    rank, cnt = _rank(top_idx)
    counts = cnt[0, :n_exp].astype(jnp.int32)
    padded = (counts + tm - 1) // tm * tm
    pend = jnp.cumsum(padded)
    pstart = pend - padded
    choice = top_idx[:, :TOP_K]
    first = jnp.sum(jnp.where(choice[:, :, None] == jnp.arange(n_exp, dtype=jnp.int32), pstart, 0), axis=-1)
    dest = ((first + rank[:, :TOP_K]) * row8).astype(jnp.int32)
    tiles_of = lambda t: dest.reshape(n_tok // t, t, TOP_K).transpose(0, 2, 1).reshape(n_tok // t, 1, TOP_K * t)
    dest_tiles = tiles_of(tc)
    td = DISPATCH_TILE
    assert n_tok % td == 0
    block_e = jnp.minimum(jnp.sum(jnp.arange(n_blocks, dtype=jnp.int32)[None, :] * tm >= pend[:, None], axis=0),
                          n_exp - 1).astype(jnp.int32)
    n_active = (pend[-1:] // tm).astype(jnp.int32)
    seg_start = jnp.concatenate([pstart + counts, pend[-1:]]).astype(jnp.int32)
    seg_len = jnp.concatenate([padded - counts, n_rows - pend[-1:]]).astype(jnp.int32)

    xb = pl.pallas_call(
        _dispatch_kernel,
        grid_spec=pltpu.PrefetchScalarGridSpec(
            num_scalar_prefetch=2,
            grid=(n_tok // td,),
            in_specs=[pl.BlockSpec((1, 1, TOP_K * td), lambda i, ss, sl: (i, 0, 0), memory_space=pltpu.SMEM),
                      pl.BlockSpec((td * row8, LANES), lambda i, ss, sl: (i, 0))],
            out_specs=pl.BlockSpec(memory_space=pl.ANY),
            scratch_shapes=[pltpu.VMEM((row8, LANES), F32), pltpu.SemaphoreType.DMA(()),
                            pltpu.SemaphoreType.DMA(())]),
        out_shape=jax.ShapeDtypeStruct((n_rows * row8, LANES), F32),
        compiler_params=_cparams(("arbitrary",)),
        name="dispatch",
    )(seg_start, seg_len, tiles_of(td), hf)

    wgu = _regroup_gate_up(w_gate_up)
    bgu = b_gate_up.reshape(n_exp, ff2 // GU_CHUNK, LANES, 2).transpose(0, 1, 3, 2).reshape(n_exp, 1, ff2)
    by_e = lambda i, be, na: (be[i], 0, 0)
    yb = pl.pallas_call(
        _experts_kernel,
        grid_spec=pltpu.PrefetchScalarGridSpec(
            num_scalar_prefetch=2,
            grid=(n_blocks,),
            in_specs=[pl.BlockSpec((tm * row8, LANES), lambda i, be, na: (jnp.minimum(i, na[0] - 1), 0)),
                      pl.BlockSpec((1, d, ff2), by_e), pl.BlockSpec((1, 1, ff2), by_e),
                      pl.BlockSpec((1, ff, d), by_e), pl.BlockSpec((1, 1, d), by_e)],
            out_specs=pl.BlockSpec((tm * row8, LANES), lambda i, be, na: (i, 0))),
        out_shape=jax.ShapeDtypeStruct((n_rows * row8, LANES), F32),
        compiler_params=_cparams(("arbitrary",)),
        name="experts",
    )(block_e, n_active, xb, wgu, bgu, w_down.astype(BF16), b_down.reshape(n_exp, 1, d))

    assert n_prompt % tc == 0 and 0 < n_prompt < n_tok
    n_ptiles = n_prompt // tc
    n_tiles = n_tok // tc
    prev = lambda i: jnp.maximum(i - 1, 0)
    return pl.pallas_call(
        functools.partial(_combine_kernel, n_tiles=n_tiles, n_ptiles=n_ptiles),
        grid=(n_tiles + 1,),
        in_specs=[pl.BlockSpec((1, 1, TOP_K * tc), lambda i: (jnp.minimum(i, n_tiles - 1), 0, 0),
                               memory_space=pltpu.SMEM),
                  pl.BlockSpec((tc, d), lambda i: (prev(i), 0)),
                  pl.BlockSpec((tc, LANES), lambda i: (prev(i), 0)),
                  pl.BlockSpec(memory_space=pl.ANY)],
        out_specs=[pl.BlockSpec((tc, d), lambda i: (jnp.minimum(prev(i), n_ptiles - 1), 0)),
                   pl.BlockSpec((tc, d), lambda i: (jnp.maximum(prev(i) - n_ptiles, 0), 0))],
        out_shape=[jax.ShapeDtypeStruct((n_prompt, d), F32),
                   jax.ShapeDtypeStruct((n_tok - n_prompt, d), F32)],
        scratch_shapes=[pltpu.VMEM((2, TOP_K * tc * row8, LANES), F32), pltpu.SemaphoreType.DMA((2,))],
        compiler_params=_cparams(("arbitrary",)),
        name="combine",
    )(dest_tiles, x2, top_w, yb)


def kernel(x_prompt, x_sample, cache_win_k, cache_win_v, state_conv, state_ssm, cache_mem_k, cache_mem_v,
           mem_prompt, norm_mix, w_in, q_gain, k_gain, conv_w, conv_b, dt_bias, a_log, d_skip, ssm_out_gain,
           w_out, norm_mem, mem_in_gain, w_mem_q, w_mem_k, w_mem_v, mem_q_gain, mem_k_gain, w_mem_o,
           norm_ffn, w_router, b_router, w_gate_up, b_gate_up, w_down, b_down):
    bp, lp, d = x_prompt.shape
    bs, ls, _ = x_sample.shape
    depth = norm_mix.shape[0]
    n_buf = cache_win_k.shape[2]
    past_len = PAST_LEN
    n_mem = mem_prompt.shape[1]
    npr, nsm = bp * lp, bs * ls
    att_w = d // 2
    n_heads = att_w // ATT_HEAD_DIM
    ssm_w = d - att_w
    conv_ch = ssm_w + 2 * SSM_GROUPS * SSM_STATE
    tail = CONV_WIDTH - 1
    keep = min(max(w for w, _ in DILATIONS), lp)
    ls_pad = SUBLANES

    assert keep == lp
    xp_out, xs_out = x_prompt.reshape(npr, d).astype(F32), x_sample.reshape(nsm, d).astype(F32)
    outs = [[] for _ in range(10)]
    for i in range(depth):
        x = (xp_out, xs_out)
        q, k, v, z, xbc, dt_raw, kt_p, vt_p = _projections(*x, norm_mix[i], w_in[i], q_gain[i], k_gain[i],
                                                           lp, ls, past_len)
        smp = lambda a: a[npr:].reshape(bs, ls, a.shape[-1])
        pad_s = lambda a: jnp.pad(smp(a), ((0, 0), (0, ls_pad - ls), (0, 0)))
        pad_rows = lambda a: pad_s(a).reshape(bs * ls_pad, a.shape[-1])
        unpad = lambda a: a.reshape(bs, ls_pad, a.shape[-1])[:, :ls].reshape(nsm, a.shape[-1])
        att_p = _attn_prompt(q, k, v, bp, lp)
        heads = lambda a, b, l: a.reshape(b, l, n_heads, ATT_HEAD_DIM)
        k_new, v_new = heads(k[npr:], bs, ls), heads(v[npr:], bs, ls)
        cache_t = lambda c: jnp.transpose(c, (0, 2, 3, 1)).reshape(bs, att_w, n_buf)
        att_s = _attn_sample(smp(q), smp(k), smp(v), cache_t(cache_win_k[i]), cache_t(cache_win_v[i]))
        ssm_par = (conv_w[i], conv_b[i], dt_bias[i], a_log[i], d_skip[i], ssm_out_gain[i])
        y_p, st_p, cs_p = _ssd(xbc, dt_raw, z, jnp.zeros((bp, tail, conv_ch), F32),
                               jnp.zeros((bp, ssm_w // SSM_HEAD_DIM, SSM_HEAD_DIM, SSM_STATE), F32), *ssm_par,
                               bp, lp, lp)
        y_s, st_s, cs_s = _ssd(pad_rows(xbc), pad_rows(dt_raw), pad_rows(z), state_conv[i], state_ssm[i],
                               *ssm_par, bs, ls_pad, ls)
        x1, qm = _out_proj(x, (att_p, att_s.reshape(nsm, att_w)), (y_p, unpad(y_s)),
                           w_out[i], norm_mem[i], w_mem_q[i], mem_q_gain[i])
        mk_p, mv_p = _mem_kv(mem_prompt.reshape(bp * n_mem, d).astype(F32), mem_in_gain[i], w_mem_k[i],
                             w_mem_v[i], mem_k_gain[i])
        mw = mk_p.shape[-1]
        o_p = _mem_attn(qm, mk_p.reshape(bp, n_mem, mw), mv_p.reshape(bp, n_mem, mw), bp, lp, TOKEN_TILE)
        o_s = _mem_attn(pad_rows(qm), cache_mem_k[i], cache_mem_v[i], bs, ls_pad, ls_pad)
        x2, hf, top_idx, top_w = _post(x1, (o_p, unpad(o_s)), w_mem_o[i], norm_ffn[i], w_router[i], b_router[i])
        xp_out, xs_out = _moe(hf, x2, top_idx, top_w, w_gate_up[i], b_gate_up[i], w_down[i], b_down[i], npr)

        untr = lambda a: jnp.transpose(a.reshape(bp, n_heads, ATT_HEAD_DIM, lp), (0, 3, 1, 2))
        new = (untr(kt_p), untr(vt_p),
               k_new, v_new,
               cs_p, cs_s,
               st_p, st_s,
               mk_p.reshape(bp, n_mem, mw // MEM_HEAD_DIM, MEM_HEAD_DIM),
               mv_p.reshape(bp, n_mem, mw // MEM_HEAD_DIM, MEM_HEAD_DIM))
        for lst, val in zip(outs, new):
            lst.append(val)
    y_p = xp_out.reshape(bp, lp, d).astype(x_prompt.dtype)
    y_s = xs_out.reshape(bs, ls, d).astype(x_sample.dtype)
    return (y_p, y_s) + tuple(jnp.stack(o) for o in outs)
```

```python
import functools
import math

import numpy as np
import jax
import jax.numpy as jnp
from jax import lax
from jax.experimental import pallas as pl
from jax.experimental.pallas import tpu as pltpu

F32 = jnp.float32
BF16 = jnp.bfloat16

ATT_HEAD_DIM = 64
DILATIONS = ((128, 1), (512, 4), (2048, 16))
ATT_BLOCK = 128
ROPE_DIM = ATT_HEAD_DIM // 4
ROPE_THETA = 500000.0
PAST_LEN = 8192
SSM_HEAD_DIM = 64
SSM_GROUPS = 2
SSM_STATE = 128
CONV_WIDTH = 4
SSM_CHUNK = 128
MEM_HEAD_DIM = 128
TOP_K = 4
SWIGLU_LIMIT = 7.0
SWIGLU_ALPHA = 1.702
NORM_EPS = 1e-6

LANES = 128
SUBLANES = 8
VMEM_LIMIT = 56 * 1024 * 1024

TOKEN_TILE = 512
MOE_TILE = 256
COMBINE_TILE = 256
DISPATCH_TILE = 512
ATTN_UNROLL = 16
LOG2E = math.log2(math.e)
NEG = -1e30


def _cparams(sem):
    return pltpu.CompilerParams(dimension_semantics=sem, vmem_limit_bytes=VMEM_LIMIT)


def _rms(x, gain):
    return x * lax.rsqrt(jnp.mean(x * x, axis=-1, keepdims=True) + NORM_EPS) * gain


def _dot(a, b):
    return jnp.dot(a, b, preferred_element_type=F32)


def _dot_nt(a, b):
    return lax.dot_general(a, b, (((1,), (1,)), ((), ())), preferred_element_type=F32)


def _dot_tn(a, b):
    return lax.dot_general(a, b, (((0,), (0,)), ((), ())), preferred_element_type=F32)


def _dot_f32(a, b):
    return jnp.dot(a, b, preferred_element_type=F32, precision=lax.Precision.HIGHEST)


def _two_source(i, n_first, first_ref, second_ref):
    return lax.cond(i < n_first, lambda: first_ref[...], lambda: second_ref[...])


def _proj_kernel(xp_ref, xs_ref, g_ref, w_ref, qg_ref, kg_ref, seg_ref, cos_ref, s1_ref, s2_ref,
                 q_ref, k_ref, v_ref, z_ref, xbc_ref, dt_ref, kt_ref, vt_ref, *,
                 att_w, ssm_w, conv_ch, n_ptiles):
    i = pl.program_id(0)
    h = _rms(_two_source(i, n_ptiles, xp_ref, xs_ref), g_ref[...]).astype(BF16)
    seg = seg_ref[...]
    cos, s1, s2 = cos_ref[...], s1_ref[...], s2_ref[...]

    def head_norm_rope(t, gain):
        ms = _dot((t * t).astype(BF16), seg) * (1.0 / ATT_HEAD_DIM)
        tn = t * lax.rsqrt(ms + NORM_EPS) * gain
        half = ROPE_DIM // 2
        return (tn * cos + pltpu.roll(tn, half, 1) * s1
                + pltpu.roll(tn, att_w - half, 1) * s2)

    q = head_norm_rope(_dot(h, w_ref[:, 0:att_w]), qg_ref[...])
    q_ref[...] = q * (ATT_HEAD_DIM ** -0.5 * LOG2E)
    k = head_norm_rope(_dot(h, w_ref[:, att_w:2 * att_w]), kg_ref[...])
    v = _dot(h, w_ref[:, 2 * att_w:3 * att_w])
    k_ref[...] = k
    v_ref[...] = v

    @pl.when(i < n_ptiles)
    def _():
        kt_ref[0] = k.T
        vt_ref[0] = v.T

    o = 3 * att_w
    z_ref[...] = _dot(h, w_ref[:, o:o + ssm_w])
    o += ssm_w
    xbc_ref[...] = _dot(h, w_ref[:, o:o + conv_ch])
    o += conv_ch
    dt_ref[...] = _dot(h, w_ref[:, o:o + LANES])


def _rope_tables(pos, n_heads):
    half = ROPE_DIM // 2
    inv_freq = jnp.power(ROPE_THETA, -jnp.arange(half, dtype=F32) / half)
    ang = pos.astype(F32)[:, None] * inv_freq[None, :]
    cos, sin = jnp.cos(ang), jnp.sin(ang)
    n = pos.shape[0]
    rest = ATT_HEAD_DIM - ROPE_DIM
    c = jnp.concatenate([cos, cos, jnp.ones((n, rest), F32)], axis=-1)
    s1 = jnp.concatenate([jnp.zeros((n, half), F32), sin, jnp.zeros((n, rest), F32)], axis=-1)
    s2 = jnp.concatenate([-sin, jnp.zeros((n, half + rest), F32)], axis=-1)
    return tuple(jnp.tile(t, (1, n_heads)) for t in (c, s1, s2))


def _projections(x_p, x_s, norm_mix, w_in, q_gain, k_gain, seq, dec_seq, past_len):
    n_prompt, d = x_p.shape
    n = n_prompt + x_s.shape[0]
    d_half = d // 2
    att_w, ssm_w = d_half, d - d_half
    n_heads = att_w // ATT_HEAD_DIM
    ssm_heads = ssm_w // SSM_HEAD_DIM
    conv_ch = ssm_w + 2 * SSM_GROUPS * SSM_STATE
    tm = TOKEN_TILE
    assert n % tm == 0 and n_prompt % tm == 0 and seq % tm == 0 and (n - n_prompt) == tm
    assert tm % dec_seq == 0
    c0 = 3 * att_w + ssm_w
    w = jnp.concatenate([w_in[:, :c0], w_in[:, c0 + ssm_heads:],
                         w_in[:, c0:c0 + ssm_heads],
                         jnp.zeros((d, LANES - ssm_heads), w_in.dtype)], axis=1).astype(BF16)
    wn = w.shape[1]
    pos = jnp.concatenate([jnp.arange(seq, dtype=jnp.int32),
                           past_len + jnp.arange(tm, dtype=jnp.int32) % dec_seq])
    cos, s1, s2 = _rope_tables(pos, n_heads)
    tiles_per_seq = seq // tm
    n_prompt_tiles = n_prompt // tm
    head_id = np.arange(att_w) // ATT_HEAD_DIM
    seg = jnp.asarray(head_id[:, None] == head_id[None, :], BF16)

    def tab_map(i):
        return (jnp.where(i < n_prompt_tiles, i % tiles_per_seq, tiles_per_seq), 0)

    row = lambda i: (i, 0)
    fix = lambda i: (0, 0)
    first = lambda i: (jnp.minimum(i, n_prompt_tiles - 1), 0)

    def t_map(i):
        j = jnp.minimum(i, n_prompt_tiles - 1)
        return (j // tiles_per_seq, 0, j % tiles_per_seq)

    tab = pl.BlockSpec((tm, att_w), tab_map)
    tr = pl.BlockSpec((1, att_w, tm), t_map)
    kern = functools.partial(_proj_kernel, att_w=att_w, ssm_w=ssm_w, conv_ch=conv_ch, n_ptiles=n_prompt_tiles)
    return pl.pallas_call(
        kern,
        grid=(n // tm,),
        in_specs=[pl.BlockSpec((tm, d), first), pl.BlockSpec((tm, d), fix), pl.BlockSpec((1, d), fix),
                  pl.BlockSpec((d, wn), fix), pl.BlockSpec((1, att_w), fix),
                  pl.BlockSpec((1, att_w), fix), pl.BlockSpec((att_w, att_w), fix),
                  tab, tab, tab],
        out_specs=[pl.BlockSpec((tm, att_w), row), pl.BlockSpec((tm, att_w), row),
                   pl.BlockSpec((tm, att_w), row), pl.BlockSpec((tm, ssm_w), row),
                   pl.BlockSpec((tm, conv_ch), row), pl.BlockSpec((tm, LANES), row), tr, tr],
        out_shape=[jax.ShapeDtypeStruct((n, att_w), F32), jax.ShapeDtypeStruct((n, att_w), F32),
                   jax.ShapeDtypeStruct((n, att_w), F32), jax.ShapeDtypeStruct((n, ssm_w), F32),
                   jax.ShapeDtypeStruct((n, conv_ch), F32), jax.ShapeDtypeStruct((n, LANES), F32),
                   jax.ShapeDtypeStruct((n_prompt // seq, att_w, seq), F32),
                   jax.ShapeDtypeStruct((n_prompt // seq, att_w, seq), F32)],
        compiler_params=_cparams(("arbitrary",)),
        name="proj",
    )(x_p, x_s, norm_mix.reshape(1, d), w, jnp.tile(q_gain, n_heads).reshape(1, att_w),
      jnp.tile(k_gain, n_heads).reshape(1, att_w), seg, cos, s1, s2)


def _attn_prompt_kernel(q_ref, k_ref, v_ref, o_ref, num_ref, m_ref, den_ref, *, seq):
    blk = ATT_BLOCK
    lane = lax.broadcasted_iota(jnp.int32, (blk, LANES), 1)
    head0 = lane < ATT_HEAD_DIM
    qi = lax.broadcasted_iota(jnp.int32, (blk, 2 * blk), 0) + blk
    ki = lax.broadcasted_iota(jnp.int32, (blk, 2 * blk), 1)
    dist = qi - ki
    band = (dist >= 0) & (dist <= blk)
    bias_rest = jnp.where(band, 0.0, NEG)
    bias_first = jnp.where(band & (ki >= blk), 0.0, NEG)

    for di, (window, dil) in enumerate(DILATIONS):
        assert window // dil == blk
        nb = seq // dil // blk

        def body(i, carry, dil=dil, nb=nb, di=di):
            r = i // nb
            j = i % nb
            if dil > 1:
                start = r + dil * blk * j
                prev = r + dil * blk * jnp.maximum(j - 1, 0)
                rows = pl.ds(start, blk, stride=dil)
                prows = pl.ds(prev, blk, stride=dil)
            else:
                rows = pl.ds(pl.multiple_of(blk * j, blk), blk)
                prows = pl.ds(pl.multiple_of(blk * jnp.maximum(j - 1, 0), blk), blk)
            qb = q_ref[rows, :].astype(BF16)
            k2 = jnp.concatenate([k_ref[prows, :], k_ref[rows, :]], axis=0).astype(BF16)
            v2 = jnp.concatenate([v_ref[prows, :], v_ref[rows, :]], axis=0).astype(BF16)
            bias = jnp.where(j > 0, bias_rest, bias_first)
            zero = jnp.zeros_like(qb)
            q2 = jnp.concatenate([jnp.where(head0, qb, zero), jnp.where(head0, zero, qb)], axis=0)
            s = _dot_nt(q2, k2) + jnp.concatenate([bias, bias], axis=0)
            m = jnp.max(s, axis=-1, keepdims=True)
            p = jnp.exp2(s - m)
            den = jnp.sum(p, axis=-1, keepdims=True)
            pv = _dot(p.astype(BF16), v2)
            num_ref[di, rows, :] = jnp.where(head0, pv[:blk], pv[blk:])
            m_ref[di, rows, :] = jnp.where(head0, m[:blk], m[blk:])
            den_ref[di, rows, :] = jnp.where(head0, den[:blk], den[blk:])
            return carry

        lax.fori_loop(0, dil * nb, body, 0, unroll=ATTN_UNROLL)

    m_all = jnp.maximum(jnp.maximum(m_ref[0], m_ref[1]), m_ref[2])
    num = jnp.zeros((seq, LANES), F32)
    den = jnp.zeros((seq, LANES), F32)
    for di in range(len(DILATIONS)):
        a = jnp.exp2(m_ref[di] - m_all)
        num = num + a * num_ref[di]
        den = den + a * den_ref[di]
    o_ref[...] = (num / den).astype(o_ref.dtype)


def _attn_prompt(q, k, v, batch, seq):
    att_w = q.shape[1]
    pairs = att_w // LANES
    nd = len(DILATIONS)
    blk = pl.BlockSpec((seq, LANES), lambda b, h: (b, h))
    return pl.pallas_call(
        functools.partial(_attn_prompt_kernel, seq=seq),
        grid=(batch, pairs),
        in_specs=[blk, blk, blk],
        out_specs=blk,
        out_shape=jax.ShapeDtypeStruct((batch * seq, att_w), BF16),
        scratch_shapes=[pltpu.VMEM((nd, seq, LANES), F32), pltpu.VMEM((nd, seq, LANES), F32),
                        pltpu.VMEM((nd, seq, LANES), F32)],
        compiler_params=_cparams(("parallel", "parallel")),
        name="attn_prompt",
    )(q, k, v)


def _attn_sample_kernel(q_ref, kn_ref, vn_ref, kc_ref, vc_ref, cc_ref, cn_ref, o_ref, *,
                        dec_seq, n_heads):
    w = q_ref.shape[-1]
    rows = dec_seq * n_heads
    q = q_ref[0]
    qm = jnp.concatenate([jnp.broadcast_to(q[t:t + 1], (n_heads, w)) for t in range(dec_seq)], axis=0)
    lane_head = lax.broadcasted_iota(jnp.int32, (rows, w), 1) // ATT_HEAD_DIM
    row_head = lax.broadcasted_iota(jnp.int32, (rows, w), 0) % n_heads
    own = lane_head == row_head
    qm = jnp.where(own, qm, 0.0).astype(BF16)
    zpad = jnp.zeros((SUBLANES - dec_seq, w), F32)
    kn = jnp.concatenate([kn_ref[0], zpad], axis=0).astype(BF16)
    vn = jnp.concatenate([vn_ref[0], zpad], axis=0).astype(BF16)
    cc, cn = cc_ref[...], cn_ref[...]
    s_c = jnp.where(cc > 0, _dot(qm, kc_ref[0].astype(BF16)), -jnp.inf)
    s_n = jnp.where(cn > 0, _dot_nt(qm, kn), -jnp.inf)
    m = jnp.maximum(jnp.max(s_c, axis=-1, keepdims=True), jnp.max(s_n, axis=-1, keepdims=True))
    p_c = cc * jnp.exp2(s_c - m)
    p_n = cn * jnp.exp2(s_n - m)
    den = jnp.sum(p_c, axis=-1, keepdims=True) + jnp.sum(p_n, axis=-1, keepdims=True)
    o = _dot_nt(p_c.astype(BF16), vc_ref[0].astype(BF16)) + _dot(p_n.astype(BF16), vn)
    o = jnp.where(own, o / den, 0.0)
    o_ref[0] = jnp.sum(o.reshape(dec_seq, n_heads, w), axis=1).astype(o_ref.dtype)


def _attn_sample(q, k_new, v_new, k_cache_t, v_cache_t):
    b, t, w = q.shape
    n_buf = k_cache_t.shape[2]
    n_heads = w // ATT_HEAD_DIM
    assert n_heads == SUBLANES and t <= SUBLANES
    assert n_buf >= max(win for win, _ in DILATIONS)

    def count(dist):
        return sum(((dist >= 0) & (dist % dil == 0) & (dist <= win)).astype(np.float32)
                   for win, dil in DILATIONS)

    tq = np.repeat(np.arange(t), n_heads)[:, None]
    cc = count(n_buf + tq - np.arange(n_buf)[None, :])
    jn = np.arange(SUBLANES)[None, :]
    cn = np.where(jn < t, count(tq - jn), 0.0).astype(np.float32)
    rows = t * n_heads
    new = pl.BlockSpec((1, t, w), lambda i: (i, 0, 0))
    cache = pl.BlockSpec((1, w, n_buf), lambda i: (i, 0, 0))
    return pl.pallas_call(
        functools.partial(_attn_sample_kernel, dec_seq=t, n_heads=n_heads),
        grid=(b,),
        in_specs=[new, new, new, cache, cache,
                  pl.BlockSpec((rows, n_buf), lambda i: (0, 0)),
                  pl.BlockSpec((rows, SUBLANES), lambda i: (0, 0))],
        out_specs=new,
        out_shape=jax.ShapeDtypeStruct((b, t, w), BF16),
        compiler_params=_cparams(("parallel",)),
        name="attn_sample",
    )(q, k_new, v_new, k_cache_t, v_cache_t, jnp.asarray(cc), jnp.asarray(cn))


def _softplus(x):
    return jnp.maximum(x, 0.0) + jnp.log1p(jnp.exp(-jnp.abs(x)))


def _silu(x):
    return x * jax.nn.sigmoid(x)


def _ssd_kernel(xbc_ref, dt_ref, dtt_ref, z_ref, cp_ref, h0_ref, cw_ref, cb_ref, dtb_ref, dtbt_ref,
                al_ref, alt_ref, dsk_ref, og_ref, y_ref, hl_ref, ct_ref, xpad_ref, h_ref, *,
                q, valid, ssm_w, n_pairs):
    c = pl.program_id(1)
    tail = CONV_WIDTH - 1

    @pl.when(c == 0)
    def _():
        h_ref[...] = h0_ref[0]
        xpad_ref[0:SUBLANES, :] = jnp.zeros((SUBLANES, xpad_ref.shape[1]), F32)
        xpad_ref[SUBLANES - tail:SUBLANES, :] = cp_ref[0]

    x = xbc_ref[...]
    xpad_ref[SUBLANES:SUBLANES + q, :] = x
    conv = cb_ref[...]
    for j in range(CONV_WIDTH):
        o = SUBLANES - tail + j
        conv = conv + xpad_ref[o:o + q, :] * cw_ref[j:j + 1, :]

    @pl.when(c == pl.num_programs(1) - 1)
    def _():
        ct_ref[0] = xpad_ref[valid:valid + SUBLANES, :]

    xpad_ref[0:SUBLANES, :] = x[q - SUBLANES:q, :]
    u = _silu(conv)
    xs = u[:, :ssm_w]
    gw = SSM_STATE
    bm = u[:, ssm_w:ssm_w + SSM_GROUPS * gw].astype(BF16)
    cm = u[:, ssm_w + SSM_GROUPS * gw:].astype(BF16)

    dt = _softplus(dt_ref[...] + dtb_ref[...])
    dtt = _softplus(dtt_ref[0] + dtbt_ref[...])
    if valid < q:
        dt = jnp.where(lax.broadcasted_iota(jnp.int32, dt.shape, 0) < valid, dt, 0.0)
        dtt = jnp.where(lax.broadcasted_iota(jnp.int32, dtt.shape, 1) < valid, dtt, 0.0)
    ri = lax.broadcasted_iota(jnp.int32, (q, q), 0)
    ci = lax.broadcasted_iota(jnp.int32, (q, q), 1)
    causal = ci <= ri
    acum = _dot_f32(causal.astype(F32), dt * -jnp.exp(al_ref[...]))
    acumt = _dot_f32(dtt * -jnp.exp(alt_ref[...]), (ri <= ci).astype(F32))

    left = lax.broadcasted_iota(jnp.int32, (q, LANES), 1) < SSM_HEAD_DIM
    top = lax.broadcasted_iota(jnp.int32, (LANES, LANES), 0) < SSM_HEAD_DIM
    rep = 2 * n_pairs // SSM_GROUPS
    gmat = [_dot_nt(cm[:, g * gw:(g + 1) * gw], bm[:, g * gw:(g + 1) * gw]) for g in range(SSM_GROUPS)]
    ys = []
    for pr in range(n_pairs):
        ha, hb = 2 * pr, 2 * pr + 1
        g = ha // rep
        bg = bm[:, g * gw:(g + 1) * gw]
        cg = cm[:, g * gw:(g + 1) * gw]
        xpair = xs[:, pr * LANES:(pr + 1) * LANES]
        xd = xpair * jnp.where(left, dt[:, ha:ha + 1], dt[:, hb:hb + 1])
        xdb = xd.astype(BF16)
        yd = []
        for hh in (ha, hb):
            seg = acum[:, hh:hh + 1] - acumt[hh:hh + 1, :]
            decay = jnp.exp(jnp.where(causal, seg, -jnp.inf))
            yd.append(_dot((gmat[g] * decay).astype(BF16), xdb))
        ac = jnp.where(left, acum[:, ha:ha + 1], acum[:, hb:hb + 1])
        hprev = h_ref[pr]
        y_off = _dot_nt(cg, hprev.astype(BF16)) * jnp.exp(ac)
        to_end = jnp.exp(ac[q - 1:q, :] - ac)
        upd = _dot_tn((xd * to_end).astype(BF16), bg)
        cdec = jnp.where(top, jnp.exp(acum[q - 1:q, ha:ha + 1]), jnp.exp(acum[q - 1:q, hb:hb + 1]))
        h_ref[pr] = hprev * cdec + upd
        ys.append(jnp.where(left, yd[0], yd[1]) + y_off
                  + xpair * dsk_ref[:, pr * LANES:(pr + 1) * LANES])
    y = jnp.concatenate(ys, axis=1) * _silu(z_ref[...])
    y_ref[...] = _rms(y, og_ref[...]).astype(y_ref.dtype)

    @pl.when(c == pl.num_programs(1) - 1)
    def _():
        hl_ref[0] = h_ref[...]


def _ssd(xbc, dt_raw, z, conv_prev, h0, conv_w, conv_b, dt_bias, a_log, d_skip, out_gain, b, l, valid_len):
    conv_ch = xbc.shape[-1]
    ssm_w = z.shape[-1]
    n_heads = ssm_w // SSM_HEAD_DIM
    n_pairs = n_heads // 2
    assert 2 * SSM_HEAD_DIM == LANES and SSM_STATE == LANES and n_heads <= SUBLANES
    assert (n_heads // SSM_GROUPS) % 2 == 0
    q = SSM_CHUNK if l % SSM_CHUNK == 0 else l
    assert l % q == 0 and q % SUBLANES == 0 and (valid_len == l or q == l)
    nc = l // q
    dtt = jnp.swapaxes(dt_raw[:b * l, :n_heads].reshape(b, l, n_heads), 1, 2)
    pad_h = lambda a: jnp.pad(a.reshape(1, n_heads), ((0, 0), (0, LANES - n_heads)))
    per_lane = lambda a: jnp.repeat(a, SSM_HEAD_DIM).reshape(1, ssm_w)
    h0p = h0.reshape(b, n_pairs, LANES, SSM_STATE).astype(F32)
    tail = CONV_WIDTH - 1
    tok = lambda w: pl.BlockSpec((q, w), lambda i, j: (i * nc + j, 0))
    fix2 = lambda r, w: pl.BlockSpec((r, w), lambda i, j: (0, 0))
    st = pl.BlockSpec((1, n_pairs, LANES, SSM_STATE), lambda i, j: (i, 0, 0, 0))
    kern = functools.partial(_ssd_kernel, q=q, valid=valid_len if q == l else q, ssm_w=ssm_w, n_pairs=n_pairs)
    y, h_last, conv_tail = pl.pallas_call(
        kern,
        grid=(b, nc),
        in_specs=[tok(conv_ch), tok(LANES), pl.BlockSpec((1, n_heads, q), lambda i, j: (i, 0, j)),
                  tok(ssm_w), pl.BlockSpec((1, tail, conv_ch), lambda i, j: (i, 0, 0)), st,
                  fix2(CONV_WIDTH, conv_ch), fix2(1, conv_ch), fix2(1, LANES), fix2(n_heads, 1),
                  fix2(1, LANES), fix2(n_heads, 1), fix2(1, ssm_w), fix2(1, ssm_w)],
        out_specs=[tok(ssm_w), st, pl.BlockSpec((1, SUBLANES, conv_ch), lambda i, j: (i, 0, 0))],
        out_shape=[jax.ShapeDtypeStruct((b * l, ssm_w), BF16),
                   jax.ShapeDtypeStruct((b, n_pairs, LANES, SSM_STATE), F32),
                   jax.ShapeDtypeStruct((b, SUBLANES, conv_ch), F32)],
        scratch_shapes=[pltpu.VMEM((q + SUBLANES, conv_ch), F32),
                        pltpu.VMEM((n_pairs, LANES, SSM_STATE), F32)],
        compiler_params=_cparams(("parallel", "arbitrary")),
        name="ssd",
    )(xbc, dt_raw, dtt, z, conv_prev.astype(F32), h0p, conv_w, conv_b.reshape(1, conv_ch),
      pad_h(dt_bias), dt_bias.reshape(n_heads, 1), pad_h(a_log), a_log.reshape(n_heads, 1),
      per_lane(d_skip), out_gain.reshape(1, ssm_w))
    return y, h_last.reshape(b, n_heads, SSM_HEAD_DIM, SSM_STATE), conv_tail[:, SUBLANES - tail:]


def _head_rms_store(dst_ref, t, gain, scale=None):
    for hd in range(t.shape[1] // MEM_HEAD_DIM):
        sl = slice(hd * MEM_HEAD_DIM, (hd + 1) * MEM_HEAD_DIM)
        r = _rms(t[:, sl], gain)
        if scale is not None:
            r = r * scale
        dst_ref[:, sl] = r.astype(dst_ref.dtype)


def _mem_kv_kernel(m_ref, g_ref, wk_ref, wv_ref, kg_ref, k_ref, v_ref):
    h = _rms(m_ref[...], g_ref[...]).astype(BF16)
    _head_rms_store(k_ref, _dot(h, wk_ref[...]), kg_ref[...])
    v_ref[...] = _dot(h, wv_ref[...])


def _mem_kv(mem, in_gain, w_k, w_v, k_gain):
    n, d = mem.shape
    mw = w_k.shape[1]
    tm = TOKEN_TILE
    assert n % tm == 0 and MEM_HEAD_DIM == LANES
    row = lambda i: (i, 0)
    fix = lambda i: (0, 0)
    return pl.pallas_call(
        _mem_kv_kernel,
        grid=(n // tm,),
        in_specs=[pl.BlockSpec((tm, d), row), pl.BlockSpec((1, d), fix), pl.BlockSpec((d, mw), fix),
                  pl.BlockSpec((d, mw), fix), pl.BlockSpec((1, MEM_HEAD_DIM), fix)],
        out_specs=[pl.BlockSpec((tm, mw), row), pl.BlockSpec((tm, mw), row)],
        out_shape=[jax.ShapeDtypeStruct((n, mw), F32), jax.ShapeDtypeStruct((n, mw), F32)],
        compiler_params=_cparams(("parallel",)),
        name="mem_kv",
    )(mem, in_gain.reshape(1, d), w_k.astype(BF16), w_v.astype(BF16), k_gain.reshape(1, MEM_HEAD_DIM))


def _out_proj_kernel(xp_ref, xs_ref, ap_ref, as_ref, yp_ref, ys_ref, wo_ref, g_ref, wq_ref, qg_ref,
                     x1_ref, q_ref, *, n_ptiles):
    i = pl.program_id(0)
    att_w = ap_ref.shape[1]
    x1 = (_two_source(i, n_ptiles, xp_ref, xs_ref)
          + _dot(_two_source(i, n_ptiles, ap_ref, as_ref), wo_ref[0:att_w, :])
          + _dot(_two_source(i, n_ptiles, yp_ref, ys_ref), wo_ref[att_w:, :]))
    x1_ref[...] = x1
    h = _rms(x1, g_ref[...]).astype(BF16)
    _head_rms_store(q_ref, _dot(h, wq_ref[...]), qg_ref[...], MEM_HEAD_DIM ** -0.5)


def _out_proj(x, att, yssm, w_out, norm_mem, w_mem_q, mem_q_gain):
    tm = TOKEN_TILE
    n_prompt, d = x[0].shape
    assert all(a[1].shape[0] == tm and a[0].shape[0] == n_prompt for a in (x, att, yssm))
    n = n_prompt + tm
    n_ptiles = n_prompt // tm
    att_w, ssm_w = att[0].shape[1], yssm[0].shape[1]
    mw = w_mem_q.shape[1]
    row = lambda i: (i, 0)
    fix = lambda i: (0, 0)
    first = lambda i: (jnp.minimum(i, n_ptiles - 1), 0)
    pair = lambda w: [pl.BlockSpec((tm, w), first), pl.BlockSpec((tm, w), fix)]
    return pl.pallas_call(
        functools.partial(_out_proj_kernel, n_ptiles=n_ptiles),
        grid=(n // tm,),
        in_specs=pair(d) + pair(att_w) + pair(ssm_w)
        + [pl.BlockSpec((att_w + ssm_w, d), fix), pl.BlockSpec((1, d), fix),
           pl.BlockSpec((d, mw), fix), pl.BlockSpec((1, MEM_HEAD_DIM), fix)],
        out_specs=[pl.BlockSpec((tm, d), row), pl.BlockSpec((tm, mw), row)],
        out_shape=[jax.ShapeDtypeStruct((n, d), F32), jax.ShapeDtypeStruct((n, mw), BF16)],
        compiler_params=_cparams(("parallel",)),
        name="out_proj",
    )(*x, *att, *yssm, w_out.astype(BF16), norm_mem.reshape(1, d), w_mem_q.astype(BF16),
      mem_q_gain.reshape(1, MEM_HEAD_DIM))


def _mem_attn_kernel(q_ref, k_ref, v_ref, o_ref):
    for hd in range(q_ref.shape[-1] // MEM_HEAD_DIM):
        sl = slice(hd * MEM_HEAD_DIM, (hd + 1) * MEM_HEAD_DIM)
        head = (lambda r: r[0, :, hd, :]) if len(k_ref.shape) == 4 else (lambda r: r[0, :, sl])
        s = _dot_nt(q_ref[:, sl], head(k_ref).astype(BF16))
        p = jnp.exp(s - jnp.max(s, axis=-1, keepdims=True))
        den = jnp.sum(p, axis=-1, keepdims=True)
        o_ref[:, sl] = (_dot(p.astype(BF16), head(v_ref).astype(BF16)) / den).astype(o_ref.dtype)


def _mem_attn(q, mem_k, mem_v, b, l, tq):
    w = q.shape[-1]
    n_mem = mem_k.shape[1]
    assert l % tq == 0
    qs = pl.BlockSpec((tq, w), lambda i, j: (i * (l // tq) + j, 0))
    ms = pl.BlockSpec((1,) + mem_k.shape[1:], lambda i, j: (i,) + (0,) * (mem_k.ndim - 1))
    return pl.pallas_call(
        _mem_attn_kernel,
        grid=(b, l // tq),
        in_specs=[qs, ms, ms],
        out_specs=qs,
        out_shape=jax.ShapeDtypeStruct((b * l, w), BF16),
        compiler_params=_cparams(("parallel", "parallel")),
        name="mem_attn",
    )(q, mem_k, mem_v)


def _store_row_tiles(dst_ref, x):
    rows = x.shape[0]
    for j in range(x.shape[1] // LANES):
        dst_ref[pl.ds(j, rows, stride=SUBLANES), :] = x[:, j * LANES:(j + 1) * LANES]


def _load_row_tiles(src_ref, first, rows):
    return jnp.concatenate([src_ref[pl.ds(first * SUBLANES + j, rows, stride=SUBLANES), :]
                            for j in range(SUBLANES)], axis=1)


def _post_kernel(x1_ref, op_ref, os_ref, wo_ref, g_ref, wrh_ref, wrl_ref, br_ref,
                 x2_ref, hf_ref, idx_ref, w_ref, *, n_ptiles):
    x2 = x1_ref[...] + _dot(_two_source(pl.program_id(0), n_ptiles, op_ref, os_ref), wo_ref[...])
    x2_ref[...] = x2
    hf = _rms(x2, g_ref[...])
    _store_row_tiles(hf_ref, hf)
    hi = hf.astype(BF16)
    lo = (hf - hi.astype(F32)).astype(BF16)
    logits = (_dot(hi, wrh_ref[...]) + _dot(lo, wrh_ref[...]) + _dot(hi, wrl_ref[...])) + br_ref[...]
    lane = lax.broadcasted_iota(jnp.int32, logits.shape, 1)
    vals, idxs = [], []
    for _ in range(TOP_K):
        m = jnp.max(logits, axis=-1, keepdims=True)
        ix = jnp.min(jnp.where(logits == m, lane, LANES), axis=-1, keepdims=True)
        vals.append(m)
        idxs.append(ix)
        logits = jnp.where(lane == ix, -jnp.inf, logits)
    es = [jnp.exp(v - vals[0]) for v in vals]
    tot = es[0]
    for e in es[1:]:
        tot = tot + e
    wout = jnp.zeros(logits.shape, F32)
    iout = jnp.zeros(logits.shape, jnp.int32)
    for kk in range(TOP_K):
        wout = jnp.where(lane == kk, es[kk] / tot, wout)
        iout = jnp.where(lane == kk, idxs[kk], iout)
    w_ref[...] = wout
    idx_ref[...] = iout


def _post(x1, o, w_mem_o, norm_ffn, w_router, b_router):
    n, d = x1.shape
    mw = o[0].shape[1]
    n_exp = w_router.shape[1]
    assert n_exp <= LANES
    tm = TOKEN_TILE
    assert o[1].shape[0] == tm and o[0].shape[0] + tm == n
    n_ptiles = o[0].shape[0] // tm
    wr = jnp.pad(w_router, ((0, 0), (0, LANES - n_exp)))
    wrh = wr.astype(BF16)
    wrl = (wr - wrh.astype(F32)).astype(BF16)
    br = jnp.concatenate([b_router.astype(F32), jnp.full((LANES - n_exp,), NEG, F32)]).reshape(1, LANES)
    row = lambda i: (i, 0)
    fix = lambda i: (0, 0)
    return pl.pallas_call(
        functools.partial(_post_kernel, n_ptiles=n_ptiles),
        grid=(n // tm,),
        in_specs=[pl.BlockSpec((tm, d), row),
                  pl.BlockSpec((tm, mw), lambda i: (jnp.minimum(i, n_ptiles - 1), 0)),
                  pl.BlockSpec((tm, mw), fix), pl.BlockSpec((mw, d), fix),
                  pl.BlockSpec((1, d), fix), pl.BlockSpec((d, LANES), fix), pl.BlockSpec((d, LANES), fix),
                  pl.BlockSpec((1, LANES), fix)],
        out_specs=[pl.BlockSpec((tm, d), row), pl.BlockSpec((tm * SUBLANES, LANES), row),
                   pl.BlockSpec((tm, LANES), row), pl.BlockSpec((tm, LANES), row)],
        out_shape=[jax.ShapeDtypeStruct((n, d), F32), jax.ShapeDtypeStruct((n * SUBLANES, LANES), F32),
                   jax.ShapeDtypeStruct((n, LANES), jnp.int32), jax.ShapeDtypeStruct((n, LANES), F32)],
        compiler_params=_cparams(("parallel",)),
        name="post",
    )(x1, *o, w_mem_o.astype(BF16), norm_ffn.reshape(1, d), wrh, wrl, br)


GU_CHUNK = 2 * LANES


def _regroup_kernel(w_ref, p_ref, o_ref):
    for c in range(w_ref.shape[-1] // GU_CHUNK):
        sl = slice(c * GU_CHUNK, (c + 1) * GU_CHUNK)
        o_ref[0, :, sl] = _dot(w_ref[0, :, sl].astype(BF16), p_ref[...]).astype(BF16)


def _regroup_gate_up(w_gate_up):
    n_exp, d, ff2 = w_gate_up.shape
    wblk = 4 * GU_CHUNK
    assert ff2 % wblk == 0
    src = np.arange(GU_CHUNK)
    dst = np.where(src % 2 == 0, src // 2, LANES + src // 2)
    perm = np.zeros((GU_CHUNK, GU_CHUNK), np.float32)
    perm[src, dst] = 1.0
    blk = pl.BlockSpec((1, d, wblk), lambda e, j: (e, 0, j))
    return pl.pallas_call(
        _regroup_kernel,
        grid=(n_exp, ff2 // wblk),
        in_specs=[blk, pl.BlockSpec((GU_CHUNK, GU_CHUNK), lambda e, j: (0, 0))],
        out_specs=blk,
        out_shape=jax.ShapeDtypeStruct((n_exp, d, ff2), BF16),
        compiler_params=_cparams(("parallel", "parallel")),
        name="regroup_gate_up",
    )(w_gate_up, jnp.asarray(perm, BF16))


def _rank_kernel(idx_ref, rank_ref, cnt_ref, carry_ref):
    @pl.when(pl.program_id(0) == 0)
    def _():
        carry_ref[...] = jnp.zeros(carry_ref.shape, F32)

    idx = idx_ref[...]
    tm = idx.shape[0]
    lane = lax.broadcasted_iota(jnp.int32, idx.shape, 1)
    hot = [(lane == idx[:, kk:kk + 1]).astype(F32) for kk in range(TOP_K)]
    tot = hot[0]
    for h in hot[1:]:
        tot = tot + h
    earlier = (lax.broadcasted_iota(jnp.int32, (tm, tm), 1)
               < lax.broadcasted_iota(jnp.int32, (tm, tm), 0)).astype(BF16)
    base = carry_ref[...] + _dot(earlier, tot.astype(BF16))
    out = jnp.zeros(idx.shape, jnp.int32)
    for kk in range(TOP_K):
        r = jnp.sum(hot[kk] * base, axis=-1, keepdims=True)
        out = jnp.where(lane == kk, r.astype(jnp.int32), out)
        base = base + hot[kk]
    rank_ref[...] = out
    carry_ref[...] = carry_ref[...] + jnp.sum(tot, axis=0, keepdims=True)
    cnt_ref[...] = carry_ref[...]


def _rank(top_idx):
    n, w = top_idx.shape
    tm = TOKEN_TILE
    return pl.pallas_call(
        _rank_kernel,
        grid=(n // tm,),
        in_specs=[pl.BlockSpec((tm, w), lambda i: (i, 0))],
        out_specs=[pl.BlockSpec((tm, w), lambda i: (i, 0)), pl.BlockSpec((1, w), lambda i: (0, 0))],
        out_shape=[jax.ShapeDtypeStruct((n, w), jnp.int32), jax.ShapeDtypeStruct((1, w), F32)],
        scratch_shapes=[pltpu.VMEM((1, w), F32)],
        compiler_params=_cparams(("arbitrary",)),
        name="rank",
    )(top_idx)


def _dispatch_kernel(seg_start_ref, seg_len_ref, dest_ref, hf_ref, xb_out, zrow, sem, zsem):
    tc = hf_ref.shape[0] // SUBLANES

    def slot(ref, row8):
        return ref.at[pl.ds(pl.multiple_of(row8, SUBLANES), SUBLANES), :]

    @pl.when(pl.program_id(0) == 0)
    def _():
        zrow[...] = jnp.zeros(zrow.shape, zrow.dtype)

        def zero_copy(row):
            return pltpu.make_async_copy(zrow, slot(xb_out, row * SUBLANES), zsem)

        def per_segment(op):
            def seg(s, carry):
                base = seg_start_ref[s]

                def row(r, c):
                    op(zero_copy(base + r))
                    return c

                return lax.fori_loop(0, seg_len_ref[s], row, carry)

            lax.fori_loop(0, seg_start_ref.shape[0], seg, 0)

        per_segment(lambda cp: cp.start())
        per_segment(lambda cp: cp.wait())

    for r in range(TOP_K * tc):
        pltpu.make_async_copy(slot(hf_ref, (r % tc) * SUBLANES), slot(xb_out, dest_ref[0, 0, r]),
                              sem).start(priority=r % 2)
    for _ in range(TOP_K):
        pltpu.make_async_copy(hf_ref, xb_out.at[pl.ds(0, tc * SUBLANES), :], sem).wait()


def _experts_kernel(be_ref, nact_ref, x_ref, wgu_ref, bgu_ref, wd_ref, bd_ref, y_ref):
    i = pl.program_id(0)

    tm = x_ref.shape[0] // SUBLANES

    @pl.when(i < nact_ref[0])
    def _():
        gu = _dot(_load_row_tiles(x_ref, 0, tm).astype(BF16), wgu_ref[0]) + bgu_ref[0]
        acts = []
        for c in range(gu.shape[1] // GU_CHUNK):
            g = jnp.minimum(gu[:, c * GU_CHUNK:c * GU_CHUNK + LANES], SWIGLU_LIMIT)
            u = jnp.clip(gu[:, c * GU_CHUNK + LANES:(c + 1) * GU_CHUNK], -SWIGLU_LIMIT, SWIGLU_LIMIT)
            acts.append(((u + 1.0) * (g * jax.nn.sigmoid(SWIGLU_ALPHA * g))).astype(BF16))
        _store_row_tiles(y_ref, _dot(jnp.concatenate(acts, axis=1), wd_ref[0]) + bd_ref[0])

    @pl.when(i >= nact_ref[0])
    def _():
        y_ref[...] = jnp.zeros(y_ref.shape, y_ref.dtype)


def _combine_kernel(dest_ref, x2_ref, w_ref, y_hbm, op_ref, os_ref, buf, sems, *, n_tiles, n_ptiles):
    i = pl.program_id(0)
    tc = x2_ref.shape[0]

    def start_gather(slot):
        for r in range(TOP_K * tc):
            src = y_hbm.at[pl.ds(pl.multiple_of(dest_ref[0, 0, r], SUBLANES), SUBLANES), :]
            pltpu.make_async_copy(src, buf.at[slot, pl.ds(r * SUBLANES, SUBLANES), :],
                                  sems.at[slot]).start(priority=r % 2)

    def finish(slot):
        rows = buf.at[slot]
        pltpu.make_async_copy(y_hbm.at[pl.ds(0, TOP_K * tc * SUBLANES), :], rows, sems.at[slot]).wait()
        acc = x2_ref[...]
        w = w_ref[...]
        for kk in range(TOP_K):
            acc = acc + w[:, kk:kk + 1] * _load_row_tiles(rows, kk * tc, tc)

        @pl.when(i - 1 < n_ptiles)
        def _():
            op_ref[...] = acc

        @pl.when(i - 1 >= n_ptiles)
        def _():
            os_ref[...] = acc

    for slot in range(2):
        pl.when((i < n_tiles) & (i % 2 == slot))(functools.partial(start_gather, slot))
    for slot in range(2):
        pl.when((i > 0) & ((i + 1) % 2 == slot))(functools.partial(finish, slot))


def _moe(hf, x2, top_idx, top_w, w_gate_up, b_gate_up, w_down, b_down, n_prompt):
    n_tok, d = x2.shape
    n_exp, _, ff2 = w_gate_up.shape
    ff = ff2 // 2
    tm, tc = MOE_TILE, COMBINE_TILE
    row8 = SUBLANES
    assert n_tok % tc == 0 and d == row8 * LANES and hf.shape == (n_tok * row8, LANES)
    n_assign = n_tok * TOP_K
    n_blocks = -(-(n_assign + n_exp * (tm - 1)) // tm)
    n_rows = n_blocks * tm

    rank, cnt = _rank(top_idx)
    counts = cnt[0, :n_exp].astype(jnp.int32)
    padded = (counts + tm - 1) // tm * tm
    pend = jnp.cumsum(padded)
    pstart = pend - padded
    choice = top_idx[:, :TOP_K]
    first = jnp.sum(jnp.where(choice[:, :, None] == jnp.arange(n_exp, dtype=jnp.int32), pstart, 0), axis=-1)
    dest = ((first + rank[:, :TOP_K]) * row8).astype(jnp.int32)
    tiles_of = lambda t: dest.reshape(n_tok // t, t, TOP_K).transpose(0, 2, 1).reshape(n_tok // t, 1, TOP_K * t)
    dest_tiles = tiles_of(tc)
    td = DISPATCH_TILE
    assert n_tok % td == 0
    block_e = jnp.minimum(jnp.sum(jnp.arange(n_blocks, dtype=jnp.int32)[None, :] * tm >= pend[:, None], axis=0),
                          n_exp - 1).astype(jnp.int32)
    n_active = (pend[-1:] // tm).astype(jnp.int32)
    seg_start = jnp.concatenate([pstart + counts, pend[-1:]]).astype(jnp.int32)
    seg_len = jnp.concatenate([padded - counts, n_rows - pend[-1:]]).astype(jnp.int32)

    xb = pl.pallas_call(
        _dispatch_kernel,
        grid_spec=pltpu.PrefetchScalarGridSpec(
            num_scalar_prefetch=2,
            grid=(n_tok // td,),
            in_specs=[pl.BlockSpec((1, 1, TOP_K * td), lambda i, ss, sl: (i, 0, 0), memory_space=pltpu.SMEM),
                      pl.BlockSpec((td * row8, LANES), lambda i, ss, sl: (i, 0))],
            out_specs=pl.BlockSpec(memory_space=pl.ANY),
            scratch_shapes=[pltpu.VMEM((row8, LANES), F32), pltpu.SemaphoreType.DMA(()),
                            pltpu.SemaphoreType.DMA(())]),
        out_shape=jax.ShapeDtypeStruct((n_rows * row8, LANES), F32),
        compiler_params=_cparams(("arbitrary",)),
        name="dispatch",
    )(seg_start, seg_len, tiles_of(td), hf)

    wgu = _regroup_gate_up(w_gate_up)
    bgu = b_gate_up.reshape(n_exp, ff2 // GU_CHUNK, LANES, 2).transpose(0, 1, 3, 2).reshape(n_exp, 1, ff2)
    by_e = lambda i, be, na: (be[i], 0, 0)
    yb = pl.pallas_call(
        _experts_kernel,
        grid_spec=pltpu.PrefetchScalarGridSpec(
            num_scalar_prefetch=2,
            grid=(n_blocks,),
            in_specs=[pl.BlockSpec((tm * row8, LANES), lambda i, be, na: (jnp.minimum(i, na[0] - 1), 0)),
                      pl.BlockSpec((1, d, ff2), by_e), pl.BlockSpec((1, 1, ff2), by_e),
                      pl.BlockSpec((1, ff, d), by_e), pl.BlockSpec((1, 1, d), by_e)],
            out_specs=pl.BlockSpec((tm * row8, LANES), lambda i, be, na: (i, 0))),
        out_shape=jax.ShapeDtypeStruct((n_rows * row8, LANES), F32),
        compiler_params=_cparams(("arbitrary",)),
        name="experts",
    )(block_e, n_active, xb, wgu, bgu, w_down.astype(BF16), b_down.reshape(n_exp, 1, d))

    assert n_prompt % tc == 0 and 0 < n_prompt < n_tok
    n_ptiles = n_prompt // tc
    n_tiles = n_tok // tc
    prev = lambda i: jnp.maximum(i - 1, 0)
    return pl.pallas_call(
        functools.partial(_combine_kernel, n_tiles=n_tiles, n_ptiles=n_ptiles),
        grid=(n_tiles + 1,),
        in_specs=[pl.BlockSpec((1, 1, TOP_K * tc), lambda i: (jnp.minimum(i, n_tiles - 1), 0, 0),
                               memory_space=pltpu.SMEM),
                  pl.BlockSpec((tc, d), lambda i: (prev(i), 0)),
                  pl.BlockSpec((tc, LANES), lambda i: (prev(i), 0)),
                  pl.BlockSpec(memory_space=pl.ANY)],
        out_specs=[pl.BlockSpec((tc, d), lambda i: (jnp.minimum(prev(i), n_ptiles - 1), 0)),
                   pl.BlockSpec((tc, d), lambda i: (jnp.maximum(prev(i) - n_ptiles, 0), 0))],
        out_shape=[jax.ShapeDtypeStruct((n_prompt, d), F32),
                   jax.ShapeDtypeStruct((n_tok - n_prompt, d), F32)],
        scratch_shapes=[pltpu.VMEM((2, TOP_K * tc * row8, LANES), F32), pltpu.SemaphoreType.DMA((2,))],
        compiler_params=_cparams(("arbitrary",)),
        name="combine",
    )(dest_tiles, x2, top_w, yb)


def kernel(x_prompt, x_sample, cache_win_k, cache_win_v, state_conv, state_ssm, cache_mem_k, cache_mem_v,
           mem_prompt, norm_mix, w_in, q_gain, k_gain, conv_w, conv_b, dt_bias, a_log, d_skip, ssm_out_gain,
           w_out, norm_mem, mem_in_gain, w_mem_q, w_mem_k, w_mem_v, mem_q_gain, mem_k_gain, w_mem_o,
           norm_ffn, w_router, b_router, w_gate_up, b_gate_up, w_down, b_down):
    bp, lp, d = x_prompt.shape
    bs, ls, _ = x_sample.shape
    depth = norm_mix.shape[0]
    n_buf = cache_win_k.shape[2]
    past_len = PAST_LEN
    n_mem = mem_prompt.shape[1]
    npr, nsm = bp * lp, bs * ls
    att_w = d // 2
    n_heads = att_w // ATT_HEAD_DIM
    ssm_w = d - att_w
    conv_ch = ssm_w + 2 * SSM_GROUPS * SSM_STATE
    tail = CONV_WIDTH - 1
    keep = min(max(w for w, _ in DILATIONS), lp)
    ls_pad = SUBLANES

    assert keep == lp
    xp_out, xs_out = x_prompt.reshape(npr, d).astype(F32), x_sample.reshape(nsm, d).astype(F32)
    outs = [[] for _ in range(10)]
    for i in range(depth):
        x = (xp_out, xs_out)
        q, k, v, z, xbc, dt_raw, kt_p, vt_p = _projections(*x, norm_mix[i], w_in[i], q_gain[i], k_gain[i],
                                                           lp, ls, past_len)
        smp = lambda a: a[npr:].reshape(bs, ls, a.shape[-1])
        pad_s = lambda a: jnp.pad(smp(a), ((0, 0), (0, ls_pad - ls), (0, 0)))
        pad_rows = lambda a: pad_s(a).reshape(bs * ls_pad, a.shape[-1])
        unpad = lambda a: a.reshape(bs, ls_pad, a.shape[-1])[:, :ls].reshape(nsm, a.shape[-1])
        att_p = _attn_prompt(q, k, v, bp, lp)
        heads = lambda a, b, l: a.reshape(b, l, n_heads, ATT_HEAD_DIM)
        k_new, v_new = heads(k[npr:], bs, ls), heads(v[npr:], bs, ls)
        cache_t = lambda c: jnp.transpose(c, (0, 2, 3, 1)).reshape(bs, att_w, n_buf)
        att_s = _attn_sample(smp(q), smp(k), smp(v), cache_t(cache_win_k[i]), cache_t(cache_win_v[i]))
        ssm_par = (conv_w[i], conv_b[i], dt_bias[i], a_log[i], d_skip[i], ssm_out_gain[i])
        y_p, st_p, cs_p = _ssd(xbc, dt_raw, z, jnp.zeros((bp, tail, conv_ch), F32),
                               jnp.zeros((bp, ssm_w // SSM_HEAD_DIM, SSM_HEAD_DIM, SSM_STATE), F32), *ssm_par,
                               bp, lp, lp)
        y_s, st_s, cs_s = _ssd(pad_rows(xbc), pad_rows(dt_raw), pad_rows(z), state_conv[i], state_ssm[i],
                               *ssm_par, bs, ls_pad, ls)
        x1, qm = _out_proj(x, (att_p, att_s.reshape(nsm, att_w)), (y_p, unpad(y_s)),
                           w_out[i], norm_mem[i], w_mem_q[i], mem_q_gain[i])
        mk_p, mv_p = _mem_kv(mem_prompt.reshape(bp * n_mem, d).astype(F32), mem_in_gain[i], w_mem_k[i],
                             w_mem_v[i], mem_k_gain[i])
        mw = mk_p.shape[-1]
        o_p = _mem_attn(qm, mk_p.reshape(bp, n_mem, mw), mv_p.reshape(bp, n_mem, mw), bp, lp, TOKEN_TILE)
        o_s = _mem_attn(pad_rows(qm), cache_mem_k[i], cache_mem_v[i], bs, ls_pad, ls_pad)
        x2, hf, top_idx, top_w = _post(x1, (o_p, unpad(o_s)), w_mem_o[i], norm_ffn[i], w_router[i], b_router[i])
        xp_out, xs_out = _moe(hf, x2, top_idx, top_w, w_gate_up[i], b_gate_up[i], w_down[i], b_down[i], npr)

        untr = lambda a: jnp.transpose(a.reshape(bp, n_heads, ATT_HEAD_DIM, lp), (0, 3, 1, 2))
        new = (untr(kt_p), untr(vt_p),
               k_new, v_new,
               cs_p, cs_s,
               st_p, st_s,
               mk_p.reshape(bp, n_mem, mw // MEM_HEAD_DIM, MEM_HEAD_DIM),
               mv_p.reshape(bp, n_mem, mw // MEM_HEAD_DIM, MEM_HEAD_DIM))
        for lst, val in zip(outs, new):
            lst.append(val)
    y_p = xp_out.reshape(bp, lp, d).astype(x_prompt.dtype)
    y_s = xs_out.reshape(bs, ls, d).astype(x_sample.dtype)
    return (y_p, y_s) + tuple(jnp.stack(o) for o in outs)
```
